```python
import math
import jax, jax.numpy as jnp
from jax import lax
import numpy as np

D_MODEL = 1024
BATCH = 8
SEQ = 2048
DEPTH = 1

D_MIX = D_MODEL
RWKV_WIDTH = D_MIX // 2
RWKV_HEAD = 64
RWKV_HEADS = RWKV_WIDTH // RWKV_HEAD
DECAY_LORA = 64
ICLR_LORA = 64
RWKV_GN_EPS = 64e-5
DIFF_WIDTH = D_MIX - RWKV_WIDTH
DIFF_HEAD_V = 128
DIFF_HEAD_QK = DIFF_HEAD_V // 2
DIFF_HEADS = DIFF_WIDTH // DIFF_HEAD_V
DIFF_QK_WIDTH = DIFF_HEADS * 2 * DIFF_HEAD_QK
Q_BLOCK = 128
NORM_EPS = 1e-6
QK_NORM_EPS = 1e-6
SUBLN_EPS = 1e-5
NEG_INF = -1e30

SHIFT_COLS = 3 * RWKV_WIDTH + DECAY_LORA + ICLR_LORA
SHIFT_SPLIT = (RWKV_WIDTH, 2 * RWKV_WIDTH, 3 * RWKV_WIDTH, 3 * RWKV_WIDTH + DECAY_LORA)
REST_SPLIT = (RWKV_WIDTH, RWKV_WIDTH + DIFF_QK_WIDTH, RWKV_WIDTH + 2 * DIFF_QK_WIDTH,
              RWKV_WIDTH + 2 * DIFF_QK_WIDTH + DIFF_WIDTH)
IN_COLS = SHIFT_COLS + RWKV_WIDTH + 2 * DIFF_QK_WIDTH + 2 * DIFF_WIDTH

kernel_name = "hymba_rwkv7_diffattn_alibi_block"


def rms_norm(x, g, eps):
    xf = x.astype(jnp.float32)
    y = xf * lax.rsqrt(jnp.mean(xf * xf, axis=-1, keepdims=True) + eps)
    return (y * g.astype(jnp.float32)).astype(x.dtype)


def token_shift(p, mu):
    prev = jnp.pad(p, ((0, 0), (1, 0), (0, 0)))[:, :-1]
    return p + (prev - p) * mu


def alibi_slopes(n):
    return jnp.asarray([2.0 ** (-8.0 * (h + 1) / n) for h in range(n)], jnp.float32)


def rwkv7_recurrence(r, w, k, v, a, b):
    B, T, H, N = r.shape

    def step(S, inp):
        r_t, w_t, k_t, v_t, a_t, b_t = inp
        sa = jnp.einsum('bhij,bhj->bhi', S, a_t)
        S = S * w_t[:, :, None, :] + sa[..., None] * b_t[:, :, None, :] + v_t[..., None] * k_t[:, :, None, :]
        y = jnp.einsum('bhij,bhj->bhi', S, r_t)
        return S, y

    xs = tuple(jnp.moveaxis(t, 1, 0) for t in (r, w, k, v, a, b))
    S0 = jnp.zeros((B, H, N, N), jnp.float32)
    _, y = lax.scan(step, S0, xs)
    return jnp.moveaxis(y, 0, 1)


def rwkv7_branch(r, k, v, zw, za, w0, w_decay_up, a0, w_iclr_up, k_k, k_a, r_k, ln_w, ln_b):
    B, T, _ = r.shape
    f32 = jnp.float32
    hn = (RWKV_HEADS, RWKV_HEAD)
    heads = lambda t: t.astype(f32).reshape(B, T, RWKV_HEADS, RWKV_HEAD)
    w = -jax.nn.softplus(-(w0.astype(f32) + jnp.tanh(zw.astype(f32)) @ w_decay_up.astype(f32))) - 0.5
    decay = jnp.exp(-jnp.exp(w))
    a = jax.nn.sigmoid(a0.astype(f32) + za.astype(f32) @ w_iclr_up.astype(f32))
    r, k, v, decay, a = heads(r), heads(k), heads(v), heads(decay), heads(a)
    kk = k * k_k.astype(f32).reshape(hn)
    kk = kk / jnp.maximum(jnp.sqrt(jnp.sum(kk * kk, axis=-1, keepdims=True)), 1e-12)
    k = k * (1.0 + (a - 1.0) * k_a.astype(f32).reshape(hn))
    y = rwkv7_recurrence(r, decay, k, v, -kk, kk * a)
    mu = jnp.mean(y, axis=-1, keepdims=True)
    var = jnp.mean(jnp.square(y - mu), axis=-1, keepdims=True)
    y = (y - mu) * lax.rsqrt(var + RWKV_GN_EPS) * ln_w.astype(f32).reshape(hn) + ln_b.astype(f32).reshape(hn)
    y = y + jnp.sum(r * k * r_k.astype(f32), axis=-1, keepdims=True) * v
    return y.reshape(B, T, RWKV_WIDTH)


def diff_attention_branch(q, k, v, q_norm_g, k_norm_g, lq1, lk1, lq2, lk2, subln_g, lambda_init):
    B, T, _ = q.shape
    f32 = jnp.float32
    q = rms_norm(q.reshape(B, T, DIFF_HEADS, 2, DIFF_HEAD_QK), q_norm_g, QK_NORM_EPS)
    k = rms_norm(k.reshape(B, T, DIFF_HEADS, 2, DIFF_HEAD_QK), k_norm_g, QK_NORM_EPS)
    q = q.transpose(0, 2, 3, 1, 4)
    k = k.transpose(0, 2, 3, 1, 4)
    v = v.reshape(B, T, DIFF_HEADS, DIFF_HEAD_V).transpose(0, 2, 1, 3)
    lam = (jnp.exp(jnp.sum(lq1.astype(f32) * lk1.astype(f32)))
           - jnp.exp(jnp.sum(lq2.astype(f32) * lk2.astype(f32))) + lambda_init)
    slopes = alibi_slopes(DIFF_HEADS)
    scale = DIFF_HEAD_QK ** -0.5
    outs = []
    for blk in range(T // Q_BLOCK):
        q0, q1 = blk * Q_BLOCK, (blk + 1) * Q_BLOCK
        s = jnp.einsum('bhmqd,bhmkd->bhmqk', q[:, :, :, q0:q1], k[:, :, :, :q1]).astype(f32) * scale
        dist = jnp.arange(q0, q1)[:, None] - jnp.arange(q1)[None, :]
        s = s - slopes[None, :, None, None, None] * dist.astype(f32)
        s = jnp.where(dist >= 0, s, NEG_INF)
        p = jax.nn.softmax(s, axis=-1)
        attn = p[:, :, 0] - lam * p[:, :, 1]
        outs.append(jnp.einsum('bhqk,bhkd->bhqd', attn.astype(v.dtype), v[:, :, :q1]))
    o = jnp.concatenate(outs, axis=2).transpose(0, 2, 1, 3)
    o = rms_norm(o, subln_g, SUBLN_EPS).astype(f32) * (1.0 - lambda_init)
    return o.reshape(B, T, DIFF_WIDTH)


def setup_inputs(seed: int = 0) -> dict:
    key = jax.random.key(seed)
    ks = jax.random.split(key, 21)
    f = jnp.float32
    L = DEPTH

    def nrm(k, shape, s):
        return s * jax.random.normal(k, shape, f)

    ramp = (jnp.arange(RWKV_WIDTH, dtype=f) / (RWKV_WIDTH - 1)) ** 0.9
    return {
        "x": jax.random.normal(ks[0], (BATCH, SEQ, D_MODEL), f),
        "norm_g": 1.0 + nrm(ks[1], (L, D_MODEL), 0.02),
        "w_in": nrm(ks[2], (L, D_MODEL, IN_COLS), D_MODEL ** -0.5),
        "shift_mu": jax.random.uniform(ks[3], (L, SHIFT_COLS), f),
        "w0": -6.0 + 5.0 * ramp[None, :] + nrm(ks[4], (L, RWKV_WIDTH), 0.1),
        "w_decay_up": nrm(ks[5], (L, DECAY_LORA, RWKV_WIDTH), 0.1),
        "a0": nrm(ks[6], (L, RWKV_WIDTH), 0.1),
        "w_iclr_up": nrm(ks[7], (L, ICLR_LORA, RWKV_WIDTH), 0.1),
        "k_k": 0.85 + nrm(ks[8], (L, RWKV_WIDTH), 0.05),
        "k_a": 1.0 + nrm(ks[9], (L, RWKV_WIDTH), 0.05),
        "r_k": nrm(ks[10], (L, RWKV_HEADS, RWKV_HEAD), 0.1),
        "ln_x_w": 1.0 + nrm(ks[11], (L, RWKV_WIDTH), 0.02),
        "ln_x_b": nrm(ks[12], (L, RWKV_WIDTH), 0.02),
        "q_norm_g": 1.0 + nrm(ks[13], (L, DIFF_HEAD_QK), 0.02),
        "k_norm_g": 1.0 + nrm(ks[14], (L, DIFF_HEAD_QK), 0.02),
        "lambda_q1": nrm(ks[15], (L, DIFF_HEAD_QK), 0.1),
        "lambda_k1": nrm(ks[16], (L, DIFF_HEAD_QK), 0.1),
        "lambda_q2": nrm(ks[17], (L, DIFF_HEAD_QK), 0.1),
        "lambda_k2": nrm(ks[18], (L, DIFF_HEAD_QK), 0.1),
        "subln_g": 1.0 + nrm(ks[19], (L, DIFF_HEAD_V), 0.02),
        "w_out": nrm(ks[20], (L, D_MIX, D_MODEL), D_MIX ** -0.5),
    }


def reference(x, norm_g, w_in, shift_mu, w0, w_decay_up, a0, w_iclr_up, k_k, k_a, r_k,
              ln_x_w, ln_x_b, q_norm_g, k_norm_g, lambda_q1, lambda_k1, lambda_q2, lambda_k2,
              subln_g, w_out):
    f32 = jnp.float32
    h = x
    for l in range(DEPTH):
        hn = rms_norm(h, norm_g[l], NORM_EPS)
        proj = jnp.einsum('btd,de->bte', hn, w_in[l])
        shifted = token_shift(proj[..., :SHIFT_COLS], shift_mu[l])
        r, k, v, zw, za = jnp.split(shifted, SHIFT_SPLIT, axis=-1)
        g_a, dq, dk, dv, g_b = jnp.split(proj[..., SHIFT_COLS:], REST_SPLIT, axis=-1)
        lambda_init = 0.8 - 0.6 * math.exp(-0.3 * l)
        y_a = rwkv7_branch(r, k, v, zw, za, w0[l], w_decay_up[l], a0[l], w_iclr_up[l],
                           k_k[l], k_a[l], r_k[l], ln_x_w[l], ln_x_b[l])
        y_b = diff_attention_branch(dq, dk, dv, q_norm_g[l], k_norm_g[l], lambda_q1[l], lambda_k1[l],
                                    lambda_q2[l], lambda_k2[l], subln_g[l], lambda_init)
        mixed = jnp.concatenate([y_a * jax.nn.silu(g_a.astype(f32)),
                                 y_b * jax.nn.silu(g_b.astype(f32))], axis=-1)
        h = h + jnp.einsum('bte,ed->btd', mixed.astype(h.dtype), w_out[l])
    return h
```

```python
import functools
import math

import jax
import jax.numpy as jnp
from jax import lax
from jax.experimental import pallas as pl
from jax.experimental.pallas import tpu as pltpu

F32 = jnp.float32
BF16 = jnp.bfloat16
HIGHEST = lax.Precision.HIGHEST

LANES = 128
HEAD = 64
CHUNK = 64
RWKV_W = 512
DIFF_W = 512
LORA = 64
SHIFT_COLS = 3 * RWKV_W + 2 * LORA
NORM_EPS = 1e-6
QK_NORM_EPS = 1e-6
SUBLN_EPS = 1e-5
RWKV_GN_EPS = 64e-5
LAMBDA_INIT = 0.8 - 0.6 * math.exp(-0.3 * 0)
NEG_INF = -1e30
VMEM_LIMIT = 56 * 1024 * 1024


def _dot(a, b):
    return jnp.dot(a.astype(BF16), b.astype(BF16), preferred_element_type=F32)


def _dot_nt(a, b):
    return lax.dot_general(a.astype(BF16), b.astype(BF16), (((1,), (1,)), ((), ())),
                           preferred_element_type=F32)


def _dot_f32(a, b):
    return jnp.dot(a, b, precision=HIGHEST, preferred_element_type=F32)


def _sigmoid(x):
    return 1.0 / (1.0 + jnp.exp(-x))


def _in_proj_kernel(x_ref, g_ref, w_ref, mu_ref, qg_ref, kg_ref, ones_ref,
                    r_ref, k_ref, v_ref, zwa_ref, ga_ref, dq_ref, dk_ref, dv_ref, gb_ref,
                    hn_ref, carry_ref):
    t = pl.program_id(1)
    x = x_ref[0]
    tm = x.shape[0]
    ms = jnp.mean(x * x, axis=-1, keepdims=True)
    hn_ref[...] = (x * lax.rsqrt(ms + NORM_EPS) * g_ref[...]).astype(BF16)

    @pl.when(t == 0)
    def _():
        carry_ref[...] = jnp.zeros_like(carry_ref)

    row0 = lax.broadcasted_iota(jnp.int32, (tm, 1), 0) == 0

    def proj(c0, c1):
        return jnp.dot(hn_ref[...], w_ref[:, c0:c1], preferred_element_type=F32)

    def shifted(c0, c1):
        p = proj(c0, c1)
        prev = jnp.where(row0, carry_ref[:, c0:c1], pltpu.roll(p, 1, 0))
        carry_ref[:, c0:c1] = p[tm - 1:tm, :]
        return p + (prev - p) * mu_ref[:, c0:c1]

    r_ref[0] = shifted(0, RWKV_W)
    k_ref[0] = shifted(RWKV_W, 2 * RWKV_W)
    v_ref[0] = shifted(2 * RWKV_W, 3 * RWKV_W)
    zwa_ref[0] = shifted(3 * RWKV_W, SHIFT_COLS)

    c = SHIFT_COLS
    g = proj(c, c + RWKV_W)
    ga_ref[0] = (g * _sigmoid(g)).astype(BF16)
    c += RWKV_W

    def qk_norm(p, gain):
        ms = jnp.dot((p * p).astype(BF16), ones_ref[...],
                     preferred_element_type=F32) * (1.0 / HEAD)
        return (p * lax.rsqrt(ms + QK_NORM_EPS) * gain).astype(BF16)

    dq_ref[0] = qk_norm(proj(c, c + DIFF_W), qg_ref[...] * (HEAD ** -0.5))
    c += DIFF_W
    dk_ref[0] = qk_norm(proj(c, c + DIFF_W), kg_ref[...])
    c += DIFF_W
    dv_ref[0] = proj(c, c + DIFF_W).astype(BF16)
    c += DIFF_W
    g = proj(c, c + DIFF_W)
    gb_ref[0] = (g * _sigmoid(g)).astype(BF16)


def _in_proj(x, norm_g, w_in, shift_mu, q_norm_g, k_norm_g, *, tm):
    B, T, D = x.shape
    in_cols = w_in.shape[1]
    n_rep = DIFF_W // HEAD
    qg = jnp.tile(q_norm_g.reshape(1, HEAD), (1, n_rep))
    kg = jnp.tile(k_norm_g.reshape(1, HEAD), (1, n_rep))
    row = lambda w: pl.BlockSpec((1, tm, w), lambda b, t: (b, t, 0))
    full = lambda a: pl.BlockSpec(a.shape, lambda b, t: (0,) * a.ndim)
    f32o = lambda w: jax.ShapeDtypeStruct((B, T, w), F32)
    bf16o = lambda w: jax.ShapeDtypeStruct((B, T, w), BF16)
    g2 = norm_g.reshape(1, D)
    mu2 = shift_mu.reshape(1, SHIFT_COLS)
    group = jnp.arange(DIFF_W) // HEAD
    ones_bd = (group[:, None] == group[None, :]).astype(BF16)
    return pl.pallas_call(
        _in_proj_kernel,
        grid=(B, T // tm),
        in_specs=[row(D), full(g2), full(w_in), full(mu2), full(qg), full(kg), full(ones_bd)],
        out_specs=[row(RWKV_W), row(RWKV_W), row(RWKV_W), row(2 * LORA), row(RWKV_W),
                   row(DIFF_W), row(DIFF_W), row(DIFF_W), row(DIFF_W)],
        out_shape=[f32o(RWKV_W), f32o(RWKV_W), f32o(RWKV_W), f32o(2 * LORA), bf16o(RWKV_W),
                   bf16o(DIFF_W), bf16o(DIFF_W), bf16o(DIFF_W), bf16o(DIFF_W)],
        scratch_shapes=[pltpu.VMEM((tm, D), BF16), pltpu.VMEM((1, SHIFT_COLS), F32)],
        compiler_params=pltpu.CompilerParams(
            dimension_semantics=("arbitrary", "arbitrary"), vmem_limit_bytes=VMEM_LIMIT),
        name="in_proj",
    )(x, g2, w_in, mu2, qg, kg, ones_bd)


def _rwkv_kernel(r_ref, k_ref, v_ref, zwa_ref, ga_ref, wd_ref, wa_ref, vec_ref,
                 o_ref, s_ref, y_ref):
    tb = r_ref.shape[1]
    n_chunks = tb // CHUNK
    P2 = 2 * CHUNK

    @pl.when(pl.program_id(2) == 0)
    def _():
        s_ref[...] = jnp.zeros_like(s_ref)

    w0 = vec_ref[0:1, :]
    a0 = vec_ref[1:2, :]
    k_k = vec_ref[2:3, :]
    k_a = vec_ref[3:4, :]
    r_k = vec_ref[4:5, :]
    ln_w = vec_ref[5:6, :]
    ln_b = vec_ref[6:7, :]

    ri = lax.broadcasted_iota(jnp.int32, (P2, LANES), 0)
    ci = lax.broadcasted_iota(jnp.int32, (P2, LANES), 1)
    same_head = (ri // HEAD) == (ci // HEAD)
    strict = same_head & ((ci % HEAD) < (ri % HEAD))
    incl = same_head & ((ci % HEAD) <= (ri % HEAD))
    eye = ri == ci
    ones_bd = jnp.where(same_head, 1.0, 0.0).astype(F32)
    li = lax.broadcasted_iota(jnp.int32, (CHUNK, CHUNK), 0)
    lj = lax.broadcasted_iota(jnp.int32, (CHUNK, CHUNK), 1)
    ltri = jnp.where(lj <= li, 1.0, 0.0).astype(F32)

    def group_sum(z):
        return _dot_f32(z, ones_bd)

    def stack(z):
        return jnp.where(same_head, jnp.concatenate([z, z], axis=0), 0.0)

    r = r_ref[0]
    k = k_ref[0]
    v = v_ref[0]
    zwa = zwa_ref[0]
    u = w0 + _dot_f32(jnp.tanh(zwa), wd_ref[...])
    lw = -math.exp(-0.5) * _sigmoid(u)
    a = _sigmoid(a0 + _dot_f32(zwa, wa_ref[...]))
    kkr = k * k_k
    kk = kkr / jnp.maximum(jnp.sqrt(group_sum(kkr * kkr)), 1e-12)
    kmod = k * (1.0 + (a - 1.0) * k_a)
    kka = kk * a

    for c in range(n_chunks):
        rows = slice(c * CHUNK, (c + 1) * CHUNK)
        lw_c = lw[rows]
        cs = _dot_f32(ltri, lw_c)
        e_pos = jnp.exp(cs)
        e_neg = jnp.exp(-cs)
        e_prev = jnp.exp(cs - lw_c)
        e_tot = e_pos[CHUNK - 1:CHUNK, :]
        al = stack(-kk[rows] * e_prev)
        rt = stack(r[rows] * e_pos)
        bt = kka[rows] * e_neg
        kt = kmod[rows] * e_neg
        vs = stack(v[rows])

        gram = _dot_nt(jnp.concatenate([al, rt], axis=0),
                       jnp.concatenate([bt, bt, kt, kt], axis=0))
        a_ab = jnp.where(strict, gram[:P2, :P2], 0.0)
        a_ak = jnp.where(strict, gram[:P2, P2:], 0.0)
        a_rb = jnp.where(incl, gram[P2:, :P2], 0.0)
        a_rk = jnp.where(incl, gram[P2:, P2:], 0.0)

        tm = jnp.where(eye, 1.0, a_ab)
        ap = a_ab
        for _ in range(int(math.log2(CHUNK)) - 1):
            ap = _dot(ap, ap)
            tm = tm + _dot(tm, ap)

        av = _dot(jnp.concatenate([a_ak, a_rk], axis=0), vs)
        wu = _dot(tm, jnp.concatenate([al, av[:P2]], axis=1))
        bh_t = stack(bt * e_tot).T
        kh_t = stack(kt * e_tot).T
        qb = _dot(jnp.concatenate([a_rb, bh_t], axis=0), wu)
        qh = rt + qb[:P2, :P2]
        y1 = qb[:P2, P2:] + av[P2:]
        gm = jnp.where(eye, e_tot, 0.0) + qb[P2:, :P2]
        hm = qb[P2:, P2:] + _dot(kh_t, vs)

        ys = _dot(jnp.concatenate([qh, gm], axis=0), s_ref[...])
        yst = ys[:P2] + y1
        y_ref[rows, :] = yst[:CHUNK] + yst[CHUNK:]
        s_ref[...] = ys[P2:] + hm

    y = y_ref[...]
    mu = group_sum(y) * (1.0 / HEAD)
    d = y - mu
    var = group_sum(d * d) * (1.0 / HEAD)
    yn = d * lax.rsqrt(var + RWKV_GN_EPS) * ln_w + ln_b
    bonus = group_sum(r * kmod * r_k) * v
    o_ref[0] = ((yn + bonus) * ga_ref[0].astype(F32)).astype(BF16)


def _rwkv(r, k, v, zwa, ga, w0, w_decay_up, a0, w_iclr_up, k_k, k_a, r_k, ln_w, ln_b, *, tb):
    B, T, W = r.shape
    n_pairs = W // LANES
    zeros = jnp.zeros((LORA, W), F32)
    wd = jnp.concatenate([w_decay_up, zeros], axis=0)
    wa = jnp.concatenate([zeros, w_iclr_up], axis=0)
    vec = jnp.stack([w0, a0, k_k, k_a, r_k.reshape(W), ln_w, ln_b, jnp.zeros((W,), F32)])
    blk = pl.BlockSpec((1, tb, LANES), lambda b, p, t: (b, t, p))
    return pl.pallas_call(
        _rwkv_kernel,
        grid=(B, n_pairs, T // tb),
        in_specs=[blk, blk, blk,
                  pl.BlockSpec((1, tb, 2 * LORA), lambda b, p, t: (b, t, 0)),
                  blk,
                  pl.BlockSpec((2 * LORA, LANES), lambda b, p, t: (0, p)),
                  pl.BlockSpec((2 * LORA, LANES), lambda b, p, t: (0, p)),
                  pl.BlockSpec((8, LANES), lambda b, p, t: (0, p))],
        out_specs=blk,
        out_shape=jax.ShapeDtypeStruct((B, T, W), BF16),
        scratch_shapes=[pltpu.VMEM((LANES, LANES), F32), pltpu.VMEM((tb, LANES), F32)],
        compiler_params=pltpu.CompilerParams(
            dimension_semantics=("arbitrary", "arbitrary", "arbitrary"),
            vmem_limit_bytes=VMEM_LIMIT),
        name="rwkv7",
    )(r, k, v, zwa, ga, wd, wa, vec)


def _diff_attn_kernel(q_ref, k_ref, v_ref, gb_ref, slope_ref, subg_ref, lam_ref, o_ref):
    tq = q_ref.shape[1]
    i = pl.program_id(2)
    q = q_ref[0]
    lane = lax.broadcasted_iota(jnp.int32, (tq, LANES), 1)
    zero = jnp.zeros_like(q)
    qs = (jnp.where(lane < HEAD, q, zero), jnp.where(lane >= HEAD, q, zero))
    slope = slope_ref[0][:, 0:1]
    col = lax.broadcasted_iota(jnp.int32, (1, tq), 1)
    row = lax.broadcasted_iota(jnp.int32, (tq, 1), 0)

    def step(j, carry, causal):
        kj = k_ref[0, pl.ds(pl.multiple_of(j * tq, tq), tq), :]
        vj = v_ref[0, pl.ds(pl.multiple_of(j * tq, tq), tq), :]
        bias = slope * (col + (j - i) * tq).astype(F32)
        out = []
        for m in range(2):
            m_old, l_old, acc = carry[m]
            s = lax.dot_general(qs[m], kj, (((1,), (1,)), ((), ())),
                                preferred_element_type=F32) + bias
            if causal:
                s = jnp.where(col <= row, s, NEG_INF)
            m_new = jnp.maximum(m_old, jnp.max(s, axis=-1, keepdims=True))
            alpha = jnp.exp(m_old - m_new)
            p = jnp.exp(s - m_new)
            l_new = alpha * l_old + jnp.sum(p, axis=-1, keepdims=True)
            acc = alpha * acc + jnp.dot(p.astype(BF16), vj, preferred_element_type=F32)
            out.append((m_new, l_new, acc))
        return tuple(out)

    init_one = (jnp.full((tq, 1), NEG_INF, F32), jnp.zeros((tq, 1), F32),
                jnp.zeros((tq, LANES), F32))
    carry = lax.fori_loop(0, i, lambda j, c: step(j, c, False), (init_one, init_one))
    (_, l0, acc0), (_, l1, acc1) = step(i, carry, True)

    lam_p = lam_ref[...]
    s1 = jnp.sum(lam_p[0:1] * lam_p[1:2], axis=-1, keepdims=True)
    s2 = jnp.sum(lam_p[2:3] * lam_p[3:4], axis=-1, keepdims=True)
    lam = jnp.exp(s1) - jnp.exp(s2) + LAMBDA_INIT
    o = acc0 / l0 - lam * (acc1 / l1)
    ms = jnp.mean(o * o, axis=-1, keepdims=True)
    o = o * lax.rsqrt(ms + SUBLN_EPS) * subg_ref[...] * (1.0 - LAMBDA_INIT)
    o_ref[0] = (o * gb_ref[0].astype(F32)).astype(BF16)


def _diff_attn(dq, dk, dv, gb, subln_g, lam_params, *, tq):
    B, T, W = dq.shape
    n_heads = W // LANES
    slopes = jnp.asarray([2.0 ** (-8.0 * (h + 1) / n_heads) for h in range(n_heads)], F32)
    slopes = jnp.broadcast_to(slopes[:, None, None], (n_heads, 1, LANES))
    qblk = pl.BlockSpec((1, tq, LANES), lambda b, h, i: (b, i, h))
    kvblk = pl.BlockSpec((1, T, LANES), lambda b, h, i: (b, 0, h))
    subg = subln_g.reshape(1, LANES)
    return pl.pallas_call(
        _diff_attn_kernel,
        grid=(B, n_heads, T // tq),
        in_specs=[qblk, kvblk, kvblk, qblk,
                  pl.BlockSpec((1, 1, LANES), lambda b, h, i: (h, 0, 0)),
                  pl.BlockSpec((1, LANES), lambda b, h, i: (0, 0)),
                  pl.BlockSpec(lam_params.shape, lambda b, h, i: (0, 0))],
        out_specs=qblk,
        out_shape=jax.ShapeDtypeStruct((B, T, W), BF16),
        compiler_params=pltpu.CompilerParams(
            dimension_semantics=("arbitrary", "arbitrary", "arbitrary"),
            vmem_limit_bytes=VMEM_LIMIT),
        name="diffattn",
    )(dq, dk, dv, gb, slopes, subg, lam_params)


def _out_proj_kernel(ya_ref, yb_ref, x_ref, w_ref, o_ref):
    wa = ya_ref.shape[2]
    acc = jnp.dot(ya_ref[0], w_ref[:wa, :], preferred_element_type=F32)
    acc = acc + jnp.dot(yb_ref[0], w_ref[wa:, :], preferred_element_type=F32)
    o_ref[0] = x_ref[0] + acc


def _out_proj(ya, yb, x, w_out, *, tm):
    B, T, D = x.shape
    row = lambda w: pl.BlockSpec((1, tm, w), lambda b, t: (b, t, 0))
    return pl.pallas_call(
        _out_proj_kernel,
        grid=(B, T // tm),
        in_specs=[row(ya.shape[2]), row(yb.shape[2]), row(D),
                  pl.BlockSpec(w_out.shape, lambda b, t: (0, 0))],
        out_specs=row(D),
        out_shape=jax.ShapeDtypeStruct((B, T, D), F32),
        compiler_params=pltpu.CompilerParams(
            dimension_semantics=("arbitrary", "arbitrary"), vmem_limit_bytes=VMEM_LIMIT),
        name="out_proj",
    )(ya, yb, x, w_out)


def kernel(x, norm_g, w_in, shift_mu, w0, w_decay_up, a0, w_iclr_up, k_k, k_a, r_k, ln_x_w, ln_x_b, q_norm_g, k_norm_g, lambda_q1, lambda_k1, lambda_q2, lambda_k2, subln_g, w_out):
    depth = norm_g.shape[0]
    assert depth == 1, "lambda_init is specialised to a single layer"
    h = x
    for l in range(depth):
        r, k, v, zwa, ga, dq, dk, dv, gb = _in_proj(
            h, norm_g[l], w_in[l].astype(BF16), shift_mu[l], q_norm_g[l], k_norm_g[l], tm=512)
        ya = _rwkv(r, k, v, zwa, ga, w0[l], w_decay_up[l], a0[l], w_iclr_up[l], k_k[l], k_a[l],
                   r_k[l], ln_x_w[l], ln_x_b[l], tb=256)
        lam_params = jnp.stack([lambda_q1[l], lambda_k1[l], lambda_q2[l], lambda_k2[l]])
        yb = _diff_attn(dq, dk, dv, gb, subln_g[l], lam_params, tq=256)
        h = _out_proj(ya, yb, h, w_out[l].astype(BF16), tm=512)
    return h
```

```python
import functools
import math

import jax
import jax.numpy as jnp
from jax import lax
from jax.experimental import pallas as pl
from jax.experimental.pallas import tpu as pltpu

F32 = jnp.float32
BF16 = jnp.bfloat16

LANES = 128
HEAD = 64
CHUNK = 64
RWKV_W = 512
DIFF_W = 512
LORA = 64
SHIFT_COLS = 3 * RWKV_W + 2 * LORA
NORM_EPS = 1e-6
QK_NORM_EPS = 1e-6
SUBLN_EPS = 1e-5
RWKV_GN_EPS = 64e-5
LAMBDA_INIT = 0.8 - 0.6 * math.exp(-0.3 * 0)
NEG_INF = -1e30
VMEM_LIMIT = 56 * 1024 * 1024


def _dot(a, b):
    return jnp.dot(a.astype(BF16), b.astype(BF16), preferred_element_type=F32)


def _dot_nt(a, b):
    return lax.dot_general(a.astype(BF16), b.astype(BF16), (((1,), (1,)), ((), ())),
                           preferred_element_type=F32)


def _sigmoid(x):
    return 1.0 / (1.0 + jnp.exp(-x))


def _in_proj_kernel(x_ref, g_ref, w_ref, mu_ref, qg_ref, kg_ref, ones_ref,
                    r_ref, k_ref, v_ref, zwa_ref, ga_ref, dq_ref, dk_ref, dv_ref, gb_ref,
                    hn_ref, carry_ref):
    t = pl.program_id(1)
    x = x_ref[0]
    tm = x.shape[0]
    ms = jnp.mean(x * x, axis=-1, keepdims=True)
    hn_ref[...] = (x * lax.rsqrt(ms + NORM_EPS) * g_ref[...]).astype(BF16)

    @pl.when(t == 0)
    def _():
        carry_ref[...] = jnp.zeros_like(carry_ref)

    row0 = lax.broadcasted_iota(jnp.int32, (tm, 1), 0) == 0

    def proj(c0, c1):
        return jnp.dot(hn_ref[...], w_ref[:, c0:c1], preferred_element_type=F32)

    def shifted(c0, c1):
        p = proj(c0, c1)
        prev = jnp.where(row0, carry_ref[:, c0:c1], pltpu.roll(p, 1, 0))
        carry_ref[:, c0:c1] = p[tm - 1:tm, :]
        return p + (prev - p) * mu_ref[:, c0:c1]

    r_ref[0] = shifted(0, RWKV_W)
    k_ref[0] = shifted(RWKV_W, 2 * RWKV_W)
    v_ref[0] = shifted(2 * RWKV_W, 3 * RWKV_W)
    zwa_ref[0] = shifted(3 * RWKV_W, SHIFT_COLS)

    c = SHIFT_COLS
    g = proj(c, c + RWKV_W)
    ga_ref[0] = (g * _sigmoid(g)).astype(BF16)
    c += RWKV_W

    def qk_norm(p, gain):
        ms = jnp.dot((p * p).astype(BF16), ones_ref[...],
                     preferred_element_type=F32) * (1.0 / HEAD)
        return (p * lax.rsqrt(ms + QK_NORM_EPS) * gain).astype(BF16)

    dq_ref[0] = qk_norm(proj(c, c + DIFF_W), qg_ref[...] * (HEAD ** -0.5))
    c += DIFF_W
    dk_ref[0] = qk_norm(proj(c, c + DIFF_W), kg_ref[...])
    c += DIFF_W
    dv_ref[0] = proj(c, c + DIFF_W).astype(BF16)
    c += DIFF_W
    g = proj(c, c + DIFF_W)
    gb_ref[0] = (g * _sigmoid(g)).astype(BF16)


def _in_proj(x, norm_g, w_in, shift_mu, q_norm_g, k_norm_g, *, tm):
    B, T, D = x.shape
    n_rep = DIFF_W // HEAD
    qg = jnp.tile(q_norm_g.reshape(1, HEAD), (1, n_rep))
    kg = jnp.tile(k_norm_g.reshape(1, HEAD), (1, n_rep))
    row = lambda w: pl.BlockSpec((1, tm, w), lambda b, t: (b, t, 0))
    full = lambda a: pl.BlockSpec(a.shape, lambda b, t: (0,) * a.ndim)
    f32o = lambda w: jax.ShapeDtypeStruct((B, T, w), F32)
    bf16o = lambda w: jax.ShapeDtypeStruct((B, T, w), BF16)
    g2 = norm_g.reshape(1, D)
    mu2 = shift_mu.reshape(1, SHIFT_COLS)
    group = jnp.arange(DIFF_W) // HEAD
    ones_bd = (group[:, None] == group[None, :]).astype(BF16)
    return pl.pallas_call(
        _in_proj_kernel,
        grid=(B, T // tm),
        in_specs=[row(D), full(g2), full(w_in), full(mu2), full(qg), full(kg), full(ones_bd)],
        out_specs=[row(RWKV_W), row(RWKV_W), row(RWKV_W), row(2 * LORA), row(RWKV_W),
                   row(DIFF_W), row(DIFF_W), row(DIFF_W), row(DIFF_W)],
        out_shape=[f32o(RWKV_W), f32o(RWKV_W), f32o(RWKV_W), f32o(2 * LORA), bf16o(RWKV_W),
                   bf16o(DIFF_W), bf16o(DIFF_W), bf16o(DIFF_W), bf16o(DIFF_W)],
        scratch_shapes=[pltpu.VMEM((tm, D), BF16), pltpu.VMEM((1, SHIFT_COLS), F32)],
        compiler_params=pltpu.CompilerParams(
            dimension_semantics=("arbitrary", "arbitrary"), vmem_limit_bytes=VMEM_LIMIT),
        name="in_proj",
    )(x, g2, w_in, mu2, qg, kg, ones_bd)


def _rwkv_kernel(r_ref, k_ref, v_ref, zwa_ref, ga_ref, wd_ref, wa_ref, vec_ref,
                 o_ref, s_ref, y_ref):
    tb, width = r_ref.shape[1], r_ref.shape[2]
    n_chunks = tb // CHUNK
    n_pairs = width // LANES
    P2 = 2 * CHUNK

    @pl.when(pl.program_id(1) == 0)
    def _():
        s_ref[...] = jnp.zeros_like(s_ref)

    w0 = vec_ref[0:1, :]
    a0 = vec_ref[1:2, :]
    k_k = vec_ref[2:3, :]
    k_a = vec_ref[3:4, :]
    r_k = vec_ref[4:5, :]
    ln_w = vec_ref[5:6, :]
    ln_b = vec_ref[6:7, :]

    ri = lax.broadcasted_iota(jnp.int32, (P2, LANES), 0)
    ci = lax.broadcasted_iota(jnp.int32, (P2, LANES), 1)
    same_head = (ri // HEAD) == (ci // HEAD)
    strict = same_head & ((ci % HEAD) < (ri % HEAD))
    incl = same_head & ((ci % HEAD) <= (ri % HEAD))
    eye = ri == ci
    ones_bd = jnp.where(same_head, 1.0, 0.0).astype(BF16)
    lane_blocks = [slice(p * LANES, (p + 1) * LANES) for p in range(n_pairs)]

    def group_sum(z):
        hi = z.astype(BF16)
        lo = (z - hi.astype(F32)).astype(BF16)
        return jnp.concatenate(
            [jnp.dot(hi[:, lb], ones_bd, preferred_element_type=F32)
             + jnp.dot(lo[:, lb], ones_bd, preferred_element_type=F32) for lb in lane_blocks],
            axis=1)

    def stack(z):
        return jnp.where(same_head, jnp.concatenate([z, z], axis=0), 0.0)

    r = r_ref[0]
    k = k_ref[0]
    v = v_ref[0]
    zwa = zwa_ref[0]
    u = w0 + _dot(jnp.tanh(zwa), wd_ref[...])
    lw = -math.exp(-0.5) * _sigmoid(u)
    a = _sigmoid(a0 + _dot(zwa, wa_ref[...]))
    kkr = k * k_k
    kk = kkr / jnp.maximum(jnp.sqrt(group_sum(kkr * kkr)), 1e-12)
    kmod = k * (1.0 + (a - 1.0) * k_a)

    row_in_chunk = lax.broadcasted_iota(jnp.int32, (tb, 1), 0) % CHUNK
    cs = lw
    step = 1
    while step < CHUNK:
        cs = cs + jnp.where(row_in_chunk >= step, pltpu.roll(cs, step, 0), 0.0)
        step *= 2
    e_pos = jnp.exp(cs)
    e_neg = jnp.exp(-cs)
    al_full = -kk * jnp.exp(cs - lw)
    rt_full = r * e_pos
    bt_full = kk * a * e_neg
    kt_full = kmod * e_neg

    items = [(c, p) for c in range(n_chunks) for p in range(n_pairs)]
    blk = lambda z, c, p: z[c * CHUNK:(c + 1) * CHUNK, lane_blocks[p]]
    e_tot = {it: blk(e_pos, *it)[CHUNK - 1:CHUNK, :] for it in items}
    al = {it: stack(blk(al_full, *it)) for it in items}
    rt = {it: stack(blk(rt_full, *it)) for it in items}
    bt = {it: blk(bt_full, *it) for it in items}
    kt = {it: blk(kt_full, *it) for it in items}
    vs = {it: stack(blk(v, *it)) for it in items}

    gram = {it: _dot_nt(jnp.concatenate([al[it], rt[it]], axis=0),
                        jnp.concatenate([bt[it], bt[it], kt[it], kt[it]], axis=0))
            for it in items}
    a_ab = {it: jnp.where(strict, gram[it][:P2, :P2], 0.0) for it in items}
    a_kr = {it: jnp.concatenate([jnp.where(strict, gram[it][:P2, P2:], 0.0),
                                 jnp.where(incl, gram[it][P2:, P2:], 0.0)], axis=0)
            for it in items}
    a_rb = {it: jnp.where(incl, gram[it][P2:, :P2], 0.0) for it in items}

    tm = {it: jnp.where(eye, 1.0, a_ab[it]) for it in items}
    ap = a_ab
    for _ in range(int(math.log2(CHUNK)) - 1):
        ap = {it: _dot(ap[it], ap[it]) for it in items}
        tm = {it: tm[it] + _dot(tm[it], ap[it]) for it in items}

    av = {it: _dot(a_kr[it], vs[it]) for it in items}
    wu = {it: _dot(tm[it], jnp.concatenate([al[it], av[it][:P2]], axis=1))
          for it in items}
    bh_t = {it: stack(bt[it] * e_tot[it]).T for it in items}
    kh_t = {it: stack(kt[it] * e_tot[it]).T for it in items}
    qb = {it: _dot(jnp.concatenate([bh_t[it], a_rb[it]], axis=0), wu[it])
          for it in items}
    khv = {it: _dot(kh_t[it], vs[it]) for it in items}
    gq = {it: jnp.concatenate([jnp.where(eye, e_tot[it], 0.0) + qb[it][:P2, :P2],
                               rt[it] + qb[it][P2:, :P2]], axis=0) for it in items}
    hm = {it: qb[it][:P2, P2:] + khv[it] for it in items}
    y1 = {it: qb[it][P2:, P2:] + av[it][P2:] for it in items}

    for p in range(n_pairs):
        s = s_ref[p]
        for c in range(n_chunks):
            it = (c, p)
            ys = _dot(gq[it], s)
            yst = ys[P2:] + y1[it]
            y_ref[c * CHUNK:(c + 1) * CHUNK, lane_blocks[p]] = yst[:CHUNK] + yst[CHUNK:]
            s = ys[:P2] + hm[it]
        s_ref[p] = s

    y = y_ref[...]
    mu = group_sum(y) * (1.0 / HEAD)
    d = y - mu
    var = group_sum(d * d) * (1.0 / HEAD)
    yn = d * lax.rsqrt(var + RWKV_GN_EPS) * ln_w + ln_b
    bonus = group_sum(r * kmod * r_k) * v
    o_ref[0] = ((yn + bonus) * ga_ref[0].astype(F32)).astype(BF16)


def _rwkv(r, k, v, zwa, ga, w0, w_decay_up, a0, w_iclr_up, k_k, k_a, r_k, ln_w, ln_b, *, tb):
    B, T, W = r.shape
    n_pairs = W // LANES
    zeros = jnp.zeros((LORA, W), F32)
    wd = jnp.concatenate([w_decay_up, zeros], axis=0)
    wa = jnp.concatenate([zeros, w_iclr_up], axis=0)
    vec = jnp.stack([w0, a0, k_k, k_a, r_k.reshape(W), ln_w, ln_b, jnp.zeros((W,), F32)])
    blk = pl.BlockSpec((1, tb, W), lambda b, t: (b, t, 0))
    full = lambda a: pl.BlockSpec(a.shape, lambda b, t: (0,) * a.ndim)
    return pl.pallas_call(
        _rwkv_kernel,
        grid=(B, T // tb),
        in_specs=[blk, blk, blk,
                  pl.BlockSpec((1, tb, 2 * LORA), lambda b, t: (b, t, 0)),
                  blk, full(wd), full(wa), full(vec)],
        out_specs=blk,
        out_shape=jax.ShapeDtypeStruct((B, T, W), BF16),
        scratch_shapes=[pltpu.VMEM((n_pairs, LANES, LANES), F32), pltpu.VMEM((tb, W), F32)],
        compiler_params=pltpu.CompilerParams(
            dimension_semantics=("arbitrary", "arbitrary"), vmem_limit_bytes=VMEM_LIMIT),
        name="rwkv7",
    )(r, k, v, zwa, ga, wd, wa, vec)


def _diff_attn_kernel(q_ref, k_ref, v_ref, gb_ref, slope_ref, subg_ref, lam_ref, o_ref, *, tq):
    seq = q_ref.shape[1]
    n_tiles = seq // tq
    n_fold = tq // LANES
    lane = lax.broadcasted_iota(jnp.int32, (tq, LANES), 1)
    slope = slope_ref[0][:, 0:1]
    col = lax.broadcasted_iota(jnp.int32, (1, tq), 1)
    row = lax.broadcasted_iota(jnp.int32, (tq, 1), 0)
    causal = col <= row

    lam_p = lam_ref[...]
    s1 = jnp.sum(lam_p[0:1] * lam_p[1:2], axis=-1, keepdims=True)
    s2 = jnp.sum(lam_p[2:3] * lam_p[3:4], axis=-1, keepdims=True)
    lam = jnp.exp(s1) - jnp.exp(s2) + LAMBDA_INIT

    def fold(z, op):
        out = z[:, :LANES]
        for c in range(1, n_fold):
            out = op(out, z[:, c * LANES:(c + 1) * LANES])
        return out

    for i in range(n_tiles):
        q = q_ref[0, i * tq:(i + 1) * tq, :]
        zero = jnp.zeros_like(q)
        heads_out = []
        for m in range(2):
            qm = jnp.where((lane < HEAD) if m == 0 else (lane >= HEAD), q, zero)

            def scores(j, diag, qm=qm, i=i):
                kj = k_ref[0, pl.ds(pl.multiple_of(j * tq, tq), tq), :]
                bias = slope * (col + (j - i) * tq).astype(F32)
                s = lax.dot_general(qm, kj, (((1,), (1,)), ((), ())),
                                    preferred_element_type=F32) + bias
                return jnp.where(causal, s, NEG_INF) if diag else s

            def max_step(j, mx):
                return jnp.maximum(mx, fold(scores(j, False), jnp.maximum))

            mx = lax.fori_loop(0, i, max_step, jnp.full((tq, LANES), NEG_INF, F32))
            mx = jnp.maximum(mx, fold(scores(i, True), jnp.maximum))
            mb = jnp.broadcast_to(jnp.max(mx, axis=-1, keepdims=True), (tq, LANES))

            def acc_step(j, carry, diag=False, mb=mb):
                lsum, acc = carry
                s = scores(j, diag)
                ps = [jnp.exp(s[:, c * LANES:(c + 1) * LANES] - mb) for c in range(n_fold)]
                for pc in ps:
                    lsum = lsum + pc
                p = jnp.concatenate(ps, axis=1).astype(BF16)
                vj = v_ref[0, pl.ds(pl.multiple_of(j * tq, tq), tq), :]
                return lsum, acc + jnp.dot(p, vj, preferred_element_type=F32)

            zeros = jnp.zeros((tq, LANES), F32)
            carry = lax.fori_loop(0, i, acc_step, (zeros, zeros))
            lsum, acc = acc_step(i, carry, diag=True)
            heads_out.append(acc / jnp.sum(lsum, axis=-1, keepdims=True))

        o = heads_out[0] - lam * heads_out[1]
        ms = jnp.mean(o * o, axis=-1, keepdims=True)
        o = o * lax.rsqrt(ms + SUBLN_EPS) * subg_ref[...] * (1.0 - LAMBDA_INIT)
        o_ref[0, i * tq:(i + 1) * tq, :] = (
            o * gb_ref[0, i * tq:(i + 1) * tq, :].astype(F32)).astype(BF16)


def _diff_attn(dq, dk, dv, gb, subln_g, lam_params, *, tq):
    B, T, W = dq.shape
    n_heads = W // LANES
    slopes = jnp.asarray([2.0 ** (-8.0 * (h + 1) / n_heads) for h in range(n_heads)], F32)
    slopes = jnp.broadcast_to(slopes[:, None, None], (n_heads, 1, LANES))
    blk = pl.BlockSpec((1, T, LANES), lambda b, h: (b, 0, h))
    subg = subln_g.reshape(1, LANES)
    return pl.pallas_call(
        functools.partial(_diff_attn_kernel, tq=tq),
        grid=(B, n_heads),
        in_specs=[blk, blk, blk, blk,
                  pl.BlockSpec((1, 1, LANES), lambda b, h: (h, 0, 0)),
                  pl.BlockSpec((1, LANES), lambda b, h: (0, 0)),
                  pl.BlockSpec(lam_params.shape, lambda b, h: (0, 0))],
        out_specs=blk,
        out_shape=jax.ShapeDtypeStruct((B, T, W), BF16),
        compiler_params=pltpu.CompilerParams(
            dimension_semantics=("arbitrary", "arbitrary"), vmem_limit_bytes=VMEM_LIMIT),
        name="diffattn",
    )(dq, dk, dv, gb, slopes, subg, lam_params)


def _out_proj_kernel(ya_ref, yb_ref, x_ref, w_ref, o_ref):
    wa = ya_ref.shape[2]
    acc = jnp.dot(ya_ref[0], w_ref[:wa, :], preferred_element_type=F32)
    acc = acc + jnp.dot(yb_ref[0], w_ref[wa:, :], preferred_element_type=F32)
    o_ref[0] = x_ref[0] + acc


def _out_proj(ya, yb, x, w_out, *, tm):
    B, T, D = x.shape
    row = lambda w: pl.BlockSpec((1, tm, w), lambda b, t: (b, t, 0))
    return pl.pallas_call(
        _out_proj_kernel,
        grid=(B, T // tm),
        in_specs=[row(ya.shape[2]), row(yb.shape[2]), row(D),
                  pl.BlockSpec(w_out.shape, lambda b, t: (0, 0))],
        out_specs=row(D),
        out_shape=jax.ShapeDtypeStruct((B, T, D), F32),
        compiler_params=pltpu.CompilerParams(
            dimension_semantics=("arbitrary", "arbitrary"), vmem_limit_bytes=VMEM_LIMIT),
        name="out_proj",
    )(ya, yb, x, w_out)


def kernel(x, norm_g, w_in, shift_mu, w0, w_decay_up, a0, w_iclr_up, k_k, k_a, r_k, ln_x_w, ln_x_b, q_norm_g, k_norm_g, lambda_q1, lambda_k1, lambda_q2, lambda_k2, subln_g, w_out):
    depth = norm_g.shape[0]
    assert depth == 1, "lambda_init is specialised to a single layer"
    h = x
    for l in range(depth):
        r, k, v, zwa, ga, dq, dk, dv, gb = _in_proj(
            h, norm_g[l], w_in[l].astype(BF16), shift_mu[l], q_norm_g[l], k_norm_g[l], tm=512)
        ya = _rwkv(r, k, v, zwa, ga, w0[l], w_decay_up[l], a0[l], w_iclr_up[l], k_k[l], k_a[l],
                   r_k[l], ln_x_w[l], ln_x_b[l], tb=128)
        lam_params = jnp.stack([lambda_q1[l], lambda_k1[l], lambda_q2[l], lambda_k2[l]])
        yb = _diff_attn(dq, dk, dv, gb, subln_g[l], lam_params, tq=512)
        h = _out_proj(ya, yb, h, w_out[l].astype(BF16), tm=512)
    return h
```

```python
import functools
import math

import jax
import jax.numpy as jnp
from jax import lax
from jax.experimental import pallas as pl
from jax.experimental.pallas import tpu as pltpu

F32 = jnp.float32
BF16 = jnp.bfloat16

LANES = 128
SUBLANES = 8
HEAD = 64
CHUNK = 64
RWKV_W = 512
DIFF_W = 512
LORA = 64
SHIFT_COLS = 3 * RWKV_W + 2 * LORA
NORM_EPS = 1e-6
QK_NORM_EPS = 1e-6
SUBLN_EPS = 1e-5
RWKV_GN_EPS = 64e-5
LAMBDA_INIT = 0.8 - 0.6 * math.exp(-0.3 * 0)
NEG_INF = -1e30
LOG2E = math.log2(math.e)
VMEM_LIMIT = 56 * 1024 * 1024


def _dot(a, b):
    return jnp.dot(a.astype(BF16), b.astype(BF16), preferred_element_type=F32)


def _dot_nt(a, b):
    return lax.dot_general(a.astype(BF16), b.astype(BF16), (((1,), (1,)), ((), ())),
                           preferred_element_type=F32)


def _sigmoid(x):
    return 1.0 / (1.0 + jnp.exp(-x))


def _in_proj_kernel(x_ref, g_ref, w_ref, mu_ref, qg_ref, kg_ref, ones_ref,
                    r_ref, k_ref, v_ref, zwa_ref, ga_ref, dq_ref, dk_ref, dv_ref, gb_ref,
                    hn_ref, carry_ref):
    t = pl.program_id(1)
    x = x_ref[0]
    tm = x.shape[0]
    ms = jnp.mean(x * x, axis=-1, keepdims=True)
    hn_ref[...] = (x * lax.rsqrt(ms + NORM_EPS) * g_ref[...]).astype(BF16)

    @pl.when(t == 0)
    def _():
        carry_ref[...] = jnp.zeros_like(carry_ref)

    row0 = lax.broadcasted_iota(jnp.int32, (tm, 1), 0) == 0

    def proj(c0, c1):
        return jnp.dot(hn_ref[...], w_ref[:, c0:c1], preferred_element_type=F32)

    def shifted(c0, c1):
        p = proj(c0, c1)
        prev = jnp.where(row0, carry_ref[:, c0:c1], pltpu.roll(p, 1, 0))
        carry_ref[:, c0:c1] = p[tm - 1:tm, :]
        return p + (prev - p) * mu_ref[:, c0:c1]

    r_ref[0] = shifted(0, RWKV_W)
    k_ref[0] = shifted(RWKV_W, 2 * RWKV_W)
    v_ref[0] = shifted(2 * RWKV_W, 3 * RWKV_W)
    zwa_ref[0] = shifted(3 * RWKV_W, SHIFT_COLS)

    c = SHIFT_COLS
    g = proj(c, c + RWKV_W)
    ga_ref[0] = (g * _sigmoid(g)).astype(BF16)
    c += RWKV_W

    def qk_norm(p, gain):
        ms = jnp.dot((p * p).astype(BF16), ones_ref[...],
                     preferred_element_type=F32) * (1.0 / HEAD)
        return (p * lax.rsqrt(ms + QK_NORM_EPS) * gain).astype(BF16)

    dq_ref[0] = qk_norm(proj(c, c + DIFF_W), qg_ref[...] * (HEAD ** -0.5 * LOG2E))
    c += DIFF_W
    dk_ref[0] = qk_norm(proj(c, c + DIFF_W), kg_ref[...])
    c += DIFF_W
    dv_ref[0] = proj(c, c + DIFF_W).astype(BF16)
    c += DIFF_W
    g = proj(c, c + DIFF_W)
    gb_ref[0] = (g * _sigmoid(g)).astype(BF16)


def _in_proj(x, norm_g, w_in, shift_mu, q_norm_g, k_norm_g, *, tm):
    B, T, D = x.shape
    n_rep = DIFF_W // HEAD
    qg = jnp.tile(q_norm_g.reshape(1, HEAD), (1, n_rep))
    kg = jnp.tile(k_norm_g.reshape(1, HEAD), (1, n_rep))
    row = lambda w: pl.BlockSpec((1, tm, w), lambda b, t: (b, t, 0))
    full = lambda a: pl.BlockSpec(a.shape, lambda b, t: (0,) * a.ndim)
    f32o = lambda w: jax.ShapeDtypeStruct((B, T, w), F32)
    bf16o = lambda w: jax.ShapeDtypeStruct((B, T, w), BF16)
    g2 = norm_g.reshape(1, D)
    mu2 = shift_mu.reshape(1, SHIFT_COLS)
    group = jnp.arange(DIFF_W) // HEAD
    ones_bd = (group[:, None] == group[None, :]).astype(BF16)
    return pl.pallas_call(
        _in_proj_kernel,
        grid=(B, T // tm),
        in_specs=[row(D), full(g2), full(w_in), full(mu2), full(qg), full(kg), full(ones_bd)],
        out_specs=[row(RWKV_W), row(RWKV_W), row(RWKV_W), row(2 * LORA), row(RWKV_W),
                   row(DIFF_W), row(DIFF_W), row(DIFF_W), row(DIFF_W)],
        out_shape=[f32o(RWKV_W), f32o(RWKV_W), f32o(RWKV_W), f32o(2 * LORA), bf16o(RWKV_W),
                   bf16o(DIFF_W), bf16o(DIFF_W), bf16o(DIFF_W), bf16o(DIFF_W)],
        scratch_shapes=[pltpu.VMEM((tm, D), BF16), pltpu.VMEM((1, SHIFT_COLS), F32)],
        compiler_params=pltpu.CompilerParams(
            dimension_semantics=("arbitrary", "arbitrary"), vmem_limit_bytes=VMEM_LIMIT),
        name="in_proj",
    )(x, g2, w_in, mu2, qg, kg, ones_bd)


def _rwkv_kernel(r_ref, k_ref, v_ref, zwa_ref, ga_ref, wd_ref, wa_ref, vec_ref,
                 o_ref, s_ref, y_ref):
    tb, width = r_ref.shape[1], r_ref.shape[2]
    n_chunks = tb // CHUNK
    n_pairs = width // LANES
    P2 = 2 * CHUNK

    @pl.when(pl.program_id(1) == 0)
    def _():
        s_ref[...] = jnp.zeros_like(s_ref)

    w0 = vec_ref[0:1, :]
    a0 = vec_ref[1:2, :]
    k_k = vec_ref[2:3, :]
    k_a = vec_ref[3:4, :]
    r_k = vec_ref[4:5, :]
    ln_w = vec_ref[5:6, :]
    ln_b = vec_ref[6:7, :]

    ri = lax.broadcasted_iota(jnp.int32, (P2, LANES), 0)
    ci = lax.broadcasted_iota(jnp.int32, (P2, LANES), 1)
    same_head = (ri // HEAD) == (ci // HEAD)
    strict = same_head & ((ci % HEAD) < (ri % HEAD))
    incl = same_head & ((ci % HEAD) <= (ri % HEAD))
    eye = ri == ci
    ones_bd = jnp.where(same_head, 1.0, 0.0).astype(BF16)
    lane_blocks = [slice(p * LANES, (p + 1) * LANES) for p in range(n_pairs)]

    def group_sum(z):
        hi = z.astype(BF16)
        lo = (z - hi.astype(F32)).astype(BF16)
        return jnp.concatenate(
            [jnp.dot(hi[:, lb], ones_bd, preferred_element_type=F32)
             + jnp.dot(lo[:, lb], ones_bd, preferred_element_type=F32) for lb in lane_blocks],
            axis=1)

    def stack(z):
        return jnp.where(same_head, jnp.concatenate([z, z], axis=0), 0.0)

    r = r_ref[0]
    k = k_ref[0]
    v = v_ref[0]
    zwa = zwa_ref[0]
    u = w0 + _dot(jnp.tanh(zwa), wd_ref[...])
    lw = -math.exp(-0.5) * _sigmoid(u)
    a = _sigmoid(a0 + _dot(zwa, wa_ref[...]))
    kkr = k * k_k
    kk = kkr / jnp.maximum(jnp.sqrt(group_sum(kkr * kkr)), 1e-12)
    kmod = k * (1.0 + (a - 1.0) * k_a)

    row_in_chunk = lax.broadcasted_iota(jnp.int32, (tb, 1), 0) % CHUNK
    cs = lw
    step = 1
    while step < CHUNK:
        cs = cs + jnp.where(row_in_chunk >= step, pltpu.roll(cs, step, 0), 0.0)
        step *= 2
    e_pos = jnp.exp(cs)
    e_neg = jnp.exp(-cs)
    al_full = -kk * jnp.exp(cs - lw)
    rt_full = r * e_pos
    bt_full = kk * a * e_neg
    kt_full = kmod * e_neg

    items = [(c, p) for c in range(n_chunks) for p in range(n_pairs)]
    blk = lambda z, c, p: z[c * CHUNK:(c + 1) * CHUNK, lane_blocks[p]]
    e_tot = {it: blk(e_pos, *it)[CHUNK - 1:CHUNK, :] for it in items}
    al = {it: stack(blk(al_full, *it)) for it in items}
    rt = {it: stack(blk(rt_full, *it)) for it in items}
    bt = {it: blk(bt_full, *it) for it in items}
    kt = {it: blk(kt_full, *it) for it in items}
    vs = {it: stack(blk(v, *it)) for it in items}

    gram = {it: _dot_nt(jnp.concatenate([al[it], rt[it]], axis=0),
                        jnp.concatenate([bt[it], bt[it], kt[it], kt[it]], axis=0))
            for it in items}
    a_ab = {it: jnp.where(strict, gram[it][:P2, :P2], 0.0) for it in items}
    a_kr = {it: jnp.concatenate([jnp.where(strict, gram[it][:P2, P2:], 0.0),
                                 jnp.where(incl, gram[it][P2:, P2:], 0.0)], axis=0)
            for it in items}
    a_rb = {it: jnp.where(incl, gram[it][P2:, :P2], 0.0) for it in items}

    n_levels = int(math.log2(CHUNK))
    tm = {it: jnp.where(eye, 1.0, a_ab[it]) for it in items}
    ap = {it: _dot(a_ab[it], a_ab[it]) for it in items}
    for _ in range(1, n_levels - 1):
        pp = {it: _dot(ap[it], jnp.concatenate([ap[it], tm[it]], axis=1)) for it in items}
        ap = {it: pp[it][:, :P2] for it in items}
        tm = {it: tm[it] + pp[it][:, P2:] for it in items}
    tm = {it: tm[it] + _dot(ap[it], tm[it]) for it in items}

    av = {it: _dot(a_kr[it], vs[it]) for it in items}
    wu = {it: _dot(tm[it], jnp.concatenate([al[it], av[it][:P2]], axis=1))
          for it in items}
    bh_t = {it: stack(bt[it] * e_tot[it]).T for it in items}
    kh_t = {it: stack(kt[it] * e_tot[it]).T for it in items}
    qb = {it: _dot(jnp.concatenate([bh_t[it], a_rb[it]], axis=0), wu[it])
          for it in items}
    khv = {it: _dot(kh_t[it], vs[it]) for it in items}
    gq = {it: jnp.concatenate([jnp.where(eye, e_tot[it], 0.0) + qb[it][:P2, :P2],
                               rt[it] + qb[it][P2:, :P2]], axis=0) for it in items}
    hm = {it: qb[it][:P2, P2:] + khv[it] for it in items}
    y1 = {it: qb[it][P2:, P2:] + av[it][P2:] for it in items}

    for p in range(n_pairs):
        s = s_ref[p]
        for c in range(n_chunks):
            it = (c, p)
            ys = _dot(gq[it], s)
            yst = ys[P2:] + y1[it]
            y_ref[c * CHUNK:(c + 1) * CHUNK, lane_blocks[p]] = yst[:CHUNK] + yst[CHUNK:]
            s = ys[:P2] + hm[it]
        s_ref[p] = s

    y = y_ref[...]
    mu = group_sum(y) * (1.0 / HEAD)
    d = y - mu
    var = group_sum(d * d) * (1.0 / HEAD)
    yn = d * lax.rsqrt(var + RWKV_GN_EPS) * ln_w + ln_b
    bonus = group_sum(r * kmod * r_k) * v
    o_ref[0] = ((yn + bonus) * ga_ref[0].astype(F32)).astype(BF16)


def _rwkv(r, k, v, zwa, ga, w0, w_decay_up, a0, w_iclr_up, k_k, k_a, r_k, ln_w, ln_b, *, tb):
    B, T, W = r.shape
    n_pairs = W // LANES
    zeros = jnp.zeros((LORA, W), F32)
    wd = jnp.concatenate([w_decay_up, zeros], axis=0)
    wa = jnp.concatenate([zeros, w_iclr_up], axis=0)
    vec = jnp.stack([w0, a0, k_k, k_a, r_k.reshape(W), ln_w, ln_b, jnp.zeros((W,), F32)])
    blk = pl.BlockSpec((1, tb, W), lambda b, t: (b, t, 0))
    full = lambda a: pl.BlockSpec(a.shape, lambda b, t: (0,) * a.ndim)
    return pl.pallas_call(
        _rwkv_kernel,
        grid=(B, T // tb),
        in_specs=[blk, blk, blk,
                  pl.BlockSpec((1, tb, 2 * LORA), lambda b, t: (b, t, 0)),
                  blk, full(wd), full(wa), full(vec)],
        out_specs=blk,
        out_shape=jax.ShapeDtypeStruct((B, T, W), BF16),
        scratch_shapes=[pltpu.VMEM((n_pairs, LANES, LANES), F32), pltpu.VMEM((tb, W), F32)],
        compiler_params=pltpu.CompilerParams(
            dimension_semantics=("arbitrary", "arbitrary"), vmem_limit_bytes=VMEM_LIMIT),
        name="rwkv7",
    )(r, k, v, zwa, ga, wd, wa, vec)


def _diff_attn_kernel(q_ref, k_ref, v_ref, gb_ref, slope_ref, subg_ref, lam_ref, o_ref, vt_ref,
                      *, tq):
    seq = q_ref.shape[1]
    n_tiles = seq // tq
    kmax = 2 * tq
    nq = 2 * tq
    for r0 in range(0, seq, kmax):
        vt_ref[:, r0:r0 + kmax] = v_ref[0, r0:r0 + kmax, :].astype(F32).T.astype(BF16)

    lane = lax.broadcasted_iota(jnp.int32, (tq, LANES), 1)
    slope = slope_ref[0][:, 0:1] * LOG2E
    key_bias = {n: slope * lax.broadcasted_iota(jnp.int32, (n, nq), 0).astype(F32)
                for n in (tq, kmax)}
    tri = (lax.broadcasted_iota(jnp.int32, (tq, nq), 0)
           <= lax.broadcasted_iota(jnp.int32, (tq, nq), 1) % tq)

    lam_p = lam_ref[...]
    s1 = jnp.sum(lam_p[0:1] * lam_p[1:2], axis=-1, keepdims=True)
    s2 = jnp.sum(lam_p[2:3] * lam_p[3:4], axis=-1, keepdims=True)
    lam = jnp.exp(s1) - jnp.exp(s2) + LAMBDA_INIT

    def all_sublanes(z, op):
        shift = SUBLANES // 2
        while shift:
            z = op(z, pltpu.roll(z, shift, 0))
            shift //= 2
        return z

    for i in range(n_tiles):
        q = q_ref[0, i * tq:(i + 1) * tq, :]
        zero = jnp.zeros_like(q)
        qcat = jnp.concatenate([jnp.where(lane < HEAD, q, zero),
                                jnp.where(lane >= HEAD, q, zero)], axis=0)
        blocks = [(j * kmax, kmax, False) for j in range(i // 2)]
        if i % 2:
            blocks.append(((i - 1) * tq, tq, False))
        blocks.append((i * tq, tq, True))
        mx = jnp.full((SUBLANES, nq), NEG_INF, F32)
        ls = jnp.zeros((SUBLANES, nq), F32)
        acc = jnp.zeros((LANES // SUBLANES, SUBLANES, nq), F32)
        for k0, klen, causal in blocks:
            st = lax.dot_general(k_ref[0, k0:k0 + klen, :], qcat,
                                 (((1,), (1,)), ((), ())), preferred_element_type=F32)
            st = st + key_bias[klen]
            if causal:
                st = jnp.where(tri, st, NEG_INF)
            st = st.reshape(klen // SUBLANES, SUBLANES, nq)
            block_bias = slope * float(k0 - i * tq)
            m_new = jnp.maximum(mx, all_sublanes(jnp.max(st, axis=0), jnp.maximum) + block_bias)
            alpha = jnp.exp2(mx - m_new)
            p = jnp.exp2(st - (m_new - block_bias))
            ls = alpha * ls + all_sublanes(jnp.sum(p, axis=0), jnp.add)
            pv = jnp.dot(vt_ref[:, k0:k0 + klen], p.reshape(klen, nq).astype(BF16),
                         preferred_element_type=F32)
            acc = alpha * acc + pv.reshape(acc.shape)
            mx = m_new

        att = acc / ls
        ot = att[:, :, :tq] - lam * att[:, :, tq:]
        ms = all_sublanes(jnp.sum(ot * ot, axis=0), jnp.add) * (1.0 / LANES)
        ot = (ot * lax.rsqrt(ms + SUBLN_EPS)).reshape(LANES, tq)
        o = ot.T * (subg_ref[...] * (1.0 - LAMBDA_INIT))
        o_ref[0, i * tq:(i + 1) * tq, :] = (
            o * gb_ref[0, i * tq:(i + 1) * tq, :].astype(F32)).astype(BF16)


def _diff_attn(dq, dk, dv, gb, subln_g, lam_params, *, tq):
    B, T, W = dq.shape
    n_heads = W // LANES
    slopes = jnp.asarray([2.0 ** (-8.0 * (h + 1) / n_heads) for h in range(n_heads)], F32)
    slopes = jnp.broadcast_to(slopes[:, None, None], (n_heads, 1, LANES))
    blk = pl.BlockSpec((1, T, LANES), lambda b, h: (b, 0, h))
    subg = subln_g.reshape(1, LANES)
    return pl.pallas_call(
        functools.partial(_diff_attn_kernel, tq=tq),
        grid=(B, n_heads),
        in_specs=[blk, blk, blk, blk,
                  pl.BlockSpec((1, 1, LANES), lambda b, h: (h, 0, 0)),
                  pl.BlockSpec((1, LANES), lambda b, h: (0, 0)),
                  pl.BlockSpec(lam_params.shape, lambda b, h: (0, 0))],
        out_specs=blk,
        out_shape=jax.ShapeDtypeStruct((B, T, W), BF16),
        scratch_shapes=[pltpu.VMEM((LANES, T), BF16)],
        compiler_params=pltpu.CompilerParams(
            dimension_semantics=("arbitrary", "arbitrary"), vmem_limit_bytes=VMEM_LIMIT),
        name="diffattn",
    )(dq, dk, dv, gb, slopes, subg, lam_params)


def _out_proj_kernel(ya_ref, yb_ref, x_ref, w_ref, o_ref):
    wa = ya_ref.shape[2]
    acc = jnp.dot(ya_ref[0], w_ref[:wa, :], preferred_element_type=F32)
    acc = acc + jnp.dot(yb_ref[0], w_ref[wa:, :], preferred_element_type=F32)
    o_ref[0] = x_ref[0] + acc


def _out_proj(ya, yb, x, w_out, *, tm):
    B, T, D = x.shape
    row = lambda w: pl.BlockSpec((1, tm, w), lambda b, t: (b, t, 0))
    return pl.pallas_call(
        _out_proj_kernel,
        grid=(B, T // tm),
        in_specs=[row(ya.shape[2]), row(yb.shape[2]), row(D),
                  pl.BlockSpec(w_out.shape, lambda b, t: (0, 0))],
        out_specs=row(D),
        out_shape=jax.ShapeDtypeStruct((B, T, D), F32),
        compiler_params=pltpu.CompilerParams(
            dimension_semantics=("arbitrary", "arbitrary"), vmem_limit_bytes=VMEM_LIMIT),
        name="out_proj",
    )(ya, yb, x, w_out)


def kernel(x, norm_g, w_in, shift_mu, w0, w_decay_up, a0, w_iclr_up, k_k, k_a, r_k, ln_x_w, ln_x_b, q_norm_g, k_norm_g, lambda_q1, lambda_k1, lambda_q2, lambda_k2, subln_g, w_out):
    depth = norm_g.shape[0]
    assert depth == 1, "lambda_init is specialised to a single layer"
    h = x
    for l in range(depth):
        r, k, v, zwa, ga, dq, dk, dv, gb = _in_proj(
            h, norm_g[l], w_in[l].astype(BF16), shift_mu[l], q_norm_g[l], k_norm_g[l], tm=512)
        ya = _rwkv(r, k, v, zwa, ga, w0[l], w_decay_up[l], a0[l], w_iclr_up[l], k_k[l], k_a[l],
                   r_k[l], ln_x_w[l], ln_x_b[l], tb=128)
        lam_params = jnp.stack([lambda_q1[l], lambda_k1[l], lambda_q2[l], lambda_k2[l]])
        yb = _diff_attn(dq, dk, dv, gb, subln_g[l], lam_params, tq=256)
        h = _out_proj(ya, yb, h, w_out[l].astype(BF16), tm=512)
    return h
```

```python
import functools
import math

import jax
import jax.numpy as jnp
from jax import lax
from jax.experimental import pallas as pl
from jax.experimental.pallas import tpu as pltpu

F32 = jnp.float32
BF16 = jnp.bfloat16

LANES = 128
SUBLANES = 8
MXU_WIDTH = 256
HEAD = 64
CHUNK = 64
RWKV_W = 512
DIFF_W = 512
LORA = 64
SHIFT_COLS = 3 * RWKV_W + 2 * LORA
NORM_EPS = 1e-6
QK_NORM_EPS = 1e-6
SUBLN_EPS = 1e-5
RWKV_GN_EPS = 64e-5
LAMBDA_INIT = 0.8 - 0.6 * math.exp(-0.3 * 0)
NEG_INF = -1e30
LOG2E = math.log2(math.e)
KEY_POS_SPLIT = 16
VMEM_LIMIT = 56 * 1024 * 1024


def _dot(a, b):
    return jnp.dot(a.astype(BF16), b.astype(BF16), preferred_element_type=F32)


def _dot_nt(a, b):
    return lax.dot_general(a.astype(BF16), b.astype(BF16), (((1,), (1,)), ((), ())),
                           preferred_element_type=F32)


def _sigmoid(x):
    return 1.0 / (1.0 + jnp.exp(-x))


def _in_proj_kernel(x_ref, g_ref, w_ref, mu_ref, qg_ref, kg_ref, ones_ref,
                    r_ref, k_ref, v_ref, zwa_ref, ga_ref, dq_ref, dk_ref, dv_ref, gb_ref,
                    hn_ref, carry_ref):
    t = pl.program_id(1)
    x = x_ref[0]
    tm = x.shape[0]
    ms = jnp.mean(x * x, axis=-1, keepdims=True)
    hn_ref[...] = (x * lax.rsqrt(ms + NORM_EPS) * g_ref[...]).astype(BF16)

    @pl.when(t == 0)
    def _():
        carry_ref[...] = jnp.zeros_like(carry_ref)

    row0 = lax.broadcasted_iota(jnp.int32, (tm, 1), 0) == 0

    def proj(c0, c1):
        return jnp.dot(hn_ref[...], w_ref[:, c0:c1], preferred_element_type=F32)

    def shifted(c0, c1):
        p = proj(c0, c1)
        prev = jnp.where(row0, carry_ref[:, c0:c1], pltpu.roll(p, 1, 0))
        carry_ref[:, c0:c1] = p[tm - 1:tm, :]
        return p + (prev - p) * mu_ref[:, c0:c1]

    r_ref[0] = shifted(0, RWKV_W)
    k_ref[0] = shifted(RWKV_W, 2 * RWKV_W)
    v_ref[0] = shifted(2 * RWKV_W, 3 * RWKV_W)
    zwa_ref[0] = shifted(3 * RWKV_W, SHIFT_COLS)

    c = SHIFT_COLS
    g = proj(c, c + RWKV_W)
    ga_ref[0] = (g * _sigmoid(g)).astype(BF16)
    c += RWKV_W

    def qk_norm(p, gain):
        sq = (p * p).astype(BF16)
        w = ones_ref.shape[0]
        ms = jnp.concatenate(
            [jnp.dot(sq[:, c0:c0 + w], ones_ref[...], preferred_element_type=F32)
             for c0 in range(0, DIFF_W, w)], axis=1) * (1.0 / HEAD)
        return (p * lax.rsqrt(ms + QK_NORM_EPS) * gain).astype(BF16)

    dq_ref[0] = qk_norm(proj(c, c + DIFF_W), qg_ref[...] * (HEAD ** -0.5 * LOG2E))
    c += DIFF_W
    dk_ref[0] = qk_norm(proj(c, c + DIFF_W), kg_ref[...])
    c += DIFF_W
    dv_ref[0] = proj(c, c + DIFF_W).astype(BF16)
    c += DIFF_W
    g = proj(c, c + DIFF_W)
    gb_ref[0] = (g * _sigmoid(g)).astype(BF16)


def _in_proj(x, norm_g, w_in, shift_mu, q_norm_g, k_norm_g, *, tm):
    B, T, D = x.shape
    n_rep = DIFF_W // HEAD
    qg = jnp.tile(q_norm_g.reshape(1, HEAD), (1, n_rep))
    kg = jnp.tile(k_norm_g.reshape(1, HEAD), (1, n_rep))
    row = lambda w: pl.BlockSpec((1, tm, w), lambda b, t: (b, t, 0))
    full = lambda a: pl.BlockSpec(a.shape, lambda b, t: (0,) * a.ndim)
    f32o = lambda w: jax.ShapeDtypeStruct((B, T, w), F32)
    bf16o = lambda w: jax.ShapeDtypeStruct((B, T, w), BF16)
    g2 = norm_g.reshape(1, D)
    mu2 = shift_mu.reshape(1, SHIFT_COLS)
    group = jnp.arange(MXU_WIDTH) // HEAD
    ones_bd = (group[:, None] == group[None, :]).astype(BF16)
    return pl.pallas_call(
        _in_proj_kernel,
        grid=(B, T // tm),
        in_specs=[row(D), full(g2), full(w_in), full(mu2), full(qg), full(kg), full(ones_bd)],
        out_specs=[row(RWKV_W), row(RWKV_W), row(RWKV_W), row(2 * LORA), row(RWKV_W),
                   row(DIFF_W), row(DIFF_W), row(DIFF_W), row(DIFF_W)],
        out_shape=[f32o(RWKV_W), f32o(RWKV_W), f32o(RWKV_W), f32o(2 * LORA), bf16o(RWKV_W),
                   bf16o(DIFF_W), bf16o(DIFF_W), bf16o(DIFF_W), bf16o(DIFF_W)],
        scratch_shapes=[pltpu.VMEM((tm, D), BF16), pltpu.VMEM((1, SHIFT_COLS), F32)],
        compiler_params=pltpu.CompilerParams(
            dimension_semantics=("arbitrary", "arbitrary"), vmem_limit_bytes=VMEM_LIMIT),
        name="in_proj",
    )(x, g2, w_in, mu2, qg, kg, ones_bd)


def _rwkv_kernel(r_ref, k_ref, v_ref, zwa_ref, ga_ref, wd_ref, wa_ref, vec_ref,
                 o_ref, s_ref, y_ref):
    tb, width = r_ref.shape[1], r_ref.shape[2]
    n_chunks = tb // CHUNK
    n_pairs = width // LANES
    P2 = 2 * CHUNK

    @pl.when(pl.program_id(1) == 0)
    def _():
        s_ref[...] = jnp.zeros_like(s_ref)

    w0 = vec_ref[0:1, :]
    a0 = vec_ref[1:2, :]
    k_k = vec_ref[2:3, :]
    k_a = vec_ref[3:4, :]
    r_k = vec_ref[4:5, :]
    ln_w = vec_ref[5:6, :]
    ln_b = vec_ref[6:7, :]

    ri = lax.broadcasted_iota(jnp.int32, (P2, LANES), 0)
    ci = lax.broadcasted_iota(jnp.int32, (P2, LANES), 1)
    same_head = (ri // HEAD) == (ci // HEAD)
    strict = same_head & ((ci % HEAD) < (ri % HEAD))
    incl = same_head & ((ci % HEAD) <= (ri % HEAD))
    eye = ri == ci
    ones_bd = jnp.where(same_head, 1.0, 0.0).astype(BF16)
    lane_blocks = [slice(p * LANES, (p + 1) * LANES) for p in range(n_pairs)]

    def group_sum(z):
        zb = z.astype(BF16)
        return jnp.concatenate(
            [jnp.dot(zb[:, lb], ones_bd, preferred_element_type=F32) for lb in lane_blocks],
            axis=1)

    def stack(z):
        return jnp.where(same_head, jnp.concatenate([z, z], axis=0), 0.0)

    r = r_ref[0]
    k = k_ref[0]
    v = v_ref[0]
    zwa = zwa_ref[0]
    u = w0 + _dot(jnp.tanh(zwa), wd_ref[...])
    lw = -math.exp(-0.5) * _sigmoid(u)
    a = _sigmoid(a0 + _dot(zwa, wa_ref[...]))
    kkr = k * k_k
    kk = kkr / jnp.maximum(jnp.sqrt(group_sum(kkr * kkr)), 1e-12)
    kmod = k * (1.0 + (a - 1.0) * k_a)

    row_in_chunk = lax.broadcasted_iota(jnp.int32, (tb, 1), 0) % CHUNK
    cs = lw
    step = 1
    while step < CHUNK:
        cs = cs + jnp.where(row_in_chunk >= step, pltpu.roll(cs, step, 0), 0.0)
        step *= 2
    e_pos = jnp.exp(cs)
    e_neg = jnp.exp(-cs)
    al_full = -kk * jnp.exp(cs - lw)
    rt_full = r * e_pos
    bt_full = kk * a * e_neg
    kt_full = kmod * e_neg

    items = [(c, p) for c in range(n_chunks) for p in range(n_pairs)]
    blk = lambda z, c, p: z[c * CHUNK:(c + 1) * CHUNK, lane_blocks[p]]
    e_tot = {it: blk(e_pos, *it)[CHUNK - 1:CHUNK, :] for it in items}
    al = {it: stack(blk(al_full, *it)) for it in items}
    rt = {it: stack(blk(rt_full, *it)) for it in items}
    bt = {it: blk(bt_full, *it) for it in items}
    kt = {it: blk(kt_full, *it) for it in items}
    vs = {it: stack(blk(v, *it)) for it in items}

    gram = {it: _dot_nt(jnp.concatenate([al[it], rt[it]], axis=0),
                        jnp.concatenate([bt[it], bt[it], kt[it], kt[it]], axis=0))
            for it in items}
    a_ab = {it: jnp.where(strict, gram[it][:P2, :P2], 0.0) for it in items}
    a_kr = {it: jnp.concatenate([jnp.where(strict, gram[it][:P2, P2:], 0.0),
                                 jnp.where(incl, gram[it][P2:, P2:], 0.0)], axis=0)
            for it in items}
    a_rb = {it: jnp.where(incl, gram[it][P2:, :P2], 0.0) for it in items}

    n_levels = int(math.log2(CHUNK))
    tm = {it: jnp.where(eye, 1.0, a_ab[it]) for it in items}
    ap = {it: _dot(a_ab[it], a_ab[it]) for it in items}
    for _ in range(1, n_levels - 1):
        pp = {it: _dot(ap[it], jnp.concatenate([ap[it], tm[it]], axis=1)) for it in items}
        ap = {it: pp[it][:, :P2] for it in items}
        tm = {it: tm[it] + pp[it][:, P2:] for it in items}
    tm = {it: tm[it] + _dot(ap[it], tm[it]) for it in items}

    av = {it: _dot(a_kr[it], vs[it]) for it in items}
    wu = {it: _dot(tm[it], jnp.concatenate([al[it], av[it][:P2]], axis=1))
          for it in items}
    bh_t = {it: stack(bt[it] * e_tot[it]).T for it in items}
    kh_t = {it: stack(kt[it] * e_tot[it]).T for it in items}
    qb = {it: _dot(jnp.concatenate([bh_t[it], a_rb[it]], axis=0), wu[it])
          for it in items}
    khv = {it: _dot(kh_t[it], vs[it]) for it in items}
    gq = {it: jnp.concatenate([jnp.where(eye, e_tot[it], 0.0) + qb[it][:P2, :P2],
                               rt[it] + qb[it][P2:, :P2]], axis=0) for it in items}
    hm = {it: qb[it][:P2, P2:] + khv[it] for it in items}
    y1 = {it: qb[it][P2:, P2:] + av[it][P2:] for it in items}

    for p in range(n_pairs):
        s = s_ref[p]
        for c in range(n_chunks):
            it = (c, p)
            ys = _dot(gq[it], s)
            yst = ys[P2:] + y1[it]
            y_ref[c * CHUNK:(c + 1) * CHUNK, lane_blocks[p]] = yst[:CHUNK] + yst[CHUNK:]
            s = ys[:P2] + hm[it]
        s_ref[p] = s

    y = y_ref[...]
    mu = group_sum(y) * (1.0 / HEAD)
    d = y - mu
    var = group_sum(d * d) * (1.0 / HEAD)
    yn = d * lax.rsqrt(var + RWKV_GN_EPS) * ln_w + ln_b
    bonus = group_sum(r * kmod * r_k) * v
    o_ref[0] = ((yn + bonus) * ga_ref[0].astype(F32)).astype(BF16)


def _rwkv(r, k, v, zwa, ga, w0, w_decay_up, a0, w_iclr_up, k_k, k_a, r_k, ln_w, ln_b, *, tb):
    B, T, W = r.shape
    n_pairs = W // LANES
    zeros = jnp.zeros((LORA, W), F32)
    wd = jnp.concatenate([w_decay_up, zeros], axis=0)
    wa = jnp.concatenate([zeros, w_iclr_up], axis=0)
    vec = jnp.stack([w0, a0, k_k, k_a, r_k.reshape(W), ln_w, ln_b, jnp.zeros((W,), F32)])
    blk = pl.BlockSpec((1, tb, W), lambda b, t: (b, t, 0))
    full = lambda a: pl.BlockSpec(a.shape, lambda b, t: (0,) * a.ndim)
    return pl.pallas_call(
        _rwkv_kernel,
        grid=(B, T // tb),
        in_specs=[blk, blk, blk,
                  pl.BlockSpec((1, tb, 2 * LORA), lambda b, t: (b, t, 0)),
                  blk, full(wd), full(wa), full(vec)],
        out_specs=blk,
        out_shape=jax.ShapeDtypeStruct((B, T, W), BF16),
        scratch_shapes=[pltpu.VMEM((n_pairs, LANES, LANES), F32), pltpu.VMEM((tb, W), F32)],
        compiler_params=pltpu.CompilerParams(
            dimension_semantics=("arbitrary", "arbitrary"), vmem_limit_bytes=VMEM_LIMIT),
        name="rwkv7",
    )(r, k, v, zwa, ga, wd, wa, vec)


def _diff_attn_kernel(q_ref, k_ref, v_ref, gb_ref, kpos_ref, slope_ref, subg_ref, lam_ref,
                      o_ref, vt_ref, *, tq):
    seq = q_ref.shape[1]
    n_tiles = seq // tq
    nq = 2 * tq
    for r0 in range(0, seq, tq):
        vt_ref[:, r0:r0 + tq] = v_ref[0, r0:r0 + tq, :].astype(F32).T.astype(BF16)

    lane = lax.broadcasted_iota(jnp.int32, (tq, LANES), 1)
    slope = slope_ref[0][:, 0:1] * LOG2E
    slope_hi = slope.astype(BF16).astype(F32)
    slope_lo = slope - slope_hi
    lane1 = lax.broadcasted_iota(jnp.int32, (1, LANES), 1)
    q_bias = jnp.where(lane1 < 2, slope_hi, jnp.where(lane1 < 4, slope_lo, 0.0))
    q_bias = jnp.broadcast_to(q_bias, (nq, LANES)).astype(BF16)
    tri = (lax.broadcasted_iota(jnp.int32, (tq, nq), 0)
           <= lax.broadcasted_iota(jnp.int32, (tq, nq), 1) % tq)

    lam_p = lam_ref[...]
    s1 = jnp.sum(lam_p[0:1] * lam_p[1:2], axis=-1, keepdims=True)
    s2 = jnp.sum(lam_p[2:3] * lam_p[3:4], axis=-1, keepdims=True)
    lam = jnp.exp(s1) - jnp.exp(s2) + LAMBDA_INIT

    def all_sublanes(z, op):
        shift = SUBLANES // 2
        while shift:
            z = op(z, pltpu.roll(z, shift, 0))
            shift //= 2
        return z

    for i in range(n_tiles):
        q = q_ref[0, i * tq:(i + 1) * tq, :]
        zero = jnp.zeros_like(q)
        qcat = jnp.concatenate([jnp.where(lane < HEAD, q, zero),
                                jnp.where(lane >= HEAD, q, zero)], axis=0)
        qcat = jnp.concatenate([qcat, q_bias], axis=1)
        klen = (i + 1) * tq
        keys = jnp.concatenate([k_ref[0, :klen, :], kpos_ref[:klen, :]], axis=1)
        st = lax.dot_general(keys, qcat, (((1,), (1,)), ((), ())),
                             preferred_element_type=F32)
        diag = jnp.where(tri, st[i * tq:], NEG_INF)
        st = diag if i == 0 else jnp.concatenate([st[:i * tq], diag], axis=0)
        st = st.reshape(klen // SUBLANES, SUBLANES, nq)
        mx = all_sublanes(jnp.max(st, axis=0), jnp.maximum)
        p = jnp.exp2(st - mx)
        ls = all_sublanes(jnp.sum(p, axis=0), jnp.add)
        acc = jnp.dot(vt_ref[:, :klen], p.reshape(klen, nq).astype(BF16),
                      preferred_element_type=F32).reshape(LANES // SUBLANES, SUBLANES, nq)

        att = acc / ls
        ot = att[:, :, :tq] - lam * att[:, :, tq:]
        ms = all_sublanes(jnp.sum(ot * ot, axis=0), jnp.add) * (1.0 / LANES)
        ot = (ot * lax.rsqrt(ms + SUBLN_EPS)).reshape(LANES, tq)
        o = ot.T * (subg_ref[...] * (1.0 - LAMBDA_INIT))
        o_ref[0, i * tq:(i + 1) * tq, :] = (
            o * gb_ref[0, i * tq:(i + 1) * tq, :].astype(F32)).astype(BF16)


def _diff_attn(dq, dk, dv, gb, subln_g, lam_params, *, tq):
    B, T, W = dq.shape
    n_heads = W // LANES
    slopes = jnp.asarray([2.0 ** (-8.0 * (h + 1) / n_heads) for h in range(n_heads)], F32)
    slopes = jnp.broadcast_to(slopes[:, None, None], (n_heads, 1, LANES))
    pos = jnp.arange(T, dtype=jnp.int32)
    pos_lo = pos % KEY_POS_SPLIT
    pos_hi = pos - pos_lo
    kpos = jnp.zeros((T, LANES), F32)
    kpos = kpos.at[:, 0].set(pos_hi).at[:, 1].set(pos_lo).at[:, 2].set(pos_hi).at[:, 3].set(pos_lo)
    kpos = kpos.astype(BF16)
    blk = pl.BlockSpec((1, T, LANES), lambda b, h: (b, 0, h))
    subg = subln_g.reshape(1, LANES)
    return pl.pallas_call(
        functools.partial(_diff_attn_kernel, tq=tq),
        grid=(B, n_heads),
        in_specs=[blk, blk, blk, blk,
                  pl.BlockSpec((T, LANES), lambda b, h: (0, 0)),
                  pl.BlockSpec((1, 1, LANES), lambda b, h: (h, 0, 0)),
                  pl.BlockSpec((1, LANES), lambda b, h: (0, 0)),
                  pl.BlockSpec(lam_params.shape, lambda b, h: (0, 0))],
        out_specs=blk,
        out_shape=jax.ShapeDtypeStruct((B, T, W), BF16),
        scratch_shapes=[pltpu.VMEM((LANES, T), BF16)],
        compiler_params=pltpu.CompilerParams(
            dimension_semantics=("arbitrary", "arbitrary"), vmem_limit_bytes=VMEM_LIMIT),
        name="diffattn",
    )(dq, dk, dv, gb, kpos, slopes, subg, lam_params)


def _out_proj_kernel(ya_ref, yb_ref, x_ref, w_ref, o_ref):
    wa = ya_ref.shape[2]
    acc = jnp.dot(ya_ref[0], w_ref[:wa, :], preferred_element_type=F32)
    acc = acc + jnp.dot(yb_ref[0], w_ref[wa:, :], preferred_element_type=F32)
    o_ref[0] = x_ref[0] + acc


def _out_proj(ya, yb, x, w_out, *, tm):
    B, T, D = x.shape
    row = lambda w: pl.BlockSpec((1, tm, w), lambda b, t: (b, t, 0))
    return pl.pallas_call(
        _out_proj_kernel,
        grid=(B, T // tm),
        in_specs=[row(ya.shape[2]), row(yb.shape[2]), row(D),
                  pl.BlockSpec(w_out.shape, lambda b, t: (0, 0))],
        out_specs=row(D),
        out_shape=jax.ShapeDtypeStruct((B, T, D), F32),
        compiler_params=pltpu.CompilerParams(
            dimension_semantics=("arbitrary", "arbitrary"), vmem_limit_bytes=VMEM_LIMIT),
        name="out_proj",
    )(ya, yb, x, w_out)


def kernel(x, norm_g, w_in, shift_mu, w0, w_decay_up, a0, w_iclr_up, k_k, k_a, r_k, ln_x_w, ln_x_b, q_norm_g, k_norm_g, lambda_q1, lambda_k1, lambda_q2, lambda_k2, subln_g, w_out):
    depth = norm_g.shape[0]
    assert depth == 1, "lambda_init is specialised to a single layer"
    h = x
    for l in range(depth):
        r, k, v, zwa, ga, dq, dk, dv, gb = _in_proj(
            h, norm_g[l], w_in[l].astype(BF16), shift_mu[l], q_norm_g[l], k_norm_g[l], tm=512)
        ya = _rwkv(r, k, v, zwa, ga, w0[l], w_decay_up[l], a0[l], w_iclr_up[l], k_k[l], k_a[l],
                   r_k[l], ln_x_w[l], ln_x_b[l], tb=128)
        lam_params = jnp.stack([lambda_q1[l], lambda_k1[l], lambda_q2[l], lambda_k2[l]])
        yb = _diff_attn(dq, dk, dv, gb, subln_g[l], lam_params, tq=512)
        h = _out_proj(ya, yb, h, w_out[l].astype(BF16), tm=512)
    return h
```

```python
import functools
import math

import jax
import jax.numpy as jnp
from jax import lax
from jax.experimental import pallas as pl
from jax.experimental.pallas import tpu as pltpu

F32 = jnp.float32
BF16 = jnp.bfloat16

LANES = 128
SUBLANES = 8
MXU_WIDTH = 256
HEAD = 64
CHUNK = 64
RWKV_W = 512
DIFF_W = 512
LORA = 64
SHIFT_COLS = 3 * RWKV_W + 2 * LORA
NORM_EPS = 1e-6
QK_NORM_EPS = 1e-6
SUBLN_EPS = 1e-5
RWKV_GN_EPS = 64e-5
LAMBDA_INIT = 0.8 - 0.6 * math.exp(-0.3 * 0)
NEG_INF = -1e30
LOG2E = math.log2(math.e)
KEY_POS_SPLIT = 16
VMEM_LIMIT = 56 * 1024 * 1024


def _dot(a, b):
    return jnp.dot(a.astype(BF16), b.astype(BF16), preferred_element_type=F32)


def _dot_nt(a, b):
    return lax.dot_general(a.astype(BF16), b.astype(BF16), (((1,), (1,)), ((), ())),
                           preferred_element_type=F32)


def _sigmoid(x):
    return 0.5 * jnp.tanh(0.5 * x) + 0.5


def _in_proj_kernel(x_ref, g_ref, w_ref, mu_ref, qg_ref, kg_ref, ones_ref,
                    r_ref, k_ref, v_ref, zwa_ref, ga_ref, dq_ref, dk_ref, dv_ref, gb_ref,
                    hn_ref, carry_ref):
    t = pl.program_id(1)
    x = x_ref[0]
    tm = x.shape[0]
    ms = jnp.mean(x * x, axis=-1, keepdims=True)
    hn_ref[...] = (x * lax.rsqrt(ms + NORM_EPS) * g_ref[...]).astype(BF16)

    @pl.when(t == 0)
    def _():
        carry_ref[...] = jnp.zeros_like(carry_ref)

    row0 = lax.broadcasted_iota(jnp.int32, (tm, 1), 0) == 0

    def proj(c0, c1):
        return jnp.dot(hn_ref[...], w_ref[:, c0:c1], preferred_element_type=F32)

    def shifted(c0, c1):
        p = proj(c0, c1)
        prev = jnp.where(row0, carry_ref[:, c0:c1], pltpu.roll(p, 1, 0))
        carry_ref[:, c0:c1] = p[tm - 1:tm, :]
        return p + (prev - p) * mu_ref[:, c0:c1]

    r_ref[0] = shifted(0, RWKV_W)
    k_ref[0] = shifted(RWKV_W, 2 * RWKV_W)
    v_ref[0] = shifted(2 * RWKV_W, 3 * RWKV_W)
    zwa_ref[0] = shifted(3 * RWKV_W, SHIFT_COLS)

    c = SHIFT_COLS
    g = proj(c, c + RWKV_W)
    ga_ref[0] = (g * _sigmoid(g)).astype(BF16)
    c += RWKV_W

    def qk_norm(p, gain):
        sq = (p * p).astype(BF16)
        w = ones_ref.shape[0]
        ms = jnp.concatenate(
            [jnp.dot(sq[:, c0:c0 + w], ones_ref[...], preferred_element_type=F32)
             for c0 in range(0, DIFF_W, w)], axis=1) * (1.0 / HEAD)
        return (p * lax.rsqrt(ms + QK_NORM_EPS) * gain).astype(BF16)

    dq_ref[0] = qk_norm(proj(c, c + DIFF_W), qg_ref[...] * (HEAD ** -0.5 * LOG2E))
    c += DIFF_W
    dk_ref[0] = qk_norm(proj(c, c + DIFF_W), kg_ref[...])
    c += DIFF_W
    dv_ref[0] = proj(c, c + DIFF_W).astype(BF16)
    c += DIFF_W
    g = proj(c, c + DIFF_W)
    gb_ref[0] = (g * _sigmoid(g)).astype(BF16)


def _in_proj(x, norm_g, w_in, shift_mu, q_norm_g, k_norm_g, *, tm):
    B, T, D = x.shape
    n_rep = DIFF_W // HEAD
    qg = jnp.tile(q_norm_g.reshape(1, HEAD), (1, n_rep))
    kg = jnp.tile(k_norm_g.reshape(1, HEAD), (1, n_rep))
    row = lambda w: pl.BlockSpec((1, tm, w), lambda b, t: (b, t, 0))
    full = lambda a: pl.BlockSpec(a.shape, lambda b, t: (0,) * a.ndim)
    f32o = lambda w: jax.ShapeDtypeStruct((B, T, w), F32)
    bf16o = lambda w: jax.ShapeDtypeStruct((B, T, w), BF16)
    g2 = norm_g.reshape(1, D)
    mu2 = shift_mu.reshape(1, SHIFT_COLS)
    group = jnp.arange(MXU_WIDTH) // HEAD
    ones_bd = (group[:, None] == group[None, :]).astype(BF16)
    return pl.pallas_call(
        _in_proj_kernel,
        grid=(B, T // tm),
        in_specs=[row(D), full(g2), full(w_in), full(mu2), full(qg), full(kg), full(ones_bd)],
        out_specs=[row(RWKV_W), row(RWKV_W), row(RWKV_W), row(2 * LORA), row(RWKV_W),
                   row(DIFF_W), row(DIFF_W), row(DIFF_W), row(DIFF_W)],
        out_shape=[f32o(RWKV_W), f32o(RWKV_W), f32o(RWKV_W), f32o(2 * LORA), bf16o(RWKV_W),
                   bf16o(DIFF_W), bf16o(DIFF_W), bf16o(DIFF_W), bf16o(DIFF_W)],
        scratch_shapes=[pltpu.VMEM((tm, D), BF16), pltpu.VMEM((1, SHIFT_COLS), F32)],
        compiler_params=pltpu.CompilerParams(
            dimension_semantics=("arbitrary", "arbitrary"), vmem_limit_bytes=VMEM_LIMIT),
        name="in_proj",
    )(x, g2, w_in, mu2, qg, kg, ones_bd)


def _rwkv_kernel(r_ref, k_ref, v_ref, zwa_ref, ga_ref, wd_ref, wa_ref, vec_ref,
                 o_ref, s_ref, y_ref):
    bb, tb, width = r_ref.shape
    rows_total = bb * tb
    chunks_per_seq = tb // CHUNK
    n_chunks = bb * chunks_per_seq
    n_pairs = width // LANES
    P2 = 2 * CHUNK

    @pl.when(pl.program_id(1) == 0)
    def _():
        s_ref[...] = jnp.zeros_like(s_ref)

    w0 = vec_ref[0:1, :]
    a0 = vec_ref[1:2, :]
    k_k = vec_ref[2:3, :]
    k_a = vec_ref[3:4, :]
    r_k = vec_ref[4:5, :]
    ln_w = vec_ref[5:6, :]
    ln_b = vec_ref[6:7, :]

    ri = lax.broadcasted_iota(jnp.int32, (P2, LANES), 0)
    ci = lax.broadcasted_iota(jnp.int32, (P2, LANES), 1)
    same_head = (ri // HEAD) == (ci // HEAD)
    ones_bd = jnp.where(same_head, 1.0, 0.0).astype(BF16)
    ti = lax.broadcasted_iota(jnp.int32, (CHUNK, LANES), 0)
    si = lax.broadcasted_iota(jnp.int32, (CHUNK, LANES), 1) % HEAD
    strict = si < ti
    incl = si <= ti
    eye = si == ti
    lane_blocks = [slice(p * LANES, (p + 1) * LANES) for p in range(n_pairs)]

    def group_sum(z):
        zb = z.astype(BF16)
        return jnp.concatenate(
            [jnp.dot(zb[:, lb], ones_bd, preferred_element_type=F32) for lb in lane_blocks],
            axis=1)

    def bd(z):
        return jnp.where(same_head, jnp.concatenate([z, z], axis=0), 0.0)

    def head_transposed(z):
        zt = bd(z).T
        return zt[:CHUNK] + zt[CHUNK:]

    flat = lambda ref: ref[...].reshape(rows_total, ref.shape[2])
    r = flat(r_ref)
    k = flat(k_ref)
    v = flat(v_ref)
    zwa = flat(zwa_ref)
    u = w0 + _dot(jnp.tanh(zwa), wd_ref[...])
    lw = -math.exp(-0.5) * _sigmoid(u)
    a = _sigmoid(a0 + _dot(zwa, wa_ref[...]))
    kkr = k * k_k
    kk = kkr * lax.rsqrt(jnp.maximum(group_sum(kkr * kkr), 1e-24))
    kmod = k * (1.0 + (a - 1.0) * k_a)

    row_in_chunk = lax.broadcasted_iota(jnp.int32, (rows_total, 1), 0) % CHUNK
    cs = lw
    step = 1
    while step < CHUNK:
        cs = cs + jnp.where(row_in_chunk >= step, pltpu.roll(cs, step, 0), 0.0)
        step *= 2
    e_pos = jnp.exp(cs)
    e_neg = jnp.exp(-cs)
    al_full = -kk * jnp.exp(cs - lw)
    rt_full = r * e_pos
    bt_full = kk * a * e_neg
    kt_full = kmod * e_neg

    items = [(c, p) for c in range(n_chunks) for p in range(n_pairs)]
    blk = lambda z, c, p: z[c * CHUNK:(c + 1) * CHUNK, lane_blocks[p]]
    e_tot = {it: blk(e_pos, *it)[CHUNK - 1:CHUNK, :] for it in items}
    al = {it: blk(al_full, *it) for it in items}
    rt = {it: blk(rt_full, *it) for it in items}
    bt = {it: blk(bt_full, *it) for it in items}
    kt = {it: blk(kt_full, *it) for it in items}
    v_bd = {it: bd(blk(v, *it)) for it in items}

    gram = {it: _dot_nt(jnp.concatenate([al[it], rt[it]], axis=0),
                        jnp.concatenate([bd(bt[it]), bd(kt[it])], axis=0))
            for it in items}
    a_ab = {it: jnp.where(strict, gram[it][:CHUNK, :LANES], 0.0) for it in items}
    a_kr = {it: jnp.concatenate([jnp.where(strict, gram[it][:CHUNK, LANES:], 0.0),
                                 jnp.where(incl, gram[it][CHUNK:, LANES:], 0.0)], axis=0)
            for it in items}
    a_rb = {it: jnp.where(incl, gram[it][CHUNK:, :LANES], 0.0) for it in items}

    n_levels = int(math.log2(CHUNK))
    tm = {it: jnp.where(eye, 1.0, a_ab[it]) for it in items}
    ap = {it: _dot(a_ab[it], bd(a_ab[it])) for it in items}
    for _ in range(1, n_levels - 1):
        pp = {it: _dot(ap[it], jnp.concatenate([bd(ap[it]), bd(tm[it])], axis=1))
              for it in items}
        ap = {it: pp[it][:, :LANES] for it in items}
        tm = {it: tm[it] + pp[it][:, LANES:] for it in items}
    tm = {it: tm[it] + _dot(ap[it], bd(tm[it])) for it in items}

    av = {it: _dot(a_kr[it], v_bd[it]) for it in items}
    wu = {it: _dot(tm[it], jnp.concatenate([bd(al[it]), bd(av[it][:CHUNK])], axis=1))
          for it in items}
    wu_bd = {it: jnp.concatenate([bd(wu[it][:, :LANES]), bd(wu[it][:, LANES:])], axis=1)
             for it in items}
    ab = {it: _dot(a_rb[it], wu_bd[it]) for it in items}
    bk_t = {it: jnp.concatenate([head_transposed(bt[it] * e_tot[it]),
                                 head_transposed(kt[it] * e_tot[it])], axis=1)
            for it in items}
    gh = {it: _dot(bk_t[it], jnp.concatenate(
        [wu_bd[it], jnp.concatenate([jnp.zeros_like(v_bd[it]), v_bd[it]], axis=1)], axis=0))
          for it in items}
    gq = {it: jnp.concatenate([jnp.where(eye, e_tot[it], 0.0) + gh[it][:, :LANES],
                               rt[it] + ab[it][:, :LANES]], axis=0) for it in items}
    hm = {it: gh[it][:, LANES:] for it in items}
    y1 = {it: ab[it][:, LANES:] + av[it][CHUNK:] for it in items}

    for b in range(bb):
        for p in range(n_pairs):
            s = s_ref[b, p]
            for c in range(b * chunks_per_seq, (b + 1) * chunks_per_seq):
                it = (c, p)
                ys = _dot(gq[it], s)
                y_ref[c * CHUNK:(c + 1) * CHUNK, lane_blocks[p]] = ys[CHUNK:] + y1[it]
                s = bd(ys[:CHUNK] + hm[it])
            s_ref[b, p] = s

    y = y_ref[...]
    mu = group_sum(y) * (1.0 / HEAD)
    d = y - mu
    var = group_sum(d * d) * (1.0 / HEAD)
    yn = d * lax.rsqrt(var + RWKV_GN_EPS) * ln_w + ln_b
    bonus = group_sum(r * kmod * r_k) * v
    out = (yn + bonus) * flat(ga_ref).astype(F32)
    o_ref[...] = out.reshape(bb, tb, width).astype(BF16)


def _rwkv(r, k, v, zwa, ga, w0, w_decay_up, a0, w_iclr_up, k_k, k_a, r_k, ln_w, ln_b, *, bb, tb):
    B, T, W = r.shape
    n_pairs = W // LANES
    zeros = jnp.zeros((LORA, W), F32)
    wd = jnp.concatenate([w_decay_up, zeros], axis=0)
    wa = jnp.concatenate([zeros, w_iclr_up], axis=0)
    vec = jnp.stack([w0, a0, k_k, k_a, r_k.reshape(W), ln_w, ln_b, jnp.zeros((W,), F32)])
    blk = pl.BlockSpec((bb, tb, W), lambda b, t: (b, t, 0))
    full = lambda a: pl.BlockSpec(a.shape, lambda b, t: (0,) * a.ndim)
    return pl.pallas_call(
        _rwkv_kernel,
        grid=(B // bb, T // tb),
        in_specs=[blk, blk, blk,
                  pl.BlockSpec((bb, tb, 2 * LORA), lambda b, t: (b, t, 0)),
                  blk, full(wd), full(wa), full(vec)],
        out_specs=blk,
        out_shape=jax.ShapeDtypeStruct((B, T, W), BF16),
        scratch_shapes=[pltpu.VMEM((bb, n_pairs, LANES, LANES), F32),
                        pltpu.VMEM((bb * tb, W), F32)],
        compiler_params=pltpu.CompilerParams(
            dimension_semantics=("arbitrary", "arbitrary"), vmem_limit_bytes=VMEM_LIMIT),
        name="rwkv7",
    )(r, k, v, zwa, ga, wd, wa, vec)


def _diff_attn_kernel(q_ref, k_ref, v_ref, gb_ref, kpos_ref, slope_ref, subg_ref, lam_ref,
                      o_ref, vt_ref, *, tq):
    seq = q_ref.shape[1]
    n_tiles = seq // tq
    nq = 2 * tq
    for r0 in range(0, seq, tq):
        vt_ref[:, r0:r0 + tq] = v_ref[0, r0:r0 + tq, :].astype(F32).T.astype(BF16)

    lane = lax.broadcasted_iota(jnp.int32, (tq, LANES), 1)
    slope = slope_ref[0][:, 0:1] * LOG2E
    slope_hi = slope.astype(BF16).astype(F32)
    slope_lo = slope - slope_hi
    lane1 = lax.broadcasted_iota(jnp.int32, (1, LANES), 1)
    q_bias = jnp.where(lane1 < 2, slope_hi, jnp.where(lane1 < 4, slope_lo, 0.0))
    q_bias = jnp.broadcast_to(q_bias, (nq, LANES)).astype(BF16)
    tri = (lax.broadcasted_iota(jnp.int32, (tq, nq), 0)
           <= lax.broadcasted_iota(jnp.int32, (tq, nq), 1) % tq)

    lam_p = lam_ref[...]
    s1 = jnp.sum(lam_p[0:1] * lam_p[1:2], axis=-1, keepdims=True)
    s2 = jnp.sum(lam_p[2:3] * lam_p[3:4], axis=-1, keepdims=True)
    lam = jnp.exp(s1) - jnp.exp(s2) + LAMBDA_INIT

    def all_sublanes(z, op):
        shift = SUBLANES // 2
        while shift:
            z = op(z, pltpu.roll(z, shift, 0))
            shift //= 2
        return z

    for i in range(n_tiles):
        q = q_ref[0, i * tq:(i + 1) * tq, :]
        zero = jnp.zeros_like(q)
        qcat = jnp.concatenate([jnp.where(lane < HEAD, q, zero),
                                jnp.where(lane >= HEAD, q, zero)], axis=0)
        qcat = jnp.concatenate([qcat, q_bias], axis=1)
        klen = (i + 1) * tq
        keys = jnp.concatenate([k_ref[0, :klen, :], kpos_ref[:klen, :]], axis=1)
        st = lax.dot_general(keys, qcat, (((1,), (1,)), ((), ())),
                             preferred_element_type=F32)
        diag = jnp.where(tri, st[i * tq:], NEG_INF)
        st = diag if i == 0 else jnp.concatenate([st[:i * tq], diag], axis=0)
        st = st.reshape(klen // SUBLANES, SUBLANES, nq)
        mx = all_sublanes(jnp.max(st, axis=0), jnp.maximum)
        p = jnp.exp2(st - mx)
        ls = all_sublanes(jnp.sum(p, axis=0), jnp.add)
        acc = jnp.dot(vt_ref[:, :klen], p.reshape(klen, nq).astype(BF16),
                      preferred_element_type=F32).reshape(LANES // SUBLANES, SUBLANES, nq)

        att = acc / ls
        ot = att[:, :, :tq] - lam * att[:, :, tq:]
        ms = all_sublanes(jnp.sum(ot * ot, axis=0), jnp.add) * (1.0 / LANES)
        ot = (ot * lax.rsqrt(ms + SUBLN_EPS)).reshape(LANES, tq)
        o = ot.T * (subg_ref[...] * (1.0 - LAMBDA_INIT))
        o_ref[0, i * tq:(i + 1) * tq, :] = (
            o * gb_ref[0, i * tq:(i + 1) * tq, :].astype(F32)).astype(BF16)


def _diff_attn(dq, dk, dv, gb, subln_g, lam_params, *, tq):
    B, T, W = dq.shape
    n_heads = W // LANES
    slopes = jnp.asarray([2.0 ** (-8.0 * (h + 1) / n_heads) for h in range(n_heads)], F32)
    slopes = jnp.broadcast_to(slopes[:, None, None], (n_heads, 1, LANES))
    pos = jnp.arange(T, dtype=jnp.int32)
    pos_lo = pos % KEY_POS_SPLIT
    pos_hi = pos - pos_lo
    kpos = jnp.zeros((T, LANES), F32)
    kpos = kpos.at[:, 0].set(pos_hi).at[:, 1].set(pos_lo).at[:, 2].set(pos_hi).at[:, 3].set(pos_lo)
    kpos = kpos.astype(BF16)
    blk = pl.BlockSpec((1, T, LANES), lambda b, h: (b, 0, h))
    subg = subln_g.reshape(1, LANES)
    return pl.pallas_call(
        functools.partial(_diff_attn_kernel, tq=tq),
        grid=(B, n_heads),
        in_specs=[blk, blk, blk, blk,
                  pl.BlockSpec((T, LANES), lambda b, h: (0, 0)),
                  pl.BlockSpec((1, 1, LANES), lambda b, h: (h, 0, 0)),
                  pl.BlockSpec((1, LANES), lambda b, h: (0, 0)),
                  pl.BlockSpec(lam_params.shape, lambda b, h: (0, 0))],
        out_specs=blk,
        out_shape=jax.ShapeDtypeStruct((B, T, W), BF16),
        scratch_shapes=[pltpu.VMEM((LANES, T), BF16)],
        compiler_params=pltpu.CompilerParams(
            dimension_semantics=("arbitrary", "arbitrary"), vmem_limit_bytes=VMEM_LIMIT),
        name="diffattn",
    )(dq, dk, dv, gb, kpos, slopes, subg, lam_params)


def _out_proj_kernel(ya_ref, yb_ref, x_ref, w_ref, o_ref):
    wa = ya_ref.shape[2]
    acc = jnp.dot(ya_ref[0], w_ref[:wa, :], preferred_element_type=F32)
    acc = acc + jnp.dot(yb_ref[0], w_ref[wa:, :], preferred_element_type=F32)
    o_ref[0] = x_ref[0] + acc


def _out_proj(ya, yb, x, w_out, *, tm):
    B, T, D = x.shape
    row = lambda w: pl.BlockSpec((1, tm, w), lambda b, t: (b, t, 0))
    return pl.pallas_call(
        _out_proj_kernel,
        grid=(B, T // tm),
        in_specs=[row(ya.shape[2]), row(yb.shape[2]), row(D),
                  pl.BlockSpec(w_out.shape, lambda b, t: (0, 0))],
        out_specs=row(D),
        out_shape=jax.ShapeDtypeStruct((B, T, D), F32),
        compiler_params=pltpu.CompilerParams(
            dimension_semantics=("arbitrary", "arbitrary"), vmem_limit_bytes=VMEM_LIMIT),
        name="out_proj",
    )(ya, yb, x, w_out)


def kernel(x, norm_g, w_in, shift_mu, w0, w_decay_up, a0, w_iclr_up, k_k, k_a, r_k, ln_x_w, ln_x_b, q_norm_g, k_norm_g, lambda_q1, lambda_k1, lambda_q2, lambda_k2, subln_g, w_out):
    depth = norm_g.shape[0]
    assert depth == 1, "lambda_init is specialised to a single layer"
    h = x
    for l in range(depth):
        r, k, v, zwa, ga, dq, dk, dv, gb = _in_proj(
            h, norm_g[l], w_in[l].astype(BF16), shift_mu[l], q_norm_g[l], k_norm_g[l], tm=512)
        ya = _rwkv(r, k, v, zwa, ga, w0[l], w_decay_up[l], a0[l], w_iclr_up[l], k_k[l], k_a[l],
                   r_k[l], ln_x_w[l], ln_x_b[l], bb=4, tb=128)
        lam_params = jnp.stack([lambda_q1[l], lambda_k1[l], lambda_q2[l], lambda_k2[l]])
        yb = _diff_attn(dq, dk, dv, gb, subln_g[l], lam_params, tq=512)
        h = _out_proj(ya, yb, h, w_out[l].astype(BF16), tm=512)
    return h
```

```python
import functools
import math

import jax
import jax.numpy as jnp
from jax import lax
from jax.experimental import pallas as pl
from jax.experimental.pallas import tpu as pltpu

F32 = jnp.float32
BF16 = jnp.bfloat16

LANES = 128
SUBLANES = 8
MXU_WIDTH = 256
HEAD = 64
CHUNK = 64
RWKV_W = 512
DIFF_W = 512
LORA = 64
SHIFT_COLS = 3 * RWKV_W + 2 * LORA
NORM_EPS = 1e-6
QK_NORM_EPS = 1e-6
SUBLN_EPS = 1e-5
RWKV_GN_EPS = 64e-5
LAMBDA_INIT = 0.8 - 0.6 * math.exp(-0.3 * 0)
NEG_INF = -1e30
LOG2E = math.log2(math.e)
KEY_POS_SPLIT = 16
VMEM_LIMIT = 56 * 1024 * 1024


def _dot(a, b):
    return jnp.dot(a.astype(BF16), b.astype(BF16), preferred_element_type=F32)


def _dot_nt(a, b):
    return lax.dot_general(a.astype(BF16), b.astype(BF16), (((1,), (1,)), ((), ())),
                           preferred_element_type=F32)


def _sigmoid(x):
    return 0.5 * jnp.tanh(0.5 * x) + 0.5


def _in_proj_kernel(x_ref, g_ref, w_ref, mu_ref, qg_ref, kg_ref, ones_ref,
                    r_ref, k_ref, v_ref, zwa_ref, ga_ref, dq_ref, dk_ref, dv_ref, gb_ref,
                    hn_ref, carry_ref):
    t = pl.program_id(1)
    x = x_ref[0]
    tm = x.shape[0]
    ms = jnp.mean(x * x, axis=-1, keepdims=True)
    hn_ref[...] = (x * lax.rsqrt(ms + NORM_EPS) * g_ref[...]).astype(BF16)

    @pl.when(t == 0)
    def _():
        carry_ref[...] = jnp.zeros_like(carry_ref)

    row0 = lax.broadcasted_iota(jnp.int32, (tm, 1), 0) == 0

    def proj(c0, c1):
        return jnp.dot(hn_ref[...], w_ref[:, c0:c1], preferred_element_type=F32)

    def shifted(c0, c1):
        p = proj(c0, c1)
        prev = jnp.where(row0, carry_ref[:, c0:c1], pltpu.roll(p, 1, 0))
        carry_ref[:, c0:c1] = p[tm - 1:tm, :]
        return p + (prev - p) * mu_ref[:, c0:c1]

    r_ref[0] = shifted(0, RWKV_W)
    k_ref[0] = shifted(RWKV_W, 2 * RWKV_W)
    v_ref[0] = shifted(2 * RWKV_W, 3 * RWKV_W)
    zwa_ref[0] = shifted(3 * RWKV_W, SHIFT_COLS)

    c = SHIFT_COLS
    g = proj(c, c + RWKV_W)
    ga_ref[0] = (g * _sigmoid(g)).astype(BF16)
    c += RWKV_W

    def qk_norm(p, gain):
        sq = (p * p).astype(BF16)
        w = ones_ref.shape[0]
        ms = jnp.concatenate(
            [jnp.dot(sq[:, c0:c0 + w], ones_ref[...], preferred_element_type=F32)
             for c0 in range(0, DIFF_W, w)], axis=1) * (1.0 / HEAD)
        return (p * lax.rsqrt(ms + QK_NORM_EPS) * gain).astype(BF16)

    dq_ref[0] = qk_norm(proj(c, c + DIFF_W), qg_ref[...] * (HEAD ** -0.5 * LOG2E))
    c += DIFF_W
    dk_ref[0] = qk_norm(proj(c, c + DIFF_W), kg_ref[...])
    c += DIFF_W
    dv_ref[0] = proj(c, c + DIFF_W).astype(BF16)
    c += DIFF_W
    g = proj(c, c + DIFF_W)
    gb_ref[0] = (g * _sigmoid(g)).astype(BF16)


def _in_proj(x, norm_g, w_in, shift_mu, q_norm_g, k_norm_g, *, tm):
    B, T, D = x.shape
    n_rep = DIFF_W // HEAD
    qg = jnp.tile(q_norm_g.reshape(1, HEAD), (1, n_rep))
    kg = jnp.tile(k_norm_g.reshape(1, HEAD), (1, n_rep))
    row = lambda w: pl.BlockSpec((1, tm, w), lambda b, t: (b, t, 0))
    full = lambda a: pl.BlockSpec(a.shape, lambda b, t: (0,) * a.ndim,
                                  pipeline_mode=pl.Buffered(1))
    f32o = lambda w: jax.ShapeDtypeStruct((B, T, w), F32)
    bf16o = lambda w: jax.ShapeDtypeStruct((B, T, w), BF16)
    g2 = norm_g.reshape(1, D)
    mu2 = shift_mu.reshape(1, SHIFT_COLS)
    group = jnp.arange(MXU_WIDTH) // HEAD
    ones_bd = (group[:, None] == group[None, :]).astype(BF16)
    return pl.pallas_call(
        _in_proj_kernel,
        grid=(B, T // tm),
        in_specs=[row(D), full(g2), full(w_in), full(mu2), full(qg), full(kg), full(ones_bd)],
        out_specs=[row(RWKV_W), row(RWKV_W), row(RWKV_W), row(2 * LORA), row(RWKV_W),
                   row(DIFF_W), row(DIFF_W), row(DIFF_W), row(DIFF_W)],
        out_shape=[f32o(RWKV_W), f32o(RWKV_W), f32o(RWKV_W), f32o(2 * LORA), bf16o(RWKV_W),
                   bf16o(DIFF_W), bf16o(DIFF_W), bf16o(DIFF_W), bf16o(DIFF_W)],
        scratch_shapes=[pltpu.VMEM((tm, D), BF16), pltpu.VMEM((1, SHIFT_COLS), F32)],
        compiler_params=pltpu.CompilerParams(
            dimension_semantics=("arbitrary", "arbitrary"), vmem_limit_bytes=VMEM_LIMIT),
        name="in_proj",
    )(x, g2, w_in, mu2, qg, kg, ones_bd)


def _rwkv_kernel(r_ref, k_ref, v_ref, zwa_ref, ga_ref, wd_ref, wa_ref, vec_ref,
                 o_ref, s_ref, y_ref):
    bb, tb, width = r_ref.shape
    rows_total = bb * tb
    chunks_per_seq = tb // CHUNK
    n_chunks = bb * chunks_per_seq
    n_pairs = width // LANES
    P2 = 2 * CHUNK

    @pl.when(pl.program_id(1) == 0)
    def _():
        s_ref[...] = jnp.zeros_like(s_ref)

    w0 = vec_ref[0:1, :]
    a0 = vec_ref[1:2, :]
    k_k = vec_ref[2:3, :]
    k_a = vec_ref[3:4, :]
    r_k = vec_ref[4:5, :]
    ln_w = vec_ref[5:6, :]
    ln_b = vec_ref[6:7, :]

    ri = lax.broadcasted_iota(jnp.int32, (P2, LANES), 0)
    ci = lax.broadcasted_iota(jnp.int32, (P2, LANES), 1)
    same_head = (ri // HEAD) == (ci // HEAD)
    ones_bd = jnp.where(same_head, 1.0, 0.0).astype(BF16)
    ti = lax.broadcasted_iota(jnp.int32, (CHUNK, LANES), 0)
    si = lax.broadcasted_iota(jnp.int32, (CHUNK, LANES), 1) % HEAD
    strict = si < ti
    incl = si <= ti
    eye = si == ti
    lane_blocks = [slice(p * LANES, (p + 1) * LANES) for p in range(n_pairs)]

    def group_sum(z):
        zb = z.astype(BF16)
        return jnp.concatenate(
            [jnp.dot(zb[:, lb], ones_bd, preferred_element_type=F32) for lb in lane_blocks],
            axis=1)

    def bd(z):
        return jnp.where(same_head, jnp.concatenate([z, z], axis=0), 0.0)

    def head_transposed(z):
        zt = bd(z).T
        return zt[:CHUNK] + zt[CHUNK:]

    flat = lambda ref: ref[...].reshape(rows_total, ref.shape[2])
    r = flat(r_ref)
    k = flat(k_ref)
    v = flat(v_ref)
    zwa = flat(zwa_ref)
    u = w0 + _dot(jnp.tanh(zwa), wd_ref[...])
    lw = -math.exp(-0.5) * _sigmoid(u)
    a = _sigmoid(a0 + _dot(zwa, wa_ref[...]))
    kkr = k * k_k
    kk = kkr * lax.rsqrt(jnp.maximum(group_sum(kkr * kkr), 1e-24))
    kmod = k * (1.0 + (a - 1.0) * k_a)

    row_in_chunk = lax.broadcasted_iota(jnp.int32, (rows_total, 1), 0) % CHUNK
    cs = lw
    step = 1
    while step < CHUNK:
        cs = cs + jnp.where(row_in_chunk >= step, pltpu.roll(cs, step, 0), 0.0)
        step *= 2
    e_pos = jnp.exp(cs)
    e_neg = jnp.exp(-cs)
    al_full = -kk * jnp.exp(cs - lw)
    rt_full = r * e_pos
    bt_full = kk * a * e_neg
    kt_full = kmod * e_neg

    items = [(c, p) for c in range(n_chunks) for p in range(n_pairs)]
    blk = lambda z, c, p: z[c * CHUNK:(c + 1) * CHUNK, lane_blocks[p]]
    e_tot = {it: blk(e_pos, *it)[CHUNK - 1:CHUNK, :] for it in items}
    al = {it: blk(al_full, *it) for it in items}
    rt = {it: blk(rt_full, *it) for it in items}
    bt = {it: blk(bt_full, *it) for it in items}
    kt = {it: blk(kt_full, *it) for it in items}
    v_bd = {it: bd(blk(v, *it)) for it in items}

    gram = {it: _dot_nt(jnp.concatenate([al[it], rt[it]], axis=0),
                        jnp.concatenate([bd(bt[it]), bd(kt[it])], axis=0))
            for it in items}
    a_ab = {it: jnp.where(strict, gram[it][:CHUNK, :LANES], 0.0) for it in items}
    a_kr = {it: jnp.concatenate([jnp.where(strict, gram[it][:CHUNK, LANES:], 0.0),
                                 jnp.where(incl, gram[it][CHUNK:, LANES:], 0.0)], axis=0)
            for it in items}
    a_rb = {it: jnp.where(incl, gram[it][CHUNK:, :LANES], 0.0) for it in items}

    n_levels = int(math.log2(CHUNK))
    tm = {it: jnp.where(eye, 1.0, a_ab[it]) for it in items}
    ap = {it: _dot(a_ab[it], bd(a_ab[it])) for it in items}
    for _ in range(1, n_levels - 1):
        pp = {it: _dot(ap[it], jnp.concatenate([bd(ap[it]), bd(tm[it])], axis=1))
              for it in items}
        ap = {it: pp[it][:, :LANES] for it in items}
        tm = {it: tm[it] + pp[it][:, LANES:] for it in items}
    tm = {it: tm[it] + _dot(ap[it], bd(tm[it])) for it in items}

    av = {it: _dot(a_kr[it], v_bd[it]) for it in items}
    wu = {it: _dot(tm[it], jnp.concatenate([bd(al[it]), bd(av[it][:CHUNK])], axis=1))
          for it in items}
    wu_bd = {it: jnp.concatenate([bd(wu[it][:, :LANES]), bd(wu[it][:, LANES:])], axis=1)
             for it in items}
    ab = {it: _dot(a_rb[it], wu_bd[it]) for it in items}
    bk_t = {it: jnp.concatenate([head_transposed(bt[it] * e_tot[it]),
                                 head_transposed(kt[it] * e_tot[it])], axis=1)
            for it in items}
    gh = {it: _dot(bk_t[it], jnp.concatenate(
        [wu_bd[it], jnp.concatenate([jnp.zeros_like(v_bd[it]), v_bd[it]], axis=1)], axis=0))
          for it in items}
    gq = {it: jnp.concatenate([jnp.where(eye, e_tot[it], 0.0) + gh[it][:, :LANES],
                               rt[it] + ab[it][:, :LANES]], axis=0) for it in items}
    hm = {it: gh[it][:, LANES:] for it in items}
    y1 = {it: ab[it][:, LANES:] + av[it][CHUNK:] for it in items}

    for b in range(bb):
        for p in range(n_pairs):
            s = s_ref[b, p]
            for c in range(b * chunks_per_seq, (b + 1) * chunks_per_seq):
                it = (c, p)
                ys = _dot(gq[it], s)
                y_ref[c * CHUNK:(c + 1) * CHUNK, lane_blocks[p]] = ys[CHUNK:] + y1[it]
                s = bd(ys[:CHUNK] + hm[it])
            s_ref[b, p] = s

    y = y_ref[...]
    mu = group_sum(y) * (1.0 / HEAD)
    d = y - mu
    var = group_sum(d * d) * (1.0 / HEAD)
    yn = d * lax.rsqrt(var + RWKV_GN_EPS) * ln_w + ln_b
    bonus = group_sum(r * kmod * r_k) * v
    out = (yn + bonus) * flat(ga_ref).astype(F32)
    o_ref[...] = out.reshape(bb, tb, width).astype(BF16)


def _rwkv(r, k, v, zwa, ga, w0, w_decay_up, a0, w_iclr_up, k_k, k_a, r_k, ln_w, ln_b, *, bb, tb):
    B, T, W = r.shape
    n_pairs = W // LANES
    zeros = jnp.zeros((LORA, W), F32)
    wd = jnp.concatenate([w_decay_up, zeros], axis=0)
    wa = jnp.concatenate([zeros, w_iclr_up], axis=0)
    vec = jnp.stack([w0, a0, k_k, k_a, r_k.reshape(W), ln_w, ln_b, jnp.zeros((W,), F32)])
    blk = pl.BlockSpec((bb, tb, W), lambda b, t: (b, t, 0))
    full = lambda a: pl.BlockSpec(a.shape, lambda b, t: (0,) * a.ndim)
    return pl.pallas_call(
        _rwkv_kernel,
        grid=(B // bb, T // tb),
        in_specs=[blk, blk, blk,
                  pl.BlockSpec((bb, tb, 2 * LORA), lambda b, t: (b, t, 0)),
                  blk, full(wd), full(wa), full(vec)],
        out_specs=blk,
        out_shape=jax.ShapeDtypeStruct((B, T, W), BF16),
        scratch_shapes=[pltpu.VMEM((bb, n_pairs, LANES, LANES), F32),
                        pltpu.VMEM((bb * tb, W), F32)],
        compiler_params=pltpu.CompilerParams(
            dimension_semantics=("arbitrary", "arbitrary"), vmem_limit_bytes=VMEM_LIMIT),
        name="rwkv7",
    )(r, k, v, zwa, ga, wd, wa, vec)


def _diff_attn_kernel(q_ref, k_ref, v_ref, gb_ref, kpos_ref, slope_ref, subg_ref, lam_ref,
                      o_ref, vt_ref, *, tq):
    seq = q_ref.shape[1]
    n_heads = q_ref.shape[2] // LANES
    n_tiles = seq // tq
    nq = 2 * tq
    head_lanes = [slice(h * LANES, (h + 1) * LANES) for h in range(n_heads)]
    for h, hl in enumerate(head_lanes):
        for r0 in range(0, seq, tq):
            vt_ref[h, :, r0:r0 + tq] = v_ref[0, r0:r0 + tq, hl].astype(F32).T.astype(BF16)

    lane = lax.broadcasted_iota(jnp.int32, (tq, LANES), 1)
    lane1 = lax.broadcasted_iota(jnp.int32, (1, LANES), 1)
    tri = (lax.broadcasted_iota(jnp.int32, (tq, nq), 0)
           <= lax.broadcasted_iota(jnp.int32, (tq, nq), 1) % tq)

    def alibi_lanes(h):
        slope = slope_ref[h][:, 0:1] * LOG2E
        slope_hi = slope.astype(BF16).astype(F32)
        slope_lo = slope - slope_hi
        q_bias = jnp.where(lane1 < 2, slope_hi, jnp.where(lane1 < 4, slope_lo, 0.0))
        return jnp.broadcast_to(q_bias, (nq, LANES)).astype(BF16)

    q_bias = [alibi_lanes(h) for h in range(n_heads)]

    lam_p = lam_ref[...]
    s1 = jnp.sum(lam_p[0:1] * lam_p[1:2], axis=-1, keepdims=True)
    s2 = jnp.sum(lam_p[2:3] * lam_p[3:4], axis=-1, keepdims=True)
    lam = jnp.exp(s1) - jnp.exp(s2) + LAMBDA_INIT

    def all_sublanes(z, op):
        shift = SUBLANES // 2
        while shift:
            z = op(z, pltpu.roll(z, shift, 0))
            shift //= 2
        return z

    for i in range(n_tiles):
        rows = slice(i * tq, (i + 1) * tq)
        klen = (i + 1) * tq
        for h, hl in enumerate(head_lanes):
            q = q_ref[0, rows, hl]
            zero = jnp.zeros_like(q)
            qcat = jnp.concatenate([jnp.where(lane < HEAD, q, zero),
                                    jnp.where(lane >= HEAD, q, zero)], axis=0)
            qcat = jnp.concatenate([qcat, q_bias[h]], axis=1)
            keys = jnp.concatenate([k_ref[0, :klen, hl], kpos_ref[:klen, :]], axis=1)
            st = lax.dot_general(keys, qcat, (((1,), (1,)), ((), ())),
                                 preferred_element_type=F32)
            diag = jnp.where(tri, st[i * tq:], NEG_INF)
            st = diag if i == 0 else jnp.concatenate([st[:i * tq], diag], axis=0)
            st = st.reshape(klen // SUBLANES, SUBLANES, nq)
            mx = all_sublanes(jnp.max(st, axis=0), jnp.maximum)
            p = jnp.exp2(st - mx)
            ls = all_sublanes(jnp.sum(p, axis=0), jnp.add)
            acc = jnp.dot(vt_ref[h, :, :klen], p.reshape(klen, nq).astype(BF16),
                          preferred_element_type=F32)
            att = acc.reshape(LANES // SUBLANES, SUBLANES, nq) / ls
            ot = att[:, :, :tq] - lam * att[:, :, tq:]
            ms = all_sublanes(jnp.sum(ot * ot, axis=0), jnp.add) * (1.0 / LANES)
            ot = (ot * lax.rsqrt(ms + SUBLN_EPS)).reshape(LANES, tq)
            o = ot.T * (subg_ref[...] * (1.0 - LAMBDA_INIT))
            o_ref[0, rows, hl] = (o * gb_ref[0, rows, hl].astype(F32)).astype(BF16)


def _diff_attn(dq, dk, dv, gb, subln_g, lam_params, *, tq, heads_per_step):
    B, T, W = dq.shape
    n_heads = W // LANES
    slopes = jnp.asarray([2.0 ** (-8.0 * (h + 1) / n_heads) for h in range(n_heads)], F32)
    slopes = jnp.broadcast_to(slopes[:, None, None], (n_heads, 1, LANES))
    pos = jnp.arange(T, dtype=jnp.int32)
    pos_lo = pos % KEY_POS_SPLIT
    pos_hi = pos - pos_lo
    kpos = jnp.zeros((T, LANES), F32)
    kpos = kpos.at[:, 0].set(pos_hi).at[:, 1].set(pos_lo).at[:, 2].set(pos_hi).at[:, 3].set(pos_lo)
    kpos = kpos.astype(BF16)
    hw = heads_per_step * LANES
    blk = pl.BlockSpec((1, T, hw), lambda b, h: (b, 0, h))
    subg = subln_g.reshape(1, LANES)
    return pl.pallas_call(
        functools.partial(_diff_attn_kernel, tq=tq),
        grid=(B, n_heads // heads_per_step),
        in_specs=[blk, blk, blk, blk,
                  pl.BlockSpec((T, LANES), lambda b, h: (0, 0)),
                  pl.BlockSpec((heads_per_step, 1, LANES), lambda b, h: (h, 0, 0)),
                  pl.BlockSpec((1, LANES), lambda b, h: (0, 0)),
                  pl.BlockSpec(lam_params.shape, lambda b, h: (0, 0))],
        out_specs=blk,
        out_shape=jax.ShapeDtypeStruct((B, T, W), BF16),
        scratch_shapes=[pltpu.VMEM((heads_per_step, LANES, T), BF16)],
        compiler_params=pltpu.CompilerParams(
            dimension_semantics=("arbitrary", "arbitrary"), vmem_limit_bytes=VMEM_LIMIT),
        name="diffattn",
    )(dq, dk, dv, gb, kpos, slopes, subg, lam_params)


def _out_proj_kernel(ya_ref, yb_ref, x_ref, w_ref, o_ref):
    wa = ya_ref.shape[2]
    acc = jnp.dot(ya_ref[0], w_ref[:wa, :], preferred_element_type=F32)
    acc = acc + jnp.dot(yb_ref[0], w_ref[wa:, :], preferred_element_type=F32)
    o_ref[0] = x_ref[0] + acc


def _out_proj(ya, yb, x, w_out, *, tm):
    B, T, D = x.shape
    row = lambda w: pl.BlockSpec((1, tm, w), lambda b, t: (b, t, 0))
    return pl.pallas_call(
        _out_proj_kernel,
        grid=(B, T // tm),
        in_specs=[row(ya.shape[2]), row(yb.shape[2]), row(D),
                  pl.BlockSpec(w_out.shape, lambda b, t: (0, 0))],
        out_specs=row(D),
        out_shape=jax.ShapeDtypeStruct((B, T, D), F32),
        compiler_params=pltpu.CompilerParams(
            dimension_semantics=("arbitrary", "arbitrary"), vmem_limit_bytes=VMEM_LIMIT),
        name="out_proj",
    )(ya, yb, x, w_out)


def kernel(x, norm_g, w_in, shift_mu, w0, w_decay_up, a0, w_iclr_up, k_k, k_a, r_k, ln_x_w, ln_x_b, q_norm_g, k_norm_g, lambda_q1, lambda_k1, lambda_q2, lambda_k2, subln_g, w_out):
    depth = norm_g.shape[0]
    assert depth == 1, "lambda_init is specialised to a single layer"
    h = x
    for l in range(depth):
        r, k, v, zwa, ga, dq, dk, dv, gb = _in_proj(
            h, norm_g[l], w_in[l].astype(BF16), shift_mu[l], q_norm_g[l], k_norm_g[l], tm=1024)
        ya = _rwkv(r, k, v, zwa, ga, w0[l], w_decay_up[l], a0[l], w_iclr_up[l], k_k[l], k_a[l],
                   r_k[l], ln_x_w[l], ln_x_b[l], bb=4, tb=128)
        lam_params = jnp.stack([lambda_q1[l], lambda_k1[l], lambda_q2[l], lambda_k2[l]])
        yb = _diff_attn(dq, dk, dv, gb, subln_g[l], lam_params, tq=512, heads_per_step=2)
        h = _out_proj(ya, yb, h, w_out[l].astype(BF16), tm=1024)
    return h
```

```python
import functools
import math

import jax
import jax.numpy as jnp
from jax import lax
from jax.experimental import pallas as pl
from jax.experimental.pallas import tpu as pltpu

F32 = jnp.float32
BF16 = jnp.bfloat16

LANES = 128
SUBLANES = 8
MXU_WIDTH = 256
HEAD = 64
CHUNK = 64
RWKV_W = 512
DIFF_W = 512
LORA = 64
SHIFT_COLS = 3 * RWKV_W + 2 * LORA
NORM_EPS = 1e-6
QK_NORM_EPS = 1e-6
SUBLN_EPS = 1e-5
RWKV_GN_EPS = 64e-5
LAMBDA_INIT = 0.8 - 0.6 * math.exp(-0.3 * 0)
NEG_INF = -1e30
LOG2E = math.log2(math.e)
KEY_POS_SPLIT = 16
VMEM_LIMIT = 56 * 1024 * 1024


def _dot(a, b):
    return jnp.dot(a.astype(BF16), b.astype(BF16), preferred_element_type=F32)


def _dot_nt(a, b):
    return lax.dot_general(a.astype(BF16), b.astype(BF16), (((1,), (1,)), ((), ())),
                           preferred_element_type=F32)


def _sigmoid(x):
    return 0.5 * jnp.tanh(0.5 * x) + 0.5


def _in_proj_kernel(x_ref, g_ref, w_ref, mu_ref, qg_ref, kg_ref, ones_ref,
                    r_ref, k_ref, v_ref, zwa_ref, ga_ref, dq_ref, dk_ref, dv_ref, gb_ref,
                    hn_ref, carry_ref):
    t = pl.program_id(1)
    x = x_ref[0]
    tm = x.shape[0]
    ms = jnp.mean(x * x, axis=-1, keepdims=True)
    hn_ref[...] = (x * lax.rsqrt(ms + NORM_EPS) * g_ref[...]).astype(BF16)

    @pl.when(t == 0)
    def _():
        carry_ref[...] = jnp.zeros_like(carry_ref)

    row0 = lax.broadcasted_iota(jnp.int32, (tm, 1), 0) == 0

    def proj(c0, c1):
        return jnp.dot(hn_ref[...], w_ref[:, c0:c1], preferred_element_type=F32)

    def shifted(c0, c1):
        p = proj(c0, c1)
        prev = jnp.where(row0, carry_ref[:, c0:c1], pltpu.roll(p, 1, 0))
        carry_ref[:, c0:c1] = p[tm - 1:tm, :]
        return p + (prev - p) * mu_ref[:, c0:c1]

    r_ref[0] = shifted(0, RWKV_W)
    k_ref[0] = shifted(RWKV_W, 2 * RWKV_W)
    v_ref[0] = shifted(2 * RWKV_W, 3 * RWKV_W)
    zwa_ref[0] = shifted(3 * RWKV_W, SHIFT_COLS)

    c = SHIFT_COLS
    g = proj(c, c + RWKV_W)
    ga_ref[0] = (g * _sigmoid(g)).astype(BF16)
    c += RWKV_W

    def qk_norm(p, gain):
        sq = (p * p).astype(BF16)
        w = ones_ref.shape[0]
        ms = jnp.concatenate(
            [jnp.dot(sq[:, c0:c0 + w], ones_ref[...], preferred_element_type=F32)
             for c0 in range(0, DIFF_W, w)], axis=1) * (1.0 / HEAD)
        return (p * lax.rsqrt(ms + QK_NORM_EPS) * gain).astype(BF16)

    dq_ref[0] = qk_norm(proj(c, c + DIFF_W), qg_ref[...] * (HEAD ** -0.5 * LOG2E))
    c += DIFF_W
    dk_ref[0] = qk_norm(proj(c, c + DIFF_W), kg_ref[...])
    c += DIFF_W
    dv_ref[0] = proj(c, c + DIFF_W).astype(BF16)
    c += DIFF_W
    g = proj(c, c + DIFF_W)
    gb_ref[0] = (g * _sigmoid(g)).astype(BF16)


def _in_proj(x, norm_g, w_in, shift_mu, q_norm_g, k_norm_g, *, tm):
    B, T, D = x.shape
    n_rep = DIFF_W // HEAD
    qg = jnp.tile(q_norm_g.reshape(1, HEAD), (1, n_rep))
    kg = jnp.tile(k_norm_g.reshape(1, HEAD), (1, n_rep))
    row = lambda w: pl.BlockSpec((1, tm, w), lambda b, t: (b, t, 0))
    full = lambda a: pl.BlockSpec(a.shape, lambda b, t: (0,) * a.ndim,
                                  pipeline_mode=pl.Buffered(1))
    f32o = lambda w: jax.ShapeDtypeStruct((B, T, w), F32)
    bf16o = lambda w: jax.ShapeDtypeStruct((B, T, w), BF16)
    g2 = norm_g.reshape(1, D)
    mu2 = shift_mu.reshape(1, SHIFT_COLS)
    group = jnp.arange(MXU_WIDTH) // HEAD
    ones_bd = (group[:, None] == group[None, :]).astype(BF16)
    return pl.pallas_call(
        _in_proj_kernel,
        grid=(B, T // tm),
        in_specs=[row(D), full(g2), full(w_in), full(mu2), full(qg), full(kg), full(ones_bd)],
        out_specs=[row(RWKV_W), row(RWKV_W), row(RWKV_W), row(2 * LORA), row(RWKV_W),
                   row(DIFF_W), row(DIFF_W), row(DIFF_W), row(DIFF_W)],
        out_shape=[f32o(RWKV_W), f32o(RWKV_W), f32o(RWKV_W), f32o(2 * LORA), bf16o(RWKV_W),
                   bf16o(DIFF_W), bf16o(DIFF_W), bf16o(DIFF_W), bf16o(DIFF_W)],
        scratch_shapes=[pltpu.VMEM((tm, D), BF16), pltpu.VMEM((1, SHIFT_COLS), F32)],
        compiler_params=pltpu.CompilerParams(
            dimension_semantics=("arbitrary", "arbitrary"), vmem_limit_bytes=VMEM_LIMIT),
        name="in_proj",
    )(x, g2, w_in, mu2, qg, kg, ones_bd)


def _rwkv_kernel(r_ref, k_ref, v_ref, zwa_ref, ga_ref, wd_ref, wa_ref, vec_ref,
                 o_ref, s_ref, y_ref):
    bb, tb, width = r_ref.shape
    rows_total = bb * tb
    chunks_per_seq = tb // CHUNK
    n_chunks = bb * chunks_per_seq
    n_pairs = width // LANES
    P2 = 2 * CHUNK

    @pl.when(pl.program_id(1) == 0)
    def _():
        s_ref[...] = jnp.zeros_like(s_ref)

    w0 = vec_ref[0:1, :]
    a0 = vec_ref[1:2, :]
    k_k = vec_ref[2:3, :]
    k_a = vec_ref[3:4, :]
    r_k = vec_ref[4:5, :]
    ln_w = vec_ref[5:6, :]
    ln_b = vec_ref[6:7, :]

    ri = lax.broadcasted_iota(jnp.int32, (P2, LANES), 0)
    ci = lax.broadcasted_iota(jnp.int32, (P2, LANES), 1)
    same_head = (ri // HEAD) == (ci // HEAD)
    ones_bd = jnp.where(same_head, 1.0, 0.0).astype(BF16)
    ti = lax.broadcasted_iota(jnp.int32, (CHUNK, LANES), 0)
    si = lax.broadcasted_iota(jnp.int32, (CHUNK, LANES), 1) % HEAD
    strict = si < ti
    incl = si <= ti
    eye = si == ti
    lane_blocks = [slice(p * LANES, (p + 1) * LANES) for p in range(n_pairs)]

    def group_sum(z):
        zb = z.astype(BF16)
        return jnp.concatenate(
            [jnp.dot(zb[:, lb], ones_bd, preferred_element_type=F32) for lb in lane_blocks],
            axis=1)

    def bd(z):
        return jnp.where(same_head, jnp.concatenate([z, z], axis=0), 0.0)

    def head_transposed(z):
        zt = bd(z).T
        return zt[:CHUNK] + zt[CHUNK:]

    flat = lambda ref: ref[...].reshape(rows_total, ref.shape[2])
    r = flat(r_ref)
    k = flat(k_ref)
    v = flat(v_ref)
    zwa = flat(zwa_ref)
    u = w0 + _dot(jnp.tanh(zwa), wd_ref[...])
    lw = -math.exp(-0.5) * _sigmoid(u)
    a = _sigmoid(a0 + _dot(zwa, wa_ref[...]))
    kkr = k * k_k
    kk = kkr * lax.rsqrt(jnp.maximum(group_sum(kkr * kkr), 1e-24))
    kmod = k * (1.0 + (a - 1.0) * k_a)

    tri_r = lax.broadcasted_iota(jnp.int32, (CHUNK, CHUNK), 0)
    tri_c = lax.broadcasted_iota(jnp.int32, (CHUNK, CHUNK), 1)
    lower = jnp.where(tri_c <= tri_r, 1.0, 0.0).astype(BF16)
    lw_hi = lw.astype(BF16)
    lw_lo = (lw - lw_hi.astype(F32)).astype(BF16)
    cs = jnp.concatenate(
        [jnp.dot(lower, lw_hi[c * CHUNK:(c + 1) * CHUNK], preferred_element_type=F32)
         + jnp.dot(lower, lw_lo[c * CHUNK:(c + 1) * CHUNK], preferred_element_type=F32)
         for c in range(n_chunks)], axis=0)
    e_pos = jnp.exp(cs)
    e_neg = jnp.exp(-cs)
    al_full = -kk * jnp.exp(cs - lw)
    rt_full = r * e_pos
    bt_full = kk * a * e_neg
    kt_full = kmod * e_neg

    items = [(c, p) for c in range(n_chunks) for p in range(n_pairs)]
    blk = lambda z, c, p: z[c * CHUNK:(c + 1) * CHUNK, lane_blocks[p]]
    e_tot = {it: blk(e_pos, *it)[CHUNK - 1:CHUNK, :] for it in items}
    al = {it: blk(al_full, *it) for it in items}
    rt = {it: blk(rt_full, *it) for it in items}
    bt = {it: blk(bt_full, *it) for it in items}
    kt = {it: blk(kt_full, *it) for it in items}
    v_bd = {it: bd(blk(v, *it)) for it in items}

    gram = {it: _dot_nt(jnp.concatenate([al[it], rt[it]], axis=0),
                        jnp.concatenate([bd(bt[it]), bd(kt[it])], axis=0))
            for it in items}
    a_ab = {it: jnp.where(strict, gram[it][:CHUNK, :LANES], 0.0) for it in items}
    a_kr = {it: jnp.concatenate([jnp.where(strict, gram[it][:CHUNK, LANES:], 0.0),
                                 jnp.where(incl, gram[it][CHUNK:, LANES:], 0.0)], axis=0)
            for it in items}
    a_rb = {it: jnp.where(incl, gram[it][CHUNK:, :LANES], 0.0) for it in items}

    n_levels = int(math.log2(CHUNK))
    tm = {it: jnp.where(eye, 1.0, a_ab[it]) for it in items}
    ap = {it: _dot(a_ab[it], bd(a_ab[it])) for it in items}
    for _ in range(1, n_levels - 1):
        pp = {it: _dot(jnp.concatenate([ap[it], tm[it]], axis=0), bd(ap[it]))
              for it in items}
        ap = {it: pp[it][:CHUNK] for it in items}
        tm = {it: tm[it] + pp[it][CHUNK:] for it in items}
    tm = {it: tm[it] + _dot(tm[it], bd(ap[it])) for it in items}

    av = {it: _dot(a_kr[it], v_bd[it]) for it in items}
    wu = {it: _dot(tm[it], jnp.concatenate([bd(al[it]), bd(av[it][:CHUNK])], axis=1))
          for it in items}
    wu_bd = {it: jnp.concatenate([bd(wu[it][:, :LANES]), bd(wu[it][:, LANES:])], axis=1)
             for it in items}
    ab = {it: _dot(a_rb[it], wu_bd[it]) for it in items}
    bk_t = {it: jnp.concatenate([head_transposed(bt[it] * e_tot[it]),
                                 head_transposed(kt[it] * e_tot[it])], axis=1)
            for it in items}
    gh = {it: _dot(bk_t[it], jnp.concatenate(
        [wu_bd[it], jnp.concatenate([jnp.zeros_like(v_bd[it]), v_bd[it]], axis=1)], axis=0))
          for it in items}
    gq = {it: jnp.concatenate([jnp.where(eye, e_tot[it], 0.0) + gh[it][:, :LANES],
                               rt[it] + ab[it][:, :LANES]], axis=0) for it in items}
    hm = {it: gh[it][:, LANES:] for it in items}
    y1 = {it: ab[it][:, LANES:] + av[it][CHUNK:] for it in items}

    for b in range(bb):
        for p in range(n_pairs):
            s = s_ref[b, p]
            for c in range(b * chunks_per_seq, (b + 1) * chunks_per_seq):
                it = (c, p)
                ys = _dot(gq[it], s)
                y_ref[c * CHUNK:(c + 1) * CHUNK, lane_blocks[p]] = ys[CHUNK:] + y1[it]
                s = bd(ys[:CHUNK] + hm[it])
            s_ref[b, p] = s

    y = y_ref[...]
    mu = group_sum(y) * (1.0 / HEAD)
    d = y - mu
    var = group_sum(d * d) * (1.0 / HEAD)
    yn = d * lax.rsqrt(var + RWKV_GN_EPS) * ln_w + ln_b
    bonus = group_sum(r * kmod * r_k) * v
    out = (yn + bonus) * flat(ga_ref).astype(F32)
    o_ref[...] = out.reshape(bb, tb, width).astype(BF16)


def _rwkv(r, k, v, zwa, ga, w0, w_decay_up, a0, w_iclr_up, k_k, k_a, r_k, ln_w, ln_b, *, bb, tb):
    B, T, W = r.shape
    n_pairs = W // LANES
    zeros = jnp.zeros((LORA, W), F32)
    wd = jnp.concatenate([w_decay_up, zeros], axis=0)
    wa = jnp.concatenate([zeros, w_iclr_up], axis=0)
    vec = jnp.stack([w0, a0, k_k, k_a, r_k.reshape(W), ln_w, ln_b, jnp.zeros((W,), F32)])
    blk = pl.BlockSpec((bb, tb, W), lambda b, t: (b, t, 0))
    full = lambda a: pl.BlockSpec(a.shape, lambda b, t: (0,) * a.ndim)
    return pl.pallas_call(
        _rwkv_kernel,
        grid=(B // bb, T // tb),
        in_specs=[blk, blk, blk,
                  pl.BlockSpec((bb, tb, 2 * LORA), lambda b, t: (b, t, 0)),
                  blk, full(wd), full(wa), full(vec)],
        out_specs=blk,
        out_shape=jax.ShapeDtypeStruct((B, T, W), BF16),
        scratch_shapes=[pltpu.VMEM((bb, n_pairs, LANES, LANES), F32),
                        pltpu.VMEM((bb * tb, W), F32)],
        compiler_params=pltpu.CompilerParams(
            dimension_semantics=("arbitrary", "arbitrary"), vmem_limit_bytes=VMEM_LIMIT),
        name="rwkv7",
    )(r, k, v, zwa, ga, wd, wa, vec)


def _diff_attn_kernel(q_ref, k_ref, v_ref, gb_ref, kpos_ref, slope_ref, subg_ref, lam_ref,
                      o_ref, vt_ref, *, tq):
    seq = q_ref.shape[1]
    n_heads = q_ref.shape[2] // LANES
    n_tiles = seq // tq
    nq = 2 * tq
    head_lanes = [slice(h * LANES, (h + 1) * LANES) for h in range(n_heads)]
    for h, hl in enumerate(head_lanes):
        for r0 in range(0, seq, tq):
            vt_ref[h, :, r0:r0 + tq] = v_ref[0, r0:r0 + tq, hl].astype(F32).T.astype(BF16)

    lane = lax.broadcasted_iota(jnp.int32, (tq, LANES), 1)
    lane1 = lax.broadcasted_iota(jnp.int32, (1, LANES), 1)
    half = tq // 2
    tri = (lax.broadcasted_iota(jnp.int32, (half, tq), 0)
           <= lax.broadcasted_iota(jnp.int32, (half, tq), 1) % half)

    def alibi_lanes(h):
        slope = slope_ref[h][:, 0:1] * LOG2E
        slope_hi = slope.astype(BF16).astype(F32)
        slope_lo = slope - slope_hi
        q_bias = jnp.where(lane1 < 2, slope_hi, jnp.where(lane1 < 4, slope_lo, 0.0))
        return jnp.broadcast_to(q_bias, (nq, LANES)).astype(BF16)

    q_bias = [alibi_lanes(h) for h in range(n_heads)]

    lam_p = lam_ref[...]
    s1 = jnp.sum(lam_p[0:1] * lam_p[1:2], axis=-1, keepdims=True)
    s2 = jnp.sum(lam_p[2:3] * lam_p[3:4], axis=-1, keepdims=True)
    lam = jnp.exp(s1) - jnp.exp(s2) + LAMBDA_INIT

    def all_sublanes(z, op):
        shift = SUBLANES // 2
        while shift:
            z = op(z, pltpu.roll(z, shift, 0))
            shift //= 2
        return z

    for i in range(n_tiles):
        rows = slice(i * tq, (i + 1) * tq)
        klen = (i + 1) * tq
        kmain = klen - half
        for h, hl in enumerate(head_lanes):
            q = q_ref[0, rows, hl]
            zero = jnp.zeros_like(q)
            q0 = jnp.where(lane < HEAD, q, zero)
            q1 = jnp.where(lane >= HEAD, q, zero)
            qcat = jnp.concatenate([q0[:half], q1[:half], q0[half:], q1[half:]], axis=0)
            qcat = jnp.concatenate([qcat, q_bias[h]], axis=1)
            keys = jnp.concatenate([k_ref[0, :klen, hl], kpos_ref[:klen, :]], axis=1)
            st = lax.dot_general(keys[:kmain], qcat, (((1,), (1,)), ((), ())),
                                 preferred_element_type=F32)
            sx = lax.dot_general(keys[kmain:], qcat[tq:], (((1,), (1,)), ((), ())),
                                 preferred_element_type=F32)
            last = st[i * tq:]
            last = jnp.concatenate([jnp.where(tri, last[:, :tq], NEG_INF), last[:, tq:]], axis=1)
            st = last if i == 0 else jnp.concatenate([st[:i * tq], last], axis=0)
            sx = jnp.where(tri, sx, NEG_INF)
            st = st.reshape(kmain // SUBLANES, SUBLANES, nq)
            sx = sx.reshape(half // SUBLANES, SUBLANES, tq)
            mx = all_sublanes(jnp.max(st, axis=0), jnp.maximum)
            mx_late = jnp.maximum(mx[:, tq:], all_sublanes(jnp.max(sx, axis=0), jnp.maximum))
            mx = jnp.concatenate([mx[:, :tq], mx_late], axis=1)
            p = jnp.exp2(st - mx)
            px = jnp.exp2(sx - mx_late)
            ls = all_sublanes(jnp.sum(p, axis=0), jnp.add)
            ls = jnp.concatenate(
                [ls[:, :tq], ls[:, tq:] + all_sublanes(jnp.sum(px, axis=0), jnp.add)], axis=1)
            acc = jnp.dot(vt_ref[h, :, :kmain], p.reshape(kmain, nq).astype(BF16),
                          preferred_element_type=F32)
            acc_late = acc[:, tq:] + jnp.dot(vt_ref[h, :, kmain:klen],
                                             px.reshape(half, tq).astype(BF16),
                                             preferred_element_type=F32)
            acc = jnp.concatenate([acc[:, :tq], acc_late], axis=1)
            att = acc.reshape(LANES // SUBLANES, SUBLANES, nq) / ls
            ot = jnp.concatenate(
                [att[:, :, :half] - lam * att[:, :, half:tq],
                 att[:, :, tq:tq + half] - lam * att[:, :, tq + half:]], axis=2)
            ms = all_sublanes(jnp.sum(ot * ot, axis=0), jnp.add) * (1.0 / LANES)
            ot = (ot * lax.rsqrt(ms + SUBLN_EPS)).reshape(LANES, tq)
            o = ot.T * (subg_ref[...] * (1.0 - LAMBDA_INIT))
            o_ref[0, rows, hl] = (o * gb_ref[0, rows, hl].astype(F32)).astype(BF16)


def _diff_attn(dq, dk, dv, gb, subln_g, lam_params, *, tq, heads_per_step):
    B, T, W = dq.shape
    n_heads = W // LANES
    slopes = jnp.asarray([2.0 ** (-8.0 * (h + 1) / n_heads) for h in range(n_heads)], F32)
    slopes = jnp.broadcast_to(slopes[:, None, None], (n_heads, 1, LANES))
    pos = jnp.arange(T, dtype=jnp.int32)
    pos_lo = pos % KEY_POS_SPLIT
    pos_hi = pos - pos_lo
    kpos = jnp.zeros((T, LANES), F32)
    kpos = kpos.at[:, 0].set(pos_hi).at[:, 1].set(pos_lo).at[:, 2].set(pos_hi).at[:, 3].set(pos_lo)
    kpos = kpos.astype(BF16)
    hw = heads_per_step * LANES
    blk = pl.BlockSpec((1, T, hw), lambda b, h: (b, 0, h))
    subg = subln_g.reshape(1, LANES)
    return pl.pallas_call(
        functools.partial(_diff_attn_kernel, tq=tq),
        grid=(B, n_heads // heads_per_step),
        in_specs=[blk, blk, blk, blk,
                  pl.BlockSpec((T, LANES), lambda b, h: (0, 0)),
                  pl.BlockSpec((heads_per_step, 1, LANES), lambda b, h: (h, 0, 0)),
                  pl.BlockSpec((1, LANES), lambda b, h: (0, 0)),
                  pl.BlockSpec(lam_params.shape, lambda b, h: (0, 0))],
        out_specs=blk,
        out_shape=jax.ShapeDtypeStruct((B, T, W), BF16),
        scratch_shapes=[pltpu.VMEM((heads_per_step, LANES, T), BF16)],
        compiler_params=pltpu.CompilerParams(
            dimension_semantics=("arbitrary", "arbitrary"), vmem_limit_bytes=VMEM_LIMIT),
        name="diffattn",
    )(dq, dk, dv, gb, kpos, slopes, subg, lam_params)


def _out_proj_kernel(ya_ref, yb_ref, x_ref, w_ref, o_ref):
    wa = ya_ref.shape[2]
    acc = jnp.dot(ya_ref[0], w_ref[:wa, :], preferred_element_type=F32)
    acc = acc + jnp.dot(yb_ref[0], w_ref[wa:, :], preferred_element_type=F32)
    o_ref[0] = x_ref[0] + acc


def _out_proj(ya, yb, x, w_out, *, tm):
    B, T, D = x.shape
    row = lambda w: pl.BlockSpec((1, tm, w), lambda b, t: (b, t, 0))
    return pl.pallas_call(
        _out_proj_kernel,
        grid=(B, T // tm),
        in_specs=[row(ya.shape[2]), row(yb.shape[2]), row(D),
                  pl.BlockSpec(w_out.shape, lambda b, t: (0, 0))],
        out_specs=row(D),
        out_shape=jax.ShapeDtypeStruct((B, T, D), F32),
        compiler_params=pltpu.CompilerParams(
            dimension_semantics=("arbitrary", "arbitrary"), vmem_limit_bytes=VMEM_LIMIT),
        name="out_proj",
    )(ya, yb, x, w_out)


def kernel(x, norm_g, w_in, shift_mu, w0, w_decay_up, a0, w_iclr_up, k_k, k_a, r_k, ln_x_w, ln_x_b, q_norm_g, k_norm_g, lambda_q1, lambda_k1, lambda_q2, lambda_k2, subln_g, w_out):
    depth = norm_g.shape[0]
    assert depth == 1, "lambda_init is specialised to a single layer"
    h = x
    for l in range(depth):
        r, k, v, zwa, ga, dq, dk, dv, gb = _in_proj(
            h, norm_g[l], w_in[l].astype(BF16), shift_mu[l], q_norm_g[l], k_norm_g[l], tm=1024)
        ya = _rwkv(r, k, v, zwa, ga, w0[l], w_decay_up[l], a0[l], w_iclr_up[l], k_k[l], k_a[l],
                   r_k[l], ln_x_w[l], ln_x_b[l], bb=4, tb=128)
        lam_params = jnp.stack([lambda_q1[l], lambda_k1[l], lambda_q2[l], lambda_k2[l]])
        yb = _diff_attn(dq, dk, dv, gb, subln_g[l], lam_params, tq=512, heads_per_step=2)
        h = _out_proj(ya, yb, h, w_out[l].astype(BF16), tm=1024)
    return h
```

```python
import functools
import math

import jax
import jax.numpy as jnp
from jax import lax
from jax.experimental import pallas as pl
from jax.experimental.pallas import tpu as pltpu

F32 = jnp.float32
BF16 = jnp.bfloat16

LANES = 128
SUBLANES = 8
MXU_WIDTH = 256
HEAD = 64
CHUNK = 64
RWKV_W = 512
DIFF_W = 512
LORA = 64
SHIFT_COLS = 3 * RWKV_W + 2 * LORA
NORM_EPS = 1e-6
QK_NORM_EPS = 1e-6
SUBLN_EPS = 1e-5
RWKV_GN_EPS = 64e-5
LAMBDA_INIT = 0.8 - 0.6 * math.exp(-0.3 * 0)
NEG_INF = -1e30
LOG2E = math.log2(math.e)
KEY_POS_SPLIT = 16
VMEM_LIMIT = 56 * 1024 * 1024


def _dot(a, b):
    return jnp.dot(a.astype(BF16), b.astype(BF16), preferred_element_type=F32)


def _dot_nt(a, b):
    return lax.dot_general(a.astype(BF16), b.astype(BF16), (((1,), (1,)), ((), ())),
                           preferred_element_type=F32)


def _sigmoid(x):
    return 0.5 * jnp.tanh(0.5 * x) + 0.5


def _in_proj_kernel(x_ref, g_ref, w_ref, mu_ref, qg_ref, kg_ref, ones_ref,
                    r_ref, k_ref, v_ref, zwa_ref, ga_ref, dq_ref, dk_ref, dv_ref, gb_ref,
                    hn_ref, carry_ref):
    t = pl.program_id(1)
    x = x_ref[0]
    tm = x.shape[0]
    ms = jnp.mean(x * x, axis=-1, keepdims=True)
    hn_ref[...] = (x * lax.rsqrt(ms + NORM_EPS) * g_ref[...]).astype(BF16)

    @pl.when(t == 0)
    def _():
        carry_ref[...] = jnp.zeros_like(carry_ref)

    row0 = lax.broadcasted_iota(jnp.int32, (tm, 1), 0) == 0

    def proj(c0, c1):
        return jnp.dot(hn_ref[...], w_ref[:, c0:c1].astype(BF16), preferred_element_type=F32)

    def shifted(c0, c1):
        p = proj(c0, c1)
        prev = jnp.where(row0, carry_ref[:, c0:c1], pltpu.roll(p, 1, 0))
        carry_ref[:, c0:c1] = p[tm - 1:tm, :]
        return p + (prev - p) * mu_ref[:, c0:c1]

    r_ref[0] = shifted(0, RWKV_W)
    k_ref[0] = shifted(RWKV_W, 2 * RWKV_W)
    v_ref[0] = shifted(2 * RWKV_W, 3 * RWKV_W)
    zwa_ref[0] = shifted(3 * RWKV_W, SHIFT_COLS)

    c = SHIFT_COLS
    g = proj(c, c + RWKV_W)
    ga_ref[0] = (g * _sigmoid(g)).astype(BF16)
    c += RWKV_W

    def qk_norm(p, gain):
        sq = (p * p).astype(BF16)
        w = ones_ref.shape[0]
        ms = jnp.concatenate(
            [jnp.dot(sq[:, c0:c0 + w], ones_ref[...], preferred_element_type=F32)
             for c0 in range(0, DIFF_W, w)], axis=1) * (1.0 / HEAD)
        return (p * lax.rsqrt(ms + QK_NORM_EPS) * gain).astype(BF16)

    dq_ref[0] = qk_norm(proj(c, c + DIFF_W), qg_ref[...] * (HEAD ** -0.5 * LOG2E))
    c += DIFF_W
    dk_ref[0] = qk_norm(proj(c, c + DIFF_W), kg_ref[...])
    c += DIFF_W
    dv_ref[0] = proj(c, c + DIFF_W).astype(BF16)
    c += DIFF_W
    g = proj(c, c + DIFF_W)
    gb_ref[0] = (g * _sigmoid(g)).astype(BF16)


def _in_proj(x, norm_g, w_in, shift_mu, q_norm_g, k_norm_g, *, tm):
    B, T, D = x.shape
    n_rep = DIFF_W // HEAD
    qg = jnp.tile(q_norm_g.reshape(1, HEAD), (1, n_rep))
    kg = jnp.tile(k_norm_g.reshape(1, HEAD), (1, n_rep))
    row = lambda w: pl.BlockSpec((1, tm, w), lambda b, t: (b, t, 0))
    full = lambda a: pl.BlockSpec(a.shape, lambda b, t: (0,) * a.ndim,
                                  pipeline_mode=pl.Buffered(1))
    f32o = lambda w: jax.ShapeDtypeStruct((B, T, w), F32)
    bf16o = lambda w: jax.ShapeDtypeStruct((B, T, w), BF16)
    g2 = norm_g.reshape(1, D)
    mu2 = shift_mu.reshape(1, SHIFT_COLS)
    group = jnp.arange(MXU_WIDTH) // HEAD
    ones_bd = (group[:, None] == group[None, :]).astype(BF16)
    return pl.pallas_call(
        _in_proj_kernel,
        grid=(B, T // tm),
        in_specs=[row(D), full(g2), full(w_in), full(mu2), full(qg), full(kg), full(ones_bd)],
        out_specs=[row(RWKV_W), row(RWKV_W), row(RWKV_W), row(2 * LORA), row(RWKV_W),
                   row(DIFF_W), row(DIFF_W), row(DIFF_W), row(DIFF_W)],
        out_shape=[f32o(RWKV_W), f32o(RWKV_W), f32o(RWKV_W), f32o(2 * LORA), bf16o(RWKV_W),
                   bf16o(DIFF_W), bf16o(DIFF_W), bf16o(DIFF_W), bf16o(DIFF_W)],
        scratch_shapes=[pltpu.VMEM((tm, D), BF16), pltpu.VMEM((1, SHIFT_COLS), F32)],
        compiler_params=pltpu.CompilerParams(
            dimension_semantics=("arbitrary", "arbitrary"), vmem_limit_bytes=VMEM_LIMIT),
        name="in_proj",
    )(x, g2, w_in, mu2, qg, kg, ones_bd)


def _rwkv_kernel(r_ref, k_ref, v_ref, zwa_ref, ga_ref, wd_ref, wa_ref, vec_ref,
                 o_ref, s_ref, y_ref):
    bb, tb, width = r_ref.shape
    rows_total = bb * tb
    chunks_per_seq = tb // CHUNK
    n_chunks = bb * chunks_per_seq
    n_pairs = width // LANES
    P2 = 2 * CHUNK

    @pl.when(pl.program_id(1) == 0)
    def _():
        s_ref[...] = jnp.zeros_like(s_ref)

    w0 = vec_ref[0:1, :]
    a0 = vec_ref[1:2, :]
    k_k = vec_ref[2:3, :]
    k_a = vec_ref[3:4, :]
    r_k = vec_ref[4:5, :]
    ln_w = vec_ref[5:6, :]
    ln_b = vec_ref[6:7, :]

    ri = lax.broadcasted_iota(jnp.int32, (P2, LANES), 0)
    ci = lax.broadcasted_iota(jnp.int32, (P2, LANES), 1)
    same_head = (ri // HEAD) == (ci // HEAD)
    ones_bd = jnp.where(same_head, 1.0, 0.0).astype(BF16)
    ti = lax.broadcasted_iota(jnp.int32, (CHUNK, LANES), 0)
    si = lax.broadcasted_iota(jnp.int32, (CHUNK, LANES), 1) % HEAD
    strict = si < ti
    incl = si <= ti
    eye = si == ti
    lane_blocks = [slice(p * LANES, (p + 1) * LANES) for p in range(n_pairs)]

    def group_sum(z):
        zb = z.astype(BF16)
        return jnp.concatenate(
            [jnp.dot(zb[:, lb], ones_bd, preferred_element_type=F32) for lb in lane_blocks],
            axis=1)

    def bd(z):
        return jnp.where(same_head, jnp.concatenate([z, z], axis=0), 0.0)

    def head_transposed(z):
        zt = bd(z).T
        return zt[:CHUNK] + zt[CHUNK:]

    flat = lambda ref: ref[...].reshape(rows_total, ref.shape[2])
    r = flat(r_ref)
    k = flat(k_ref)
    v = flat(v_ref)
    zwa = flat(zwa_ref)
    u = w0 + _dot(jnp.tanh(zwa), wd_ref[...])
    lw = -math.exp(-0.5) * _sigmoid(u)
    a = _sigmoid(a0 + _dot(zwa, wa_ref[...]))
    kkr = k * k_k
    kk = kkr * lax.rsqrt(jnp.maximum(group_sum(kkr * kkr), 1e-24))
    kmod = k * (1.0 + (a - 1.0) * k_a)

    tri_r = lax.broadcasted_iota(jnp.int32, (CHUNK, CHUNK), 0)
    tri_c = lax.broadcasted_iota(jnp.int32, (CHUNK, CHUNK), 1)
    lower = jnp.where(tri_c <= tri_r, 1.0, 0.0).astype(BF16)
    lw_hi = lw.astype(BF16)
    lw_lo = (lw - lw_hi.astype(F32)).astype(BF16)
    cs = jnp.concatenate(
        [jnp.dot(lower, lw_hi[c * CHUNK:(c + 1) * CHUNK], preferred_element_type=F32)
         + jnp.dot(lower, lw_lo[c * CHUNK:(c + 1) * CHUNK], preferred_element_type=F32)
         for c in range(n_chunks)], axis=0)
    e_pos = jnp.exp(cs)
    e_neg = jnp.exp(-cs)
    al_full = -kk * jnp.exp(cs - lw)
    rt_full = r * e_pos
    bt_full = kk * a * e_neg
    kt_full = kmod * e_neg

    items = [(c, p) for c in range(n_chunks) for p in range(n_pairs)]
    blk = lambda z, c, p: z[c * CHUNK:(c + 1) * CHUNK, lane_blocks[p]]
    e_tot = {it: blk(e_pos, *it)[CHUNK - 1:CHUNK, :] for it in items}
    al = {it: blk(al_full, *it) for it in items}
    rt = {it: blk(rt_full, *it) for it in items}
    bt = {it: blk(bt_full, *it) for it in items}
    kt = {it: blk(kt_full, *it) for it in items}
    v_bd = {it: bd(blk(v, *it)) for it in items}

    gram = {it: _dot_nt(jnp.concatenate([al[it], rt[it]], axis=0),
                        jnp.concatenate([bd(bt[it]), bd(kt[it])], axis=0))
            for it in items}
    a_ab = {it: jnp.where(strict, gram[it][:CHUNK, :LANES], 0.0) for it in items}
    a_kr = {it: jnp.concatenate([jnp.where(strict, gram[it][:CHUNK, LANES:], 0.0),
                                 jnp.where(incl, gram[it][CHUNK:, LANES:], 0.0)], axis=0)
            for it in items}
    a_rb = {it: jnp.where(incl, gram[it][CHUNK:, :LANES], 0.0) for it in items}

    n_levels = int(math.log2(CHUNK))
    tm = {it: jnp.where(eye, 1.0, a_ab[it]) for it in items}
    ap = {it: _dot(a_ab[it], bd(a_ab[it])) for it in items}
    for _ in range(1, n_levels - 1):
        pp = {it: _dot(jnp.concatenate([ap[it], tm[it]], axis=0), bd(ap[it]))
              for it in items}
        ap = {it: pp[it][:CHUNK] for it in items}
        tm = {it: tm[it] + pp[it][CHUNK:] for it in items}
    tm = {it: tm[it] + _dot(tm[it], bd(ap[it])) for it in items}

    av = {it: _dot(a_kr[it], v_bd[it]) for it in items}
    wu = {it: _dot(tm[it], jnp.concatenate([bd(al[it]), bd(av[it][:CHUNK])], axis=1))
          for it in items}
    wu_bd = {it: jnp.concatenate([bd(wu[it][:, :LANES]), bd(wu[it][:, LANES:])], axis=1)
             for it in items}
    ab = {it: _dot(a_rb[it], wu_bd[it]) for it in items}
    bk_t = {it: jnp.concatenate([head_transposed(bt[it] * e_tot[it]),
                                 head_transposed(kt[it] * e_tot[it])], axis=1)
            for it in items}
    gh = {it: _dot(bk_t[it], jnp.concatenate(
        [wu_bd[it], jnp.concatenate([jnp.zeros_like(v_bd[it]), v_bd[it]], axis=1)], axis=0))
          for it in items}
    gq = {it: jnp.concatenate([jnp.where(eye, e_tot[it], 0.0) + gh[it][:, :LANES],
                               rt[it] + ab[it][:, :LANES]], axis=0) for it in items}
    hm = {it: gh[it][:, LANES:] for it in items}
    y1 = {it: ab[it][:, LANES:] + av[it][CHUNK:] for it in items}

    for b in range(bb):
        for p in range(n_pairs):
            s = s_ref[b, p]
            for c in range(b * chunks_per_seq, (b + 1) * chunks_per_seq):
                it = (c, p)
                ys = _dot(gq[it], s)
                y_ref[c * CHUNK:(c + 1) * CHUNK, lane_blocks[p]] = ys[CHUNK:] + y1[it]
                s = bd(ys[:CHUNK] + hm[it])
            s_ref[b, p] = s

    y = y_ref[...]
    mu = group_sum(y) * (1.0 / HEAD)
    d = y - mu
    var = group_sum(d * d) * (1.0 / HEAD)
    yn = d * lax.rsqrt(var + RWKV_GN_EPS) * ln_w + ln_b
    bonus = group_sum(r * kmod * r_k) * v
    out = (yn + bonus) * flat(ga_ref).astype(F32)
    o_ref[...] = out.reshape(bb, tb, width).astype(BF16)


def _rwkv(r, k, v, zwa, ga, w0, w_decay_up, a0, w_iclr_up, k_k, k_a, r_k, ln_w, ln_b, *, bb, tb):
    B, T, W = r.shape
    n_pairs = W // LANES
    zeros = jnp.zeros((LORA, W), F32)
    wd = jnp.concatenate([w_decay_up, zeros], axis=0)
    wa = jnp.concatenate([zeros, w_iclr_up], axis=0)
    vec = jnp.stack([w0, a0, k_k, k_a, r_k.reshape(W), ln_w, ln_b, jnp.zeros((W,), F32)])
    blk = pl.BlockSpec((bb, tb, W), lambda b, t: (b, t, 0))
    full = lambda a: pl.BlockSpec(a.shape, lambda b, t: (0,) * a.ndim)
    return pl.pallas_call(
        _rwkv_kernel,
        grid=(B // bb, T // tb),
        in_specs=[blk, blk, blk,
                  pl.BlockSpec((bb, tb, 2 * LORA), lambda b, t: (b, t, 0)),
                  blk, full(wd), full(wa), full(vec)],
        out_specs=blk,
        out_shape=jax.ShapeDtypeStruct((B, T, W), BF16),
        scratch_shapes=[pltpu.VMEM((bb, n_pairs, LANES, LANES), F32),
                        pltpu.VMEM((bb * tb, W), F32)],
        compiler_params=pltpu.CompilerParams(
            dimension_semantics=("arbitrary", "arbitrary"), vmem_limit_bytes=VMEM_LIMIT),
        name="rwkv7",
    )(r, k, v, zwa, ga, wd, wa, vec)


def _diff_attn_kernel(q_ref, k_ref, v_ref, gb_ref, kpos_ref, slope_ref, subg_ref, lam_ref,
                      o_ref, vt_ref, *, tq):
    seq = q_ref.shape[1]
    n_heads = q_ref.shape[2] // LANES
    n_tiles = seq // tq
    nq = 2 * tq
    head_lanes = [slice(h * LANES, (h + 1) * LANES) for h in range(n_heads)]
    for h, hl in enumerate(head_lanes):
        for r0 in range(0, seq, tq):
            vt_ref[h, :, r0:r0 + tq] = v_ref[0, r0:r0 + tq, hl].astype(F32).T.astype(BF16)

    lane = lax.broadcasted_iota(jnp.int32, (tq, LANES), 1)
    lane1 = lax.broadcasted_iota(jnp.int32, (1, LANES), 1)
    half = tq // 2
    tri = (lax.broadcasted_iota(jnp.int32, (half, tq), 0)
           <= lax.broadcasted_iota(jnp.int32, (half, tq), 1) % half)

    def alibi_lanes(h):
        slope = slope_ref[h][:, 0:1] * LOG2E
        slope_hi = slope.astype(BF16).astype(F32)
        slope_lo = slope - slope_hi
        q_bias = jnp.where(lane1 < 2, slope_hi, jnp.where(lane1 < 4, slope_lo, 0.0))
        return jnp.broadcast_to(q_bias, (nq, LANES)).astype(BF16)

    q_bias = [alibi_lanes(h) for h in range(n_heads)]

    lam_p = lam_ref[...]
    s1 = jnp.sum(lam_p[0:1] * lam_p[1:2], axis=-1, keepdims=True)
    s2 = jnp.sum(lam_p[2:3] * lam_p[3:4], axis=-1, keepdims=True)
    lam = jnp.exp(s1) - jnp.exp(s2) + LAMBDA_INIT

    def all_sublanes(z, op):
        shift = SUBLANES // 2
        while shift:
            z = op(z, pltpu.roll(z, shift, 0))
            shift //= 2
        return z

    for i in range(n_tiles):
        rows = slice(i * tq, (i + 1) * tq)
        klen = (i + 1) * tq
        kmain = klen - half
        for h, hl in enumerate(head_lanes):
            q = q_ref[0, rows, hl]
            zero = jnp.zeros_like(q)
            q0 = jnp.where(lane < HEAD, q, zero)
            q1 = jnp.where(lane >= HEAD, q, zero)
            qcat = jnp.concatenate([q0[:half], q1[:half], q0[half:], q1[half:]], axis=0)
            qcat = jnp.concatenate([qcat, q_bias[h]], axis=1)
            keys = jnp.concatenate([k_ref[0, :klen, hl], kpos_ref[:klen, :]], axis=1)
            st = lax.dot_general(keys[:kmain], qcat, (((1,), (1,)), ((), ())),
                                 preferred_element_type=F32)
            sx = lax.dot_general(keys[kmain:], qcat[tq:], (((1,), (1,)), ((), ())),
                                 preferred_element_type=F32)
            last = st[i * tq:]
            last = jnp.concatenate([jnp.where(tri, last[:, :tq], NEG_INF), last[:, tq:]], axis=1)
            st = last if i == 0 else jnp.concatenate([st[:i * tq], last], axis=0)
            sx = jnp.where(tri, sx, NEG_INF)
            st = st.reshape(kmain // SUBLANES, SUBLANES, nq)
            sx = sx.reshape(half // SUBLANES, SUBLANES, tq)
            mx = all_sublanes(jnp.max(st, axis=0), jnp.maximum)
            mx_late = jnp.maximum(mx[:, tq:], all_sublanes(jnp.max(sx, axis=0), jnp.maximum))
            mx = jnp.concatenate([mx[:, :tq], mx_late], axis=1)
            p = jnp.exp2(st - mx)
            px = jnp.exp2(sx - mx_late)
            ls = all_sublanes(jnp.sum(p, axis=0), jnp.add)
            ls = jnp.concatenate(
                [ls[:, :tq], ls[:, tq:] + all_sublanes(jnp.sum(px, axis=0), jnp.add)], axis=1)
            acc = jnp.dot(vt_ref[h, :, :kmain], p.reshape(kmain, nq).astype(BF16),
                          preferred_element_type=F32)
            acc_late = acc[:, tq:] + jnp.dot(vt_ref[h, :, kmain:klen],
                                             px.reshape(half, tq).astype(BF16),
                                             preferred_element_type=F32)
            acc = jnp.concatenate([acc[:, :tq], acc_late], axis=1)
            att = acc.reshape(LANES // SUBLANES, SUBLANES, nq) / ls
            ot = jnp.concatenate(
                [att[:, :, :half] - lam * att[:, :, half:tq],
                 att[:, :, tq:tq + half] - lam * att[:, :, tq + half:]], axis=2)
            ms = all_sublanes(jnp.sum(ot * ot, axis=0), jnp.add) * (1.0 / LANES)
            ot = (ot * lax.rsqrt(ms + SUBLN_EPS)).reshape(LANES, tq)
            o = ot.T * (subg_ref[...] * (1.0 - LAMBDA_INIT))
            o_ref[0, rows, hl] = (o * gb_ref[0, rows, hl].astype(F32)).astype(BF16)


def _diff_attn(dq, dk, dv, gb, subln_g, lam_params, *, tq, heads_per_step):
    B, T, W = dq.shape
    n_heads = W // LANES
    slopes = jnp.asarray([2.0 ** (-8.0 * (h + 1) / n_heads) for h in range(n_heads)], F32)
    slopes = jnp.broadcast_to(slopes[:, None, None], (n_heads, 1, LANES))
    pos = jnp.arange(T, dtype=jnp.int32)
    pos_lo = pos % KEY_POS_SPLIT
    pos_hi = pos - pos_lo
    kpos = jnp.zeros((T, LANES), F32)
    kpos = kpos.at[:, 0].set(pos_hi).at[:, 1].set(pos_lo).at[:, 2].set(pos_hi).at[:, 3].set(pos_lo)
    kpos = kpos.astype(BF16)
    hw = heads_per_step * LANES
    blk = pl.BlockSpec((1, T, hw), lambda b, h: (b, 0, h))
    subg = subln_g.reshape(1, LANES)
    return pl.pallas_call(
        functools.partial(_diff_attn_kernel, tq=tq),
        grid=(B, n_heads // heads_per_step),
        in_specs=[blk, blk, blk, blk,
                  pl.BlockSpec((T, LANES), lambda b, h: (0, 0)),
                  pl.BlockSpec((heads_per_step, 1, LANES), lambda b, h: (h, 0, 0)),
                  pl.BlockSpec((1, LANES), lambda b, h: (0, 0)),
                  pl.BlockSpec(lam_params.shape, lambda b, h: (0, 0))],
        out_specs=blk,
        out_shape=jax.ShapeDtypeStruct((B, T, W), BF16),
        scratch_shapes=[pltpu.VMEM((heads_per_step, LANES, T), BF16)],
        compiler_params=pltpu.CompilerParams(
            dimension_semantics=("arbitrary", "arbitrary"), vmem_limit_bytes=VMEM_LIMIT),
        name="diffattn",
    )(dq, dk, dv, gb, kpos, slopes, subg, lam_params)


def _out_proj_kernel(ya_ref, yb_ref, x_ref, w_ref, o_ref):
    wa = ya_ref.shape[2]
    acc = jnp.dot(ya_ref[0], w_ref[:wa, :].astype(BF16), preferred_element_type=F32)
    acc = acc + jnp.dot(yb_ref[0], w_ref[wa:, :].astype(BF16), preferred_element_type=F32)
    o_ref[0] = x_ref[0] + acc


def _out_proj(ya, yb, x, w_out, *, tm):
    B, T, D = x.shape
    row = lambda w: pl.BlockSpec((1, tm, w), lambda b, t: (b, t, 0))
    return pl.pallas_call(
        _out_proj_kernel,
        grid=(B, T // tm),
        in_specs=[row(ya.shape[2]), row(yb.shape[2]), row(D),
                  pl.BlockSpec(w_out.shape, lambda b, t: (0, 0))],
        out_specs=row(D),
        out_shape=jax.ShapeDtypeStruct((B, T, D), F32),
        compiler_params=pltpu.CompilerParams(
            dimension_semantics=("arbitrary", "arbitrary"), vmem_limit_bytes=VMEM_LIMIT),
        name="out_proj",
    )(ya, yb, x, w_out)


def kernel(x, norm_g, w_in, shift_mu, w0, w_decay_up, a0, w_iclr_up, k_k, k_a, r_k, ln_x_w, ln_x_b, q_norm_g, k_norm_g, lambda_q1, lambda_k1, lambda_q2, lambda_k2, subln_g, w_out):
    depth = norm_g.shape[0]
    assert depth == 1, "lambda_init is specialised to a single layer"
    h = x
    for l in range(depth):
        r, k, v, zwa, ga, dq, dk, dv, gb = _in_proj(
            h, norm_g[l], w_in[l], shift_mu[l], q_norm_g[l], k_norm_g[l], tm=1024)
        ya = _rwkv(r, k, v, zwa, ga, w0[l], w_decay_up[l], a0[l], w_iclr_up[l], k_k[l], k_a[l],
                   r_k[l], ln_x_w[l], ln_x_b[l], bb=8, tb=128)
        lam_params = jnp.stack([lambda_q1[l], lambda_k1[l], lambda_q2[l], lambda_k2[l]])
        yb = _diff_attn(dq, dk, dv, gb, subln_g[l], lam_params, tq=512, heads_per_step=4)
        h = _out_proj(ya, yb, h, w_out[l], tm=1024)
    return h
```

```python
import functools
import math

import jax
import jax.numpy as jnp
import numpy as np
from jax import lax
from jax.experimental import pallas as pl
from jax.experimental.pallas import tpu as pltpu

F32 = jnp.float32
BF16 = jnp.bfloat16

LANES = 128
SUBLANES = 8
MXU_WIDTH = 256
HEAD = 64
CHUNK = 64
RWKV_W = 512
DIFF_W = 512
LORA = 64
SHIFT_COLS = 3 * RWKV_W + 2 * LORA
NORM_EPS = 1e-6
QK_NORM_EPS = 1e-6
SUBLN_EPS = 1e-5
RWKV_GN_EPS = 64e-5
LAMBDA_INIT = 0.8 - 0.6 * math.exp(-0.3 * 0)
NEG_INF = -1e30
LOG2E = math.log2(math.e)
KEY_POS_SPLIT = 16
VMEM_LIMIT = 56 * 1024 * 1024


def _dot(a, b):
    return jnp.dot(a.astype(BF16), b.astype(BF16), preferred_element_type=F32)


def _dot_nt(a, b):
    return lax.dot_general(a.astype(BF16), b.astype(BF16), (((1,), (1,)), ((), ())),
                           preferred_element_type=F32)


def _sigmoid(x):
    return 0.5 * jnp.tanh(0.5 * x) + 0.5


def _in_proj_kernel(x_ref, g_ref, w_ref, mu_ref, qg_ref, kg_ref, ones_ref,
                    r_ref, k_ref, v_ref, zwa_ref, ga_ref, dq_ref, dk_ref, dv_ref, gb_ref,
                    hn_ref, carry_ref):
    t = pl.program_id(1)
    x = x_ref[0]
    tm = x.shape[0]
    ms = jnp.mean(x * x, axis=-1, keepdims=True)
    hn_ref[...] = (x * lax.rsqrt(ms + NORM_EPS) * g_ref[...]).astype(BF16)

    @pl.when(t == 0)
    def _():
        carry_ref[...] = jnp.zeros_like(carry_ref)

    row0 = lax.broadcasted_iota(jnp.int32, (tm, 1), 0) == 0

    def proj(c0, c1):
        return jnp.dot(hn_ref[...], w_ref[:, c0:c1].astype(BF16), preferred_element_type=F32)

    def shifted(c0, c1):
        p = proj(c0, c1)
        prev = jnp.where(row0, carry_ref[:, c0:c1], pltpu.roll(p, 1, 0))
        carry_ref[:, c0:c1] = p[tm - 1:tm, :]
        return p + (prev - p) * mu_ref[:, c0:c1]

    r_ref[0] = shifted(0, RWKV_W)
    k_ref[0] = shifted(RWKV_W, 2 * RWKV_W)
    v_ref[0] = shifted(2 * RWKV_W, 3 * RWKV_W)
    zwa_ref[0] = shifted(3 * RWKV_W, SHIFT_COLS)

    c = SHIFT_COLS
    g = proj(c, c + RWKV_W)
    ga_ref[0] = (g * _sigmoid(g)).astype(BF16)
    c += RWKV_W

    def qk_norm(p, gain):
        sq = (p * p).astype(BF16)
        w = ones_ref.shape[0]
        ms = jnp.concatenate(
            [jnp.dot(sq[:, c0:c0 + w], ones_ref[...], preferred_element_type=F32)
             for c0 in range(0, DIFF_W, w)], axis=1) * (1.0 / HEAD)
        return (p * lax.rsqrt(ms + QK_NORM_EPS) * gain).astype(BF16)

    dq_ref[0] = qk_norm(proj(c, c + DIFF_W), qg_ref[...] * (HEAD ** -0.5 * LOG2E))
    c += DIFF_W
    dk_ref[0] = qk_norm(proj(c, c + DIFF_W), kg_ref[...])
    c += DIFF_W
    dv_ref[0] = proj(c, c + DIFF_W).astype(BF16)
    c += DIFF_W
    g = proj(c, c + DIFF_W)
    gb_ref[0] = (g * _sigmoid(g)).astype(BF16)


def _in_proj(x, norm_g, w_in, shift_mu, q_norm_g, k_norm_g, *, tm):
    B, T, D = x.shape
    n_rep = DIFF_W // HEAD
    qg = jnp.tile(q_norm_g.reshape(1, HEAD), (1, n_rep))
    kg = jnp.tile(k_norm_g.reshape(1, HEAD), (1, n_rep))
    row = lambda w: pl.BlockSpec((1, tm, w), lambda b, t: (b, t, 0))
    full = lambda a: pl.BlockSpec(a.shape, lambda b, t: (0,) * a.ndim,
                                  pipeline_mode=pl.Buffered(1))
    f32o = lambda w: jax.ShapeDtypeStruct((B, T, w), F32)
    bf16o = lambda w: jax.ShapeDtypeStruct((B, T, w), BF16)
    g2 = norm_g.reshape(1, D)
    mu2 = shift_mu.reshape(1, SHIFT_COLS)
    group = np.arange(MXU_WIDTH) // HEAD
    ones_bd = jnp.asarray(group[:, None] == group[None, :], BF16)
    return pl.pallas_call(
        _in_proj_kernel,
        grid=(B, T // tm),
        in_specs=[row(D), full(g2), full(w_in), full(mu2), full(qg), full(kg), full(ones_bd)],
        out_specs=[row(RWKV_W), row(RWKV_W), row(RWKV_W), row(2 * LORA), row(RWKV_W),
                   row(DIFF_W), row(DIFF_W), row(DIFF_W), row(DIFF_W)],
        out_shape=[f32o(RWKV_W), f32o(RWKV_W), f32o(RWKV_W), f32o(2 * LORA), bf16o(RWKV_W),
                   bf16o(DIFF_W), bf16o(DIFF_W), bf16o(DIFF_W), bf16o(DIFF_W)],
        scratch_shapes=[pltpu.VMEM((tm, D), BF16), pltpu.VMEM((1, SHIFT_COLS), F32)],
        compiler_params=pltpu.CompilerParams(
            dimension_semantics=("arbitrary", "arbitrary"), vmem_limit_bytes=VMEM_LIMIT),
        name="in_proj",
    )(x, g2, w_in, mu2, qg, kg, ones_bd)


def _rwkv_kernel(r_ref, k_ref, v_ref, zwa_ref, ga_ref, wd_ref, wa_ref, vec_ref,
                 o_ref, s_ref, y_ref):
    bb, tb, width = r_ref.shape
    rows_total = bb * tb
    chunks_per_seq = tb // CHUNK
    n_chunks = bb * chunks_per_seq
    n_pairs = width // LANES
    P2 = 2 * CHUNK

    @pl.when(pl.program_id(1) == 0)
    def _():
        s_ref[...] = jnp.zeros_like(s_ref)

    w0 = vec_ref[0:1, :]
    a0 = vec_ref[1:2, :]
    k_k = vec_ref[2:3, :]
    k_a = vec_ref[3:4, :]
    r_k = vec_ref[4:5, :]
    ln_w = vec_ref[5:6, :]
    ln_b = vec_ref[6:7, :]

    ri = lax.broadcasted_iota(jnp.int32, (P2, LANES), 0)
    ci = lax.broadcasted_iota(jnp.int32, (P2, LANES), 1)
    same_head = (ri // HEAD) == (ci // HEAD)
    ones_bd = jnp.where(same_head, 1.0, 0.0).astype(BF16)
    ti = lax.broadcasted_iota(jnp.int32, (CHUNK, LANES), 0)
    si = lax.broadcasted_iota(jnp.int32, (CHUNK, LANES), 1) % HEAD
    strict = si < ti
    incl = si <= ti
    eye = si == ti
    lane_blocks = [slice(p * LANES, (p + 1) * LANES) for p in range(n_pairs)]

    def group_sum(z):
        zb = z.astype(BF16)
        return jnp.concatenate(
            [jnp.dot(zb[:, lb], ones_bd, preferred_element_type=F32) for lb in lane_blocks],
            axis=1)

    def bd(z):
        return jnp.where(same_head, jnp.concatenate([z, z], axis=0), 0.0)

    def head_transposed(z):
        zt = bd(z).T
        return zt[:CHUNK] + zt[CHUNK:]

    flat = lambda ref: ref[...].reshape(rows_total, ref.shape[2])
    r = flat(r_ref)
    k = flat(k_ref)
    v = flat(v_ref)
    zwa = flat(zwa_ref)
    u = w0 + _dot(jnp.tanh(zwa), wd_ref[...])
    lw = -math.exp(-0.5) * _sigmoid(u)
    a = _sigmoid(a0 + _dot(zwa, wa_ref[...]))
    kkr = k * k_k
    kk = kkr * lax.rsqrt(jnp.maximum(group_sum(kkr * kkr), 1e-24))
    kmod = k * (1.0 + (a - 1.0) * k_a)

    tri_r = lax.broadcasted_iota(jnp.int32, (CHUNK, CHUNK), 0)
    tri_c = lax.broadcasted_iota(jnp.int32, (CHUNK, CHUNK), 1)
    lower = jnp.where(tri_c <= tri_r, 1.0, 0.0).astype(BF16)
    lw_hi = lw.astype(BF16)
    lw_lo = (lw - lw_hi.astype(F32)).astype(BF16)
    cs = jnp.concatenate(
        [jnp.dot(lower, lw_hi[c * CHUNK:(c + 1) * CHUNK], preferred_element_type=F32)
         + jnp.dot(lower, lw_lo[c * CHUNK:(c + 1) * CHUNK], preferred_element_type=F32)
         for c in range(n_chunks)], axis=0)
    e_pos = jnp.exp(cs)
    e_neg = jnp.exp(-cs)
    al_full = -kk * jnp.exp(cs - lw)
    rt_full = r * e_pos
    bt_full = kk * a * e_neg
    kt_full = kmod * e_neg

    items = [(c, p) for c in range(n_chunks) for p in range(n_pairs)]
    blk = lambda z, c, p: z[c * CHUNK:(c + 1) * CHUNK, lane_blocks[p]]
    e_tot = {it: blk(e_pos, *it)[CHUNK - 1:CHUNK, :] for it in items}
    al = {it: blk(al_full, *it) for it in items}
    rt = {it: blk(rt_full, *it) for it in items}
    bt = {it: blk(bt_full, *it) for it in items}
    kt = {it: blk(kt_full, *it) for it in items}
    v_bd = {it: bd(blk(v, *it)) for it in items}

    gram = {it: _dot_nt(jnp.concatenate([al[it], rt[it]], axis=0),
                        jnp.concatenate([bd(bt[it]), bd(kt[it])], axis=0))
            for it in items}
    a_ab = {it: jnp.where(strict, gram[it][:CHUNK, :LANES], 0.0) for it in items}
    a_kr = {it: jnp.concatenate([jnp.where(strict, gram[it][:CHUNK, LANES:], 0.0),
                                 jnp.where(incl, gram[it][CHUNK:, LANES:], 0.0)], axis=0)
            for it in items}
    a_rb = {it: jnp.where(incl, gram[it][CHUNK:, :LANES], 0.0) for it in items}

    n_levels = int(math.log2(CHUNK))
    tm = {it: jnp.where(eye, 1.0, a_ab[it]) for it in items}
    ap = {it: _dot(a_ab[it], bd(a_ab[it])) for it in items}
    for _ in range(1, n_levels - 1):
        pp = {it: _dot(jnp.concatenate([ap[it], tm[it]], axis=0), bd(ap[it]))
              for it in items}
        ap = {it: pp[it][:CHUNK] for it in items}
        tm = {it: tm[it] + pp[it][CHUNK:] for it in items}
    tm = {it: tm[it] + _dot(tm[it], bd(ap[it])) for it in items}

    av = {it: _dot(a_kr[it], v_bd[it]) for it in items}
    wu = {it: _dot(tm[it], jnp.concatenate([bd(al[it]), bd(av[it][:CHUNK])], axis=1))
          for it in items}
    wu_bd = {it: jnp.concatenate([bd(wu[it][:, :LANES]), bd(wu[it][:, LANES:])], axis=1)
             for it in items}
    ab = {it: _dot(a_rb[it], wu_bd[it]) for it in items}
    bk_t = {it: jnp.concatenate([head_transposed(bt[it] * e_tot[it]),
                                 head_transposed(kt[it] * e_tot[it])], axis=1)
            for it in items}
    gh = {it: _dot(bk_t[it], jnp.concatenate(
        [wu_bd[it], jnp.concatenate([jnp.zeros_like(v_bd[it]), v_bd[it]], axis=1)], axis=0))
          for it in items}
    gq = {it: jnp.concatenate([jnp.where(eye, e_tot[it], 0.0) + gh[it][:, :LANES],
                               rt[it] + ab[it][:, :LANES]], axis=0) for it in items}
    hm = {it: gh[it][:, LANES:] for it in items}
    y1 = {it: ab[it][:, LANES:] + av[it][CHUNK:] for it in items}

    for b in range(bb):
        for p in range(n_pairs):
            s = s_ref[b, p]
            for c in range(b * chunks_per_seq, (b + 1) * chunks_per_seq):
                it = (c, p)
                ys = _dot(gq[it], s)
                y_ref[c * CHUNK:(c + 1) * CHUNK, lane_blocks[p]] = ys[CHUNK:] + y1[it]
                s = bd(ys[:CHUNK] + hm[it])
            s_ref[b, p] = s

    y = y_ref[...]
    mu = group_sum(y) * (1.0 / HEAD)
    d = y - mu
    var = group_sum(d * d) * (1.0 / HEAD)
    yn = d * lax.rsqrt(var + RWKV_GN_EPS) * ln_w + ln_b
    bonus = group_sum(r * kmod * r_k) * v
    out = (yn + bonus) * flat(ga_ref).astype(F32)
    o_ref[...] = out.reshape(bb, tb, width).astype(BF16)


def _rwkv(r, k, v, zwa, ga, w0, w_decay_up, a0, w_iclr_up, k_k, k_a, r_k, ln_w, ln_b, *, bb, tb):
    B, T, W = r.shape
    n_pairs = W // LANES
    zeros = jnp.zeros((LORA, W), F32)
    wd = jnp.concatenate([w_decay_up, zeros], axis=0)
    wa = jnp.concatenate([zeros, w_iclr_up], axis=0)
    vec = jnp.stack([w0, a0, k_k, k_a, r_k.reshape(W), ln_w, ln_b, jnp.zeros((W,), F32)])
    blk = pl.BlockSpec((bb, tb, W), lambda b, t: (b, t, 0))
    full = lambda a: pl.BlockSpec(a.shape, lambda b, t: (0,) * a.ndim)
    return pl.pallas_call(
        _rwkv_kernel,
        grid=(B // bb, T // tb),
        in_specs=[blk, blk, blk,
                  pl.BlockSpec((bb, tb, 2 * LORA), lambda b, t: (b, t, 0)),
                  blk, full(wd), full(wa), full(vec)],
        out_specs=blk,
        out_shape=jax.ShapeDtypeStruct((B, T, W), BF16),
        scratch_shapes=[pltpu.VMEM((bb, n_pairs, LANES, LANES), F32),
                        pltpu.VMEM((bb * tb, W), F32)],
        compiler_params=pltpu.CompilerParams(
            dimension_semantics=("arbitrary", "arbitrary"), vmem_limit_bytes=VMEM_LIMIT),
        name="rwkv7",
    )(r, k, v, zwa, ga, wd, wa, vec)


def _diff_attn_kernel(q_ref, k_ref, v_ref, gb_ref, kpos_ref, slope_ref, subg_ref, lam_ref,
                      o_ref, vt_ref, *, tq):
    seq = q_ref.shape[1]
    n_heads = q_ref.shape[2] // LANES
    n_tiles = seq // tq
    nq = 2 * tq
    head_lanes = [slice(h * LANES, (h + 1) * LANES) for h in range(n_heads)]
    for h, hl in enumerate(head_lanes):
        for r0 in range(0, seq, tq):
            vt_ref[h, :, r0:r0 + tq] = v_ref[0, r0:r0 + tq, hl].astype(F32).T.astype(BF16)

    lane = lax.broadcasted_iota(jnp.int32, (tq, LANES), 1)
    lane1 = lax.broadcasted_iota(jnp.int32, (1, LANES), 1)
    half = tq // 2
    tri = (lax.broadcasted_iota(jnp.int32, (half, tq), 0)
           <= lax.broadcasted_iota(jnp.int32, (half, tq), 1) % half)

    def alibi_lanes(h):
        slope = slope_ref[h][:, 0:1] * LOG2E
        slope_hi = slope.astype(BF16).astype(F32)
        slope_lo = slope - slope_hi
        q_bias = jnp.where(lane1 < 2, slope_hi, jnp.where(lane1 < 4, slope_lo, 0.0))
        return jnp.broadcast_to(q_bias, (nq, LANES)).astype(BF16)

    q_bias = [alibi_lanes(h) for h in range(n_heads)]

    lam_p = lam_ref[...]
    s1 = jnp.sum(lam_p[0:1] * lam_p[1:2], axis=-1, keepdims=True)
    s2 = jnp.sum(lam_p[2:3] * lam_p[3:4], axis=-1, keepdims=True)
    lam = jnp.exp(s1) - jnp.exp(s2) + LAMBDA_INIT

    def all_sublanes(z, op):
        shift = SUBLANES // 2
        while shift:
            z = op(z, pltpu.roll(z, shift, 0))
            shift //= 2
        return z

    for i in range(n_tiles):
        rows = slice(i * tq, (i + 1) * tq)
        klen = (i + 1) * tq
        kmain = klen - half
        for h, hl in enumerate(head_lanes):
            q = q_ref[0, rows, hl]
            zero = jnp.zeros_like(q)
            q0 = jnp.where(lane < HEAD, q, zero)
            q1 = jnp.where(lane >= HEAD, q, zero)
            qcat = jnp.concatenate([q0[:half], q1[:half], q0[half:], q1[half:]], axis=0)
            qcat = jnp.concatenate([qcat, q_bias[h]], axis=1)
            keys = jnp.concatenate([k_ref[0, :klen, hl], kpos_ref[:klen, :]], axis=1)
            st = lax.dot_general(keys[:kmain], qcat, (((1,), (1,)), ((), ())),
                                 preferred_element_type=F32)
            sx = lax.dot_general(keys[kmain:], qcat[tq:], (((1,), (1,)), ((), ())),
                                 preferred_element_type=F32)
            last = st[i * tq:]
            last = jnp.concatenate([jnp.where(tri, last[:, :tq], NEG_INF), last[:, tq:]], axis=1)
            st = last if i == 0 else jnp.concatenate([st[:i * tq], last], axis=0)
            sx = jnp.where(tri, sx, NEG_INF)
            st = st.reshape(kmain // SUBLANES, SUBLANES, nq)
            sx = sx.reshape(half // SUBLANES, SUBLANES, tq)
            mx = all_sublanes(jnp.max(st, axis=0), jnp.maximum)
            mx_late = jnp.maximum(mx[:, tq:], all_sublanes(jnp.max(sx, axis=0), jnp.maximum))
            mx = jnp.concatenate([mx[:, :tq], mx_late], axis=1)
            p = jnp.exp2(st - mx)
            px = jnp.exp2(sx - mx_late)
            ls = all_sublanes(jnp.sum(p, axis=0), jnp.add)
            ls = jnp.concatenate(
                [ls[:, :tq], ls[:, tq:] + all_sublanes(jnp.sum(px, axis=0), jnp.add)], axis=1)
            acc = jnp.dot(vt_ref[h, :, :kmain], p.reshape(kmain, nq).astype(BF16),
                          preferred_element_type=F32)
            acc_late = acc[:, tq:] + jnp.dot(vt_ref[h, :, kmain:klen],
                                             px.reshape(half, tq).astype(BF16),
                                             preferred_element_type=F32)
            acc = jnp.concatenate([acc[:, :tq], acc_late], axis=1)
            att = acc.reshape(LANES // SUBLANES, SUBLANES, nq) / ls
            ot = jnp.concatenate(
                [att[:, :, :half] - lam * att[:, :, half:tq],
                 att[:, :, tq:tq + half] - lam * att[:, :, tq + half:]], axis=2)
            ms = all_sublanes(jnp.sum(ot * ot, axis=0), jnp.add) * (1.0 / LANES)
            ot = (ot * lax.rsqrt(ms + SUBLN_EPS)).reshape(LANES, tq)
            o = ot.T * (subg_ref[...] * (1.0 - LAMBDA_INIT))
            o_ref[0, rows, hl] = (o * gb_ref[0, rows, hl].astype(F32)).astype(BF16)


def _diff_attn(dq, dk, dv, gb, subln_g, lam_params, *, tq, heads_per_step):
    B, T, W = dq.shape
    n_heads = W // LANES
    slopes = np.asarray([2.0 ** (-8.0 * (h + 1) / n_heads) for h in range(n_heads)], np.float32)
    slopes = jnp.asarray(np.broadcast_to(slopes[:, None, None], (n_heads, 1, LANES)))
    pos = np.arange(T)
    pos_lo = pos % KEY_POS_SPLIT
    pos_hi = pos - pos_lo
    kpos = np.zeros((T, LANES), np.float32)
    kpos[:, 0], kpos[:, 1], kpos[:, 2], kpos[:, 3] = pos_hi, pos_lo, pos_hi, pos_lo
    kpos = jnp.asarray(kpos, BF16)
    hw = heads_per_step * LANES
    blk = pl.BlockSpec((1, T, hw), lambda b, h: (b, 0, h))
    subg = subln_g.reshape(1, LANES)
    return pl.pallas_call(
        functools.partial(_diff_attn_kernel, tq=tq),
        grid=(B, n_heads // heads_per_step),
        in_specs=[blk, blk, blk, blk,
                  pl.BlockSpec((T, LANES), lambda b, h: (0, 0)),
                  pl.BlockSpec((heads_per_step, 1, LANES), lambda b, h: (h, 0, 0)),
                  pl.BlockSpec((1, LANES), lambda b, h: (0, 0)),
                  pl.BlockSpec(lam_params.shape, lambda b, h: (0, 0))],
        out_specs=blk,
        out_shape=jax.ShapeDtypeStruct((B, T, W), BF16),
        scratch_shapes=[pltpu.VMEM((heads_per_step, LANES, T), BF16)],
        compiler_params=pltpu.CompilerParams(
            dimension_semantics=("arbitrary", "arbitrary"), vmem_limit_bytes=VMEM_LIMIT),
        name="diffattn",
    )(dq, dk, dv, gb, kpos, slopes, subg, lam_params)


def _out_proj_kernel(ya_ref, yb_ref, x_ref, w_ref, o_ref):
    wa = ya_ref.shape[2]
    acc = jnp.dot(ya_ref[0], w_ref[:wa, :].astype(BF16), preferred_element_type=F32)
    acc = acc + jnp.dot(yb_ref[0], w_ref[wa:, :].astype(BF16), preferred_element_type=F32)
    o_ref[0] = x_ref[0] + acc


def _out_proj(ya, yb, x, w_out, *, tm):
    B, T, D = x.shape
    row = lambda w: pl.BlockSpec((1, tm, w), lambda b, t: (b, t, 0))
    return pl.pallas_call(
        _out_proj_kernel,
        grid=(B, T // tm),
        in_specs=[row(ya.shape[2]), row(yb.shape[2]), row(D),
                  pl.BlockSpec(w_out.shape, lambda b, t: (0, 0))],
        out_specs=row(D),
        out_shape=jax.ShapeDtypeStruct((B, T, D), F32),
        compiler_params=pltpu.CompilerParams(
            dimension_semantics=("arbitrary", "arbitrary"), vmem_limit_bytes=VMEM_LIMIT),
        name="out_proj",
    )(ya, yb, x, w_out)


def kernel(x, norm_g, w_in, shift_mu, w0, w_decay_up, a0, w_iclr_up, k_k, k_a, r_k, ln_x_w, ln_x_b, q_norm_g, k_norm_g, lambda_q1, lambda_k1, lambda_q2, lambda_k2, subln_g, w_out):
    depth = norm_g.shape[0]
    assert depth == 1, "lambda_init is specialised to a single layer"
    h = x
    for l in range(depth):
        r, k, v, zwa, ga, dq, dk, dv, gb = _in_proj(
            h, norm_g[l], w_in[l], shift_mu[l], q_norm_g[l], k_norm_g[l], tm=1024)
        ya = _rwkv(r, k, v, zwa, ga, w0[l], w_decay_up[l], a0[l], w_iclr_up[l], k_k[l], k_a[l],
                   r_k[l], ln_x_w[l], ln_x_b[l], bb=8, tb=128)
        lam_params = jnp.stack([lambda_q1[l], lambda_k1[l], lambda_q2[l], lambda_k2[l]])
        yb = _diff_attn(dq, dk, dv, gb, subln_g[l], lam_params, tq=512, heads_per_step=4)
        h = _out_proj(ya, yb, h, w_out[l], tm=1024)
    return h
```

```python
import functools
import math

import jax
import jax.numpy as jnp
import numpy as np
from jax import lax
from jax.experimental import pallas as pl
from jax.experimental.pallas import tpu as pltpu

F32 = jnp.float32
BF16 = jnp.bfloat16

LANES = 128
SUBLANES = 8
MXU_WIDTH = 256
HEAD = 64
CHUNK = 64
RWKV_W = 512
DIFF_W = 512
LORA = 64
SHIFT_COLS = 3 * RWKV_W + 2 * LORA
NORM_EPS = 1e-6
QK_NORM_EPS = 1e-6
SUBLN_EPS = 1e-5
RWKV_GN_EPS = 64e-5
LAMBDA_INIT = 0.8 - 0.6 * math.exp(-0.3 * 0)
NEG_INF = -1e30
LOG2E = math.log2(math.e)
KEY_POS_SPLIT = 16
VMEM_LIMIT = 56 * 1024 * 1024


def _dot(a, b):
    return jnp.dot(a.astype(BF16), b.astype(BF16), preferred_element_type=F32)


def _dot_nt(a, b):
    return lax.dot_general(a.astype(BF16), b.astype(BF16), (((1,), (1,)), ((), ())),
                           preferred_element_type=F32)


def _sigmoid(x):
    return 0.5 * jnp.tanh(0.5 * x) + 0.5


def _in_proj_kernel(x_ref, g_ref, w_ref, mu_ref, qg_ref, kg_ref, ones_ref,
                    r_ref, k_ref, v_ref, zwa_ref, ga_ref, dq_ref, dk_ref, dv_ref, gb_ref,
                    hn_ref, carry_ref):
    t = pl.program_id(1)
    x = x_ref[0]
    tm = x.shape[0]
    ms = jnp.mean(x * x, axis=-1, keepdims=True)
    hn_ref[...] = (x * lax.rsqrt(ms + NORM_EPS) * g_ref[...]).astype(BF16)

    @pl.when(t == 0)
    def _():
        carry_ref[...] = jnp.zeros_like(carry_ref)

    row0 = lax.broadcasted_iota(jnp.int32, (tm, 1), 0) == 0

    def proj(c0, c1):
        return jnp.dot(hn_ref[...], w_ref[:, c0:c1].astype(BF16), preferred_element_type=F32)

    def shifted(c0, c1):
        p = proj(c0, c1)
        prev = jnp.where(row0, carry_ref[:, c0:c1], pltpu.roll(p, 1, 0))
        carry_ref[:, c0:c1] = p[tm - 1:tm, :]
        return p + (prev - p) * mu_ref[:, c0:c1]

    r_ref[0] = shifted(0, RWKV_W)
    k_ref[0] = shifted(RWKV_W, 2 * RWKV_W)
    v_ref[0] = shifted(2 * RWKV_W, 3 * RWKV_W)
    zwa_ref[0] = shifted(3 * RWKV_W, SHIFT_COLS)

    c = SHIFT_COLS
    g = proj(c, c + RWKV_W)
    ga_ref[0] = (g * _sigmoid(g)).astype(BF16)
    c += RWKV_W

    def qk_norm(p, gain):
        sq = (p * p).astype(BF16)
        w = ones_ref.shape[0]
        ms = jnp.concatenate(
            [jnp.dot(sq[:, c0:c0 + w], ones_ref[...], preferred_element_type=F32)
             for c0 in range(0, DIFF_W, w)], axis=1) * (1.0 / HEAD)
        return (p * lax.rsqrt(ms + QK_NORM_EPS) * gain).astype(BF16)

    dq_ref[0] = qk_norm(proj(c, c + DIFF_W), qg_ref[...] * (HEAD ** -0.5 * LOG2E))
    c += DIFF_W
    dk_ref[0] = qk_norm(proj(c, c + DIFF_W), kg_ref[...])
    c += DIFF_W
    dv_ref[0] = proj(c, c + DIFF_W).astype(BF16)
    c += DIFF_W
    g = proj(c, c + DIFF_W)
    gb_ref[0] = (g * _sigmoid(g)).astype(BF16)


def _in_proj(x, norm_g, w_in, shift_mu, q_norm_g, k_norm_g, *, tm):
    B, T, D = x.shape
    n_rep = DIFF_W // HEAD
    qg = jnp.tile(q_norm_g.reshape(1, HEAD), (1, n_rep))
    kg = jnp.tile(k_norm_g.reshape(1, HEAD), (1, n_rep))
    row = lambda w: pl.BlockSpec((1, tm, w), lambda b, t: (b, t, 0))
    full = lambda a: pl.BlockSpec(a.shape, lambda b, t: (0,) * a.ndim,
                                  pipeline_mode=pl.Buffered(1))
    f32o = lambda w: jax.ShapeDtypeStruct((B, T, w), F32)
    bf16o = lambda w: jax.ShapeDtypeStruct((B, T, w), BF16)
    g2 = norm_g.reshape(1, D)
    mu2 = shift_mu.reshape(1, SHIFT_COLS)
    group = np.arange(MXU_WIDTH) // HEAD
    ones_bd = jnp.asarray(group[:, None] == group[None, :], BF16)
    return pl.pallas_call(
        _in_proj_kernel,
        grid=(B, T // tm),
        in_specs=[row(D), full(g2), full(w_in), full(mu2), full(qg), full(kg), full(ones_bd)],
        out_specs=[row(RWKV_W), row(RWKV_W), row(RWKV_W), row(2 * LORA), row(RWKV_W),
                   row(DIFF_W), row(DIFF_W), row(DIFF_W), row(DIFF_W)],
        out_shape=[f32o(RWKV_W), f32o(RWKV_W), f32o(RWKV_W), f32o(2 * LORA), bf16o(RWKV_W),
                   bf16o(DIFF_W), bf16o(DIFF_W), bf16o(DIFF_W), bf16o(DIFF_W)],
        scratch_shapes=[pltpu.VMEM((tm, D), BF16), pltpu.VMEM((1, SHIFT_COLS), F32)],
        compiler_params=pltpu.CompilerParams(
            dimension_semantics=("arbitrary", "arbitrary"), vmem_limit_bytes=VMEM_LIMIT),
        name="in_proj",
    )(x, g2, w_in, mu2, qg, kg, ones_bd)


def _rwkv_kernel(r_ref, k_ref, v_ref, zwa_ref, ga_ref, wd_ref, wa_ref, vec_ref,
                 o_ref, s_ref, y_ref):
    bb, tb, width = r_ref.shape
    rows_total = bb * tb
    chunks_per_seq = tb // CHUNK
    n_chunks = bb * chunks_per_seq
    n_pairs = width // LANES
    P2 = 2 * CHUNK

    @pl.when(pl.program_id(1) == 0)
    def _():
        s_ref[...] = jnp.zeros_like(s_ref)

    w0 = vec_ref[0:1, :]
    a0 = vec_ref[1:2, :]
    k_k = vec_ref[2:3, :]
    k_a = vec_ref[3:4, :]
    r_k = vec_ref[4:5, :]
    ln_w = vec_ref[5:6, :]
    ln_b = vec_ref[6:7, :]

    ri = lax.broadcasted_iota(jnp.int32, (P2, LANES), 0)
    ci = lax.broadcasted_iota(jnp.int32, (P2, LANES), 1)
    same_head = (ri // HEAD) == (ci // HEAD)
    ones_bd = jnp.where(same_head, 1.0, 0.0).astype(BF16)
    ti = lax.broadcasted_iota(jnp.int32, (CHUNK, LANES), 0)
    si = lax.broadcasted_iota(jnp.int32, (CHUNK, LANES), 1) % HEAD
    strict = si < ti
    incl = si <= ti
    eye = si == ti
    lane_blocks = [slice(p * LANES, (p + 1) * LANES) for p in range(n_pairs)]

    def group_sum(z):
        zb = z.astype(BF16)
        return jnp.concatenate(
            [jnp.dot(zb[:, lb], ones_bd, preferred_element_type=F32) for lb in lane_blocks],
            axis=1)

    def bd(z):
        return jnp.where(same_head, jnp.concatenate([z, z], axis=0), 0.0)

    def head_transposed(z):
        zt = bd(z).T
        return zt[:CHUNK] + zt[CHUNK:]

    flat = lambda ref: ref[...].reshape(rows_total, ref.shape[2])
    r = flat(r_ref)
    k = flat(k_ref)
    v = flat(v_ref)
    zwa = flat(zwa_ref)
    u = w0 + _dot(jnp.tanh(zwa), wd_ref[...])
    lw = -math.exp(-0.5) * _sigmoid(u)
    a = _sigmoid(a0 + _dot(zwa, wa_ref[...]))
    kkr = k * k_k
    kk = kkr * lax.rsqrt(jnp.maximum(group_sum(kkr * kkr), 1e-24))
    kmod = k * (1.0 + (a - 1.0) * k_a)

    tri_r = lax.broadcasted_iota(jnp.int32, (CHUNK, CHUNK), 0)
    tri_c = lax.broadcasted_iota(jnp.int32, (CHUNK, CHUNK), 1)
    lower = jnp.where(tri_c <= tri_r, 1.0, 0.0).astype(BF16)
    lw_hi = lw.astype(BF16)
    lw_lo = (lw - lw_hi.astype(F32)).astype(BF16)
    cs = jnp.concatenate(
        [jnp.dot(lower, lw_hi[c * CHUNK:(c + 1) * CHUNK], preferred_element_type=F32)
         + jnp.dot(lower, lw_lo[c * CHUNK:(c + 1) * CHUNK], preferred_element_type=F32)
         for c in range(n_chunks)], axis=0)
    e_pos = jnp.exp(cs)
    e_neg = jnp.exp(-cs)
    al_full = -kk * jnp.exp(cs - lw)
    rt_full = r * e_pos
    bt_full = kk * a * e_neg
    kt_full = kmod * e_neg

    items = [(c, p) for c in range(n_chunks) for p in range(n_pairs)]
    blk = lambda z, c, p: z[c * CHUNK:(c + 1) * CHUNK, lane_blocks[p]]
    e_tot = {it: blk(e_pos, *it)[CHUNK - 1:CHUNK, :] for it in items}
    al = {it: blk(al_full, *it) for it in items}
    rt = {it: blk(rt_full, *it) for it in items}
    bt = {it: blk(bt_full, *it) for it in items}
    kt = {it: blk(kt_full, *it) for it in items}
    v_bd = {it: bd(blk(v, *it)) for it in items}

    gram = {it: _dot_nt(jnp.concatenate([al[it], rt[it]], axis=0),
                        jnp.concatenate([bd(bt[it]), bd(kt[it])], axis=0))
            for it in items}
    a_ab = {it: jnp.where(strict, gram[it][:CHUNK, :LANES], 0.0) for it in items}
    a_kr = {it: jnp.concatenate([jnp.where(strict, gram[it][:CHUNK, LANES:], 0.0),
                                 jnp.where(incl, gram[it][CHUNK:, LANES:], 0.0)], axis=0)
            for it in items}
    a_rb = {it: jnp.where(incl, gram[it][CHUNK:, :LANES], 0.0) for it in items}

    n_levels = int(math.log2(CHUNK))
    tm = {it: jnp.where(eye, 1.0, a_ab[it]) for it in items}
    ap = {it: _dot(a_ab[it], bd(a_ab[it])) for it in items}
    for _ in range(1, n_levels - 1):
        pp = {it: _dot(jnp.concatenate([ap[it], tm[it]], axis=0), bd(ap[it]))
              for it in items}
        ap = {it: pp[it][:CHUNK] for it in items}
        tm = {it: tm[it] + pp[it][CHUNK:] for it in items}
    tm = {it: tm[it] + _dot(tm[it], bd(ap[it])) for it in items}

    av = {it: _dot(a_kr[it], v_bd[it]) for it in items}
    wu = {it: _dot(tm[it], jnp.concatenate([bd(al[it]), bd(av[it][:CHUNK])], axis=1))
          for it in items}
    wu_bd = {it: jnp.concatenate([bd(wu[it][:, :LANES]), bd(wu[it][:, LANES:])], axis=1)
             for it in items}
    ab = {it: _dot(a_rb[it], wu_bd[it]) for it in items}
    bk_t = {it: jnp.concatenate([head_transposed(bt[it] * e_tot[it]),
                                 head_transposed(kt[it] * e_tot[it])], axis=1)
            for it in items}
    gh = {it: _dot(bk_t[it], jnp.concatenate(
        [wu_bd[it], jnp.concatenate([jnp.zeros_like(v_bd[it]), v_bd[it]], axis=1)], axis=0))
          for it in items}
    gq = {it: jnp.concatenate([jnp.where(eye, e_tot[it], 0.0) + gh[it][:, :LANES],
                               rt[it] + ab[it][:, :LANES]], axis=0) for it in items}
    hm = {it: gh[it][:, LANES:] for it in items}
    y1 = {it: ab[it][:, LANES:] + av[it][CHUNK:] for it in items}

    for b in range(bb):
        for p in range(n_pairs):
            s = s_ref[b, p]
            for c in range(b * chunks_per_seq, (b + 1) * chunks_per_seq):
                it = (c, p)
                ys = _dot(gq[it], s)
                y_ref[c * CHUNK:(c + 1) * CHUNK, lane_blocks[p]] = ys[CHUNK:] + y1[it]
                s = bd(ys[:CHUNK] + hm[it])
            s_ref[b, p] = s

    y = y_ref[...]
    mu = group_sum(y) * (1.0 / HEAD)
    d = y - mu
    var = group_sum(d * d) * (1.0 / HEAD)
    yn = d * lax.rsqrt(var + RWKV_GN_EPS) * ln_w + ln_b
    bonus = group_sum(r * kmod * r_k) * v
    out = (yn + bonus) * flat(ga_ref).astype(F32)
    o_ref[...] = out.reshape(bb, tb, width).astype(BF16)


def _rwkv(r, k, v, zwa, ga, w0, w_decay_up, a0, w_iclr_up, k_k, k_a, r_k, ln_w, ln_b, *, bb, tb):
    B, T, W = r.shape
    n_pairs = W // LANES
    zeros = jnp.zeros((LORA, W), F32)
    wd = jnp.concatenate([w_decay_up, zeros], axis=0)
    wa = jnp.concatenate([zeros, w_iclr_up], axis=0)
    vec = jnp.stack([w0, a0, k_k, k_a, r_k.reshape(W), ln_w, ln_b, jnp.zeros((W,), F32)])
    blk = pl.BlockSpec((bb, tb, W), lambda b, t: (b, t, 0))
    full = lambda a: pl.BlockSpec(a.shape, lambda b, t: (0,) * a.ndim)
    return pl.pallas_call(
        _rwkv_kernel,
        grid=(B // bb, T // tb),
        in_specs=[blk, blk, blk,
                  pl.BlockSpec((bb, tb, 2 * LORA), lambda b, t: (b, t, 0)),
                  blk, full(wd), full(wa), full(vec)],
        out_specs=blk,
        out_shape=jax.ShapeDtypeStruct((B, T, W), BF16),
        scratch_shapes=[pltpu.VMEM((bb, n_pairs, LANES, LANES), F32),
                        pltpu.VMEM((bb * tb, W), F32)],
        compiler_params=pltpu.CompilerParams(
            dimension_semantics=("arbitrary", "arbitrary"), vmem_limit_bytes=VMEM_LIMIT),
        name="rwkv7",
    )(r, k, v, zwa, ga, wd, wa, vec)


def _diff_attn_kernel(q_ref, k_ref, v_ref, gb_ref, kpos_ref, slope_ref, subg_ref, lam_ref,
                      o_ref, vt_ref, *, tq):
    seq = q_ref.shape[1]
    n_heads = q_ref.shape[2] // LANES
    n_tiles = seq // tq
    nq = 2 * tq
    head_lanes = [slice(h * LANES, (h + 1) * LANES) for h in range(n_heads)]
    for h, hl in enumerate(head_lanes):
        for r0 in range(0, seq, tq):
            vt_ref[h, :, r0:r0 + tq] = v_ref[0, r0:r0 + tq, hl].astype(F32).T.astype(BF16)

    lane = lax.broadcasted_iota(jnp.int32, (tq, LANES), 1)
    lane1 = lax.broadcasted_iota(jnp.int32, (1, LANES), 1)
    half = tq // 2
    tri = (lax.broadcasted_iota(jnp.int32, (half, tq), 0)
           <= lax.broadcasted_iota(jnp.int32, (half, tq), 1) % half)

    def alibi_lanes(h):
        slope = slope_ref[h][:, 0:1] * LOG2E
        slope_hi = slope.astype(BF16).astype(F32)
        slope_lo = slope - slope_hi
        q_bias = jnp.where(lane1 < 2, slope_hi, jnp.where(lane1 < 4, slope_lo, 0.0))
        return jnp.broadcast_to(q_bias, (nq, LANES)).astype(BF16)

    q_bias = [alibi_lanes(h) for h in range(n_heads)]

    lam_p = lam_ref[...]
    s1 = jnp.sum(lam_p[0:1] * lam_p[1:2], axis=-1, keepdims=True)
    s2 = jnp.sum(lam_p[2:3] * lam_p[3:4], axis=-1, keepdims=True)
    lam = jnp.exp(s1) - jnp.exp(s2) + LAMBDA_INIT

    def all_sublanes(z, op):
        shift = SUBLANES // 2
        while shift:
            z = op(z, pltpu.roll(z, shift, 0))
            shift //= 2
        return z

    for i in range(n_tiles):
        rows = slice(i * tq, (i + 1) * tq)
        klen = (i + 1) * tq
        kmain = klen - half
        for h, hl in enumerate(head_lanes):
            q = q_ref[0, rows, hl]
            zero = jnp.zeros_like(q)
            q0 = jnp.where(lane < HEAD, q, zero)
            q1 = jnp.where(lane >= HEAD, q, zero)
            qcat = jnp.concatenate([q0[:half], q1[:half], q0[half:], q1[half:]], axis=0)
            qcat = jnp.concatenate([qcat, q_bias[h]], axis=1)
            keys = jnp.concatenate([k_ref[0, :klen, hl], kpos_ref[:klen, :]], axis=1)
            st = lax.dot_general(keys[:kmain], qcat, (((1,), (1,)), ((), ())),
                                 preferred_element_type=F32)
            sx = lax.dot_general(keys[kmain:], qcat[tq:], (((1,), (1,)), ((), ())),
                                 preferred_element_type=F32)
            last = st[i * tq:]
            last = jnp.concatenate([jnp.where(tri, last[:, :tq], NEG_INF), last[:, tq:]], axis=1)
            st = last if i == 0 else jnp.concatenate([st[:i * tq], last], axis=0)
            sx = jnp.where(tri, sx, NEG_INF)
            st = st.reshape(kmain // SUBLANES, SUBLANES, nq)
            sx = sx.reshape(half // SUBLANES, SUBLANES, tq)
            mx = all_sublanes(jnp.max(st, axis=0), jnp.maximum)
            mx_late = jnp.maximum(mx[:, tq:], all_sublanes(jnp.max(sx, axis=0), jnp.maximum))
            mx = jnp.concatenate([mx[:, :tq], mx_late], axis=1)
            p = jnp.exp2(st - mx)
            px = jnp.exp2(sx - mx_late)
            ls = all_sublanes(jnp.sum(p, axis=0), jnp.add)
            ls = jnp.concatenate(
                [ls[:, :tq], ls[:, tq:] + all_sublanes(jnp.sum(px, axis=0), jnp.add)], axis=1)
            acc = jnp.dot(vt_ref[h, :, :kmain], p.reshape(kmain, nq).astype(BF16),
                          preferred_element_type=F32)
            acc_late = acc[:, tq:] + jnp.dot(vt_ref[h, :, kmain:klen],
                                             px.reshape(half, tq).astype(BF16),
                                             preferred_element_type=F32)
            acc = jnp.concatenate([acc[:, :tq], acc_late], axis=1)
            att = acc.reshape(LANES // SUBLANES, SUBLANES, nq) / ls
            ot = jnp.concatenate(
                [att[:, :, :half] - lam * att[:, :, half:tq],
                 att[:, :, tq:tq + half] - lam * att[:, :, tq + half:]], axis=2)
            ms = all_sublanes(jnp.sum(ot * ot, axis=0), jnp.add) * (1.0 / LANES)
            ot = (ot * lax.rsqrt(ms + SUBLN_EPS)).reshape(LANES, tq)
            o = ot.T * (subg_ref[...] * (1.0 - LAMBDA_INIT))
            o_ref[0, rows, hl] = (o * gb_ref[0, rows, hl].astype(F32)).astype(BF16)


def _diff_attn(dq, dk, dv, gb, subln_g, lam_params, *, tq, heads_per_step):
    B, T, W = dq.shape
    n_heads = W // LANES
    slopes = np.asarray([2.0 ** (-8.0 * (h + 1) / n_heads) for h in range(n_heads)], np.float32)
    slopes = jnp.asarray(np.broadcast_to(slopes[:, None, None], (n_heads, 1, LANES)))
    pos = lax.broadcasted_iota(jnp.int32, (T, LANES), 0)
    col = lax.broadcasted_iota(jnp.int32, (T, LANES), 1)
    pos_lo = pos % KEY_POS_SPLIT
    kpos = jnp.where(col < 4, jnp.where(col % 2 == 0, pos - pos_lo, pos_lo), 0).astype(BF16)
    hw = heads_per_step * LANES
    blk = pl.BlockSpec((1, T, hw), lambda b, h: (b, 0, h))
    subg = subln_g.reshape(1, LANES)
    return pl.pallas_call(
        functools.partial(_diff_attn_kernel, tq=tq),
        grid=(B, n_heads // heads_per_step),
        in_specs=[blk, blk, blk, blk,
                  pl.BlockSpec((T, LANES), lambda b, h: (0, 0)),
                  pl.BlockSpec((heads_per_step, 1, LANES), lambda b, h: (h, 0, 0)),
                  pl.BlockSpec((1, LANES), lambda b, h: (0, 0)),
                  pl.BlockSpec(lam_params.shape, lambda b, h: (0, 0))],
        out_specs=blk,
        out_shape=jax.ShapeDtypeStruct((B, T, W), BF16),
        scratch_shapes=[pltpu.VMEM((heads_per_step, LANES, T), BF16)],
        compiler_params=pltpu.CompilerParams(
            dimension_semantics=("arbitrary", "arbitrary"), vmem_limit_bytes=VMEM_LIMIT),
        name="diffattn",
    )(dq, dk, dv, gb, kpos, slopes, subg, lam_params)


def _out_proj_kernel(ya_ref, yb_ref, x_ref, w_ref, o_ref):
    wa = ya_ref.shape[2]
    acc = jnp.dot(ya_ref[0], w_ref[:wa, :].astype(BF16), preferred_element_type=F32)
    acc = acc + jnp.dot(yb_ref[0], w_ref[wa:, :].astype(BF16), preferred_element_type=F32)
    o_ref[0] = x_ref[0] + acc


def _out_proj(ya, yb, x, w_out, *, tm):
    B, T, D = x.shape
    row = lambda w: pl.BlockSpec((1, tm, w), lambda b, t: (b, t, 0))
    return pl.pallas_call(
        _out_proj_kernel,
        grid=(B, T // tm),
        in_specs=[row(ya.shape[2]), row(yb.shape[2]), row(D),
                  pl.BlockSpec(w_out.shape, lambda b, t: (0, 0))],
        out_specs=row(D),
        out_shape=jax.ShapeDtypeStruct((B, T, D), F32),
        compiler_params=pltpu.CompilerParams(
            dimension_semantics=("arbitrary", "arbitrary"), vmem_limit_bytes=VMEM_LIMIT),
        name="out_proj",
    )(ya, yb, x, w_out)


def kernel(x, norm_g, w_in, shift_mu, w0, w_decay_up, a0, w_iclr_up, k_k, k_a, r_k, ln_x_w, ln_x_b, q_norm_g, k_norm_g, lambda_q1, lambda_k1, lambda_q2, lambda_k2, subln_g, w_out):
    depth = norm_g.shape[0]
    assert depth == 1, "lambda_init is specialised to a single layer"
    h = x
    for l in range(depth):
        r, k, v, zwa, ga, dq, dk, dv, gb = _in_proj(
            h, norm_g[l], w_in[l], shift_mu[l], q_norm_g[l], k_norm_g[l], tm=1024)
        ya = _rwkv(r, k, v, zwa, ga, w0[l], w_decay_up[l], a0[l], w_iclr_up[l], k_k[l], k_a[l],
                   r_k[l], ln_x_w[l], ln_x_b[l], bb=8, tb=128)
        lam_params = jnp.stack([lambda_q1[l], lambda_k1[l], lambda_q2[l], lambda_k2[l]])
        yb = _diff_attn(dq, dk, dv, gb, subln_g[l], lam_params, tq=512, heads_per_step=4)
        h = _out_proj(ya, yb, h, w_out[l], tm=1024)
    return h
```

```python
import functools
import math

import jax
import jax.numpy as jnp
import numpy as np
from jax import lax
from jax.experimental import pallas as pl
from jax.experimental.pallas import tpu as pltpu

F32 = jnp.float32
BF16 = jnp.bfloat16

LANES = 128
SUBLANES = 8
MXU_WIDTH = 256
HEAD = 64
CHUNK = 64
RWKV_W = 512
DIFF_W = 512
LORA = 64
SHIFT_COLS = 3 * RWKV_W + 2 * LORA
NORM_EPS = 1e-6
QK_NORM_EPS = 1e-6
SUBLN_EPS = 1e-5
RWKV_GN_EPS = 64e-5
LAMBDA_INIT = 0.8 - 0.6 * math.exp(-0.3 * 0)
NEG_INF = -1e30
LOG2E = math.log2(math.e)
KEY_POS_SPLIT = 16
VMEM_LIMIT = 56 * 1024 * 1024


def _dot(a, b):
    return jnp.dot(a.astype(BF16), b.astype(BF16), preferred_element_type=F32)


def _dot_nt(a, b):
    return lax.dot_general(a.astype(BF16), b.astype(BF16), (((1,), (1,)), ((), ())),
                           preferred_element_type=F32)


def _sigmoid(x):
    return 0.5 * jnp.tanh(0.5 * x) + 0.5


def _in_proj_kernel(x_ref, g_ref, w_ref, mu_ref, qg_ref, kg_ref, ones_ref,
                    r_ref, k_ref, v_ref, zwa_ref, ga_ref, dq_ref, dk_ref, dv_ref, gb_ref,
                    hn_ref, carry_ref):
    t = pl.program_id(1)
    x = x_ref[0]
    tm = x.shape[0]
    ms = jnp.mean(x * x, axis=-1, keepdims=True)
    hn_ref[...] = (x * lax.rsqrt(ms + NORM_EPS) * g_ref[...]).astype(BF16)

    @pl.when(t == 0)
    def _():
        carry_ref[...] = jnp.zeros_like(carry_ref)

    row0 = lax.broadcasted_iota(jnp.int32, (tm, 1), 0) == 0

    def proj(c0, c1):
        return jnp.dot(hn_ref[...], w_ref[:, c0:c1].astype(BF16), preferred_element_type=F32)

    def shifted(c0, c1):
        p = proj(c0, c1)
        prev = jnp.where(row0, carry_ref[:, c0:c1], pltpu.roll(p, 1, 0))
        carry_ref[:, c0:c1] = p[tm - 1:tm, :]
        return p + (prev - p) * mu_ref[:, c0:c1]

    r_ref[0] = shifted(0, RWKV_W)
    k_ref[0] = shifted(RWKV_W, 2 * RWKV_W)
    v_ref[0] = shifted(2 * RWKV_W, 3 * RWKV_W)
    zwa_ref[0] = shifted(3 * RWKV_W, SHIFT_COLS)

    c = SHIFT_COLS
    g = proj(c, c + RWKV_W)
    ga_ref[0] = (g * _sigmoid(g)).astype(BF16)
    c += RWKV_W

    def qk_norm(p, gain):
        sq = (p * p).astype(BF16)
        w = ones_ref.shape[0]
        ms = jnp.concatenate(
            [jnp.dot(sq[:, c0:c0 + w], ones_ref[...], preferred_element_type=F32)
             for c0 in range(0, DIFF_W, w)], axis=1) * (1.0 / HEAD)
        return (p * lax.rsqrt(ms + QK_NORM_EPS) * gain).astype(BF16)

    dq_ref[0] = qk_norm(proj(c, c + DIFF_W), qg_ref[...] * (HEAD ** -0.5 * LOG2E))
    c += DIFF_W
    dk_ref[0] = qk_norm(proj(c, c + DIFF_W), kg_ref[...])
    c += DIFF_W
    dv_ref[0] = proj(c, c + DIFF_W).astype(BF16)
    c += DIFF_W
    g = proj(c, c + DIFF_W)
    gb_ref[0] = (g * _sigmoid(g)).astype(BF16)


def _in_proj(x, norm_g, w_in, shift_mu, q_norm_g, k_norm_g, *, tm):
    B, T, D = x.shape
    n_rep = DIFF_W // HEAD
    qg = jnp.tile(q_norm_g.reshape(1, HEAD), (1, n_rep))
    kg = jnp.tile(k_norm_g.reshape(1, HEAD), (1, n_rep))
    row = lambda w: pl.BlockSpec((1, tm, w), lambda b, t: (b, t, 0))
    full = lambda a: pl.BlockSpec(a.shape, lambda b, t: (0,) * a.ndim,
                                  pipeline_mode=pl.Buffered(1))
    f32o = lambda w: jax.ShapeDtypeStruct((B, T, w), F32)
    bf16o = lambda w: jax.ShapeDtypeStruct((B, T, w), BF16)
    g2 = norm_g.reshape(1, D)
    mu2 = shift_mu.reshape(1, SHIFT_COLS)
    group = np.arange(MXU_WIDTH) // HEAD
    ones_bd = jnp.asarray(group[:, None] == group[None, :], BF16)
    return pl.pallas_call(
        _in_proj_kernel,
        grid=(B, T // tm),
        in_specs=[row(D), full(g2), full(w_in), full(mu2), full(qg), full(kg), full(ones_bd)],
        out_specs=[row(RWKV_W), row(RWKV_W), row(RWKV_W), row(2 * LORA), row(RWKV_W),
                   row(DIFF_W), row(DIFF_W), row(DIFF_W), row(DIFF_W)],
        out_shape=[f32o(RWKV_W), f32o(RWKV_W), f32o(RWKV_W), f32o(2 * LORA), bf16o(RWKV_W),
                   bf16o(DIFF_W), bf16o(DIFF_W), bf16o(DIFF_W), bf16o(DIFF_W)],
        scratch_shapes=[pltpu.VMEM((tm, D), BF16), pltpu.VMEM((1, SHIFT_COLS), F32)],
        compiler_params=pltpu.CompilerParams(
            dimension_semantics=("arbitrary", "arbitrary"), vmem_limit_bytes=VMEM_LIMIT),
        name="in_proj",
    )(x, g2, w_in, mu2, qg, kg, ones_bd)


def _rwkv_kernel(r_ref, k_ref, v_ref, zwa_ref, ga_ref, wd_ref, wa_ref, vec_ref,
                 o_ref, s_ref, y_ref):
    bb, tb, width = r_ref.shape
    rows_total = bb * tb
    chunks_per_seq = tb // CHUNK
    n_chunks = bb * chunks_per_seq
    n_pairs = width // LANES
    P2 = 2 * CHUNK

    @pl.when(pl.program_id(1) == 0)
    def _():
        s_ref[...] = jnp.zeros_like(s_ref)

    w0 = vec_ref[0:1, :]
    a0 = vec_ref[1:2, :]
    k_k = vec_ref[2:3, :]
    k_a = vec_ref[3:4, :]
    r_k = vec_ref[4:5, :]
    ln_w = vec_ref[5:6, :]
    ln_b = vec_ref[6:7, :]

    ri = lax.broadcasted_iota(jnp.int32, (P2, LANES), 0)
    ci = lax.broadcasted_iota(jnp.int32, (P2, LANES), 1)
    same_head = (ri // HEAD) == (ci // HEAD)
    ones_bd = jnp.where(same_head, 1.0, 0.0).astype(BF16)
    ti = lax.broadcasted_iota(jnp.int32, (CHUNK, LANES), 0)
    si = lax.broadcasted_iota(jnp.int32, (CHUNK, LANES), 1) % HEAD
    strict = si < ti
    incl = si <= ti
    eye = si == ti
    lane_blocks = [slice(p * LANES, (p + 1) * LANES) for p in range(n_pairs)]

    def group_sum(z):
        zb = z.astype(BF16)
        return jnp.concatenate(
            [jnp.dot(zb[:, lb], ones_bd, preferred_element_type=F32) for lb in lane_blocks],
            axis=1)

    def bd(z):
        return jnp.where(same_head, jnp.concatenate([z, z], axis=0), 0.0)

    def head_transposed(z):
        zt = bd(z).T
        return zt[:CHUNK] + zt[CHUNK:]

    flat = lambda ref: ref[...].reshape(rows_total, ref.shape[2])
    r = flat(r_ref)
    k = flat(k_ref)
    v = flat(v_ref)
    zwa = flat(zwa_ref)
    u = w0 + _dot(jnp.tanh(zwa), wd_ref[...])
    lw = -math.exp(-0.5) * _sigmoid(u)
    a = _sigmoid(a0 + _dot(zwa, wa_ref[...]))
    kkr = k * k_k
    kk = kkr * lax.rsqrt(jnp.maximum(group_sum(kkr * kkr), 1e-24))
    kmod = k * (1.0 + (a - 1.0) * k_a)

    tri_r = lax.broadcasted_iota(jnp.int32, (CHUNK, CHUNK), 0)
    tri_c = lax.broadcasted_iota(jnp.int32, (CHUNK, CHUNK), 1)
    lower = jnp.where(tri_c <= tri_r, 1.0, 0.0).astype(BF16)
    lw_hi = lw.astype(BF16)
    lw_lo = (lw - lw_hi.astype(F32)).astype(BF16)
    cs = jnp.concatenate(
        [jnp.dot(lower, lw_hi[c * CHUNK:(c + 1) * CHUNK], preferred_element_type=F32)
         + jnp.dot(lower, lw_lo[c * CHUNK:(c + 1) * CHUNK], preferred_element_type=F32)
         for c in range(n_chunks)], axis=0)
    e_pos = jnp.exp(cs)
    e_neg = jnp.exp(-cs)
    al_full = -kk * jnp.exp(cs - lw)
    rt_full = r * e_pos
    bt_full = kk * a * e_neg
    kt_full = kmod * e_neg

    items = [(c, p) for c in range(n_chunks) for p in range(n_pairs)]
    blk = lambda z, c, p: z[c * CHUNK:(c + 1) * CHUNK, lane_blocks[p]]
    e_tot = {it: blk(e_pos, *it)[CHUNK - 1:CHUNK, :] for it in items}
    al = {it: blk(al_full, *it) for it in items}
    rt = {it: blk(rt_full, *it) for it in items}
    bt = {it: blk(bt_full, *it) for it in items}
    kt = {it: blk(kt_full, *it) for it in items}
    v_bd = {it: bd(blk(v, *it)) for it in items}

    gram = {it: _dot_nt(jnp.concatenate([al[it], rt[it]], axis=0),
                        jnp.concatenate([bd(bt[it]), bd(kt[it])], axis=0))
            for it in items}
    a_ab = {it: jnp.where(strict, gram[it][:CHUNK, :LANES], 0.0) for it in items}
    a_kr = {it: jnp.concatenate([jnp.where(strict, gram[it][:CHUNK, LANES:], 0.0),
                                 jnp.where(incl, gram[it][CHUNK:, LANES:], 0.0)], axis=0)
            for it in items}
    a_rb = {it: jnp.where(incl, gram[it][CHUNK:, :LANES], 0.0) for it in items}

    n_levels = int(math.log2(CHUNK))
    tm = {it: jnp.where(eye, 1.0, a_ab[it]) for it in items}
    ap = {it: _dot(a_ab[it], bd(a_ab[it])) for it in items}
    for _ in range(1, n_levels - 1):
        pp = {it: _dot(jnp.concatenate([ap[it], tm[it]], axis=0), bd(ap[it]))
              for it in items}
        ap = {it: pp[it][:CHUNK] for it in items}
        tm = {it: tm[it] + pp[it][CHUNK:] for it in items}
    tm = {it: tm[it] + _dot(tm[it], bd(ap[it])) for it in items}

    av = {it: _dot(a_kr[it], v_bd[it]) for it in items}
    wu = {it: _dot(tm[it], jnp.concatenate([bd(al[it]), bd(av[it][:CHUNK])], axis=1))
          for it in items}
    wu_bd = {it: jnp.concatenate([bd(wu[it][:, :LANES]), bd(wu[it][:, LANES:])], axis=1)
             for it in items}
    ab = {it: _dot(a_rb[it], wu_bd[it]) for it in items}
    bk_t = {it: jnp.concatenate([head_transposed(bt[it] * e_tot[it]),
                                 head_transposed(kt[it] * e_tot[it])], axis=1)
            for it in items}
    gh = {it: _dot(bk_t[it], jnp.concatenate(
        [wu_bd[it], jnp.concatenate([jnp.zeros_like(v_bd[it]), v_bd[it]], axis=1)], axis=0))
          for it in items}
    gq = {it: jnp.concatenate([jnp.where(eye, e_tot[it], 0.0) + gh[it][:, :LANES],
                               rt[it] + ab[it][:, :LANES]], axis=0) for it in items}
    hm = {it: gh[it][:, LANES:] for it in items}
    y1 = {it: ab[it][:, LANES:] + av[it][CHUNK:] for it in items}

    for b in range(bb):
        for p in range(n_pairs):
            s = s_ref[b, p]
            for c in range(b * chunks_per_seq, (b + 1) * chunks_per_seq):
                it = (c, p)
                ys = _dot(gq[it], s)
                y_ref[c * CHUNK:(c + 1) * CHUNK, lane_blocks[p]] = ys[CHUNK:] + y1[it]
                s = bd(ys[:CHUNK] + hm[it])
            s_ref[b, p] = s

    y = y_ref[...]
    mu = group_sum(y) * (1.0 / HEAD)
    d = y - mu
    var = group_sum(d * d) * (1.0 / HEAD)
    yn = d * lax.rsqrt(var + RWKV_GN_EPS) * ln_w + ln_b
    bonus = group_sum(r * kmod * r_k) * v
    out = (yn + bonus) * flat(ga_ref).astype(F32)
    o_ref[...] = out.reshape(bb, tb, width).astype(BF16)


def _rwkv(r, k, v, zwa, ga, w0, w_decay_up, a0, w_iclr_up, k_k, k_a, r_k, ln_w, ln_b, *, bb, tb):
    B, T, W = r.shape
    n_pairs = W // LANES
    zeros = jnp.zeros((LORA, W), F32)
    wd = jnp.concatenate([w_decay_up, zeros], axis=0)
    wa = jnp.concatenate([zeros, w_iclr_up], axis=0)
    vec = jnp.stack([w0, a0, k_k, k_a, r_k.reshape(W), ln_w, ln_b, jnp.zeros((W,), F32)])
    blk = pl.BlockSpec((bb, tb, W), lambda b, t: (b, t, 0))
    full = lambda a: pl.BlockSpec(a.shape, lambda b, t: (0,) * a.ndim)
    return pl.pallas_call(
        _rwkv_kernel,
        grid=(B // bb, T // tb),
        in_specs=[blk, blk, blk,
                  pl.BlockSpec((bb, tb, 2 * LORA), lambda b, t: (b, t, 0)),
                  blk, full(wd), full(wa), full(vec)],
        out_specs=blk,
        out_shape=jax.ShapeDtypeStruct((B, T, W), BF16),
        scratch_shapes=[pltpu.VMEM((bb, n_pairs, LANES, LANES), F32),
                        pltpu.VMEM((bb * tb, W), F32)],
        compiler_params=pltpu.CompilerParams(
            dimension_semantics=("arbitrary", "arbitrary"), vmem_limit_bytes=VMEM_LIMIT),
        name="rwkv7",
    )(r, k, v, zwa, ga, wd, wa, vec)


def _diff_attn_kernel(q_ref, k_ref, v_ref, gb_ref, kpos_ref, slope_ref, subg_ref, lam_ref,
                      o_ref, vt_ref, *, tq):
    seq = q_ref.shape[1]
    n_heads = q_ref.shape[2] // LANES
    n_tiles = seq // tq
    nq = 2 * tq
    head_lanes = [slice(h * LANES, (h + 1) * LANES) for h in range(n_heads)]
    for h, hl in enumerate(head_lanes):
        for r0 in range(0, seq, tq):
            vt_ref[h, :, r0:r0 + tq] = v_ref[0, r0:r0 + tq, hl].astype(F32).T.astype(BF16)

    lane = lax.broadcasted_iota(jnp.int32, (tq, LANES), 1)
    lane1 = lax.broadcasted_iota(jnp.int32, (1, LANES), 1)
    half = tq // 2
    tri = (lax.broadcasted_iota(jnp.int32, (half, tq), 0)
           <= lax.broadcasted_iota(jnp.int32, (half, tq), 1) % half)

    def alibi_lanes(h):
        slope = slope_ref[h][:, 0:1] * LOG2E
        slope_hi = slope.astype(BF16).astype(F32)
        slope_lo = slope - slope_hi
        q_bias = jnp.where(lane1 < 2, slope_hi, jnp.where(lane1 < 4, slope_lo, 0.0))
        return jnp.broadcast_to(q_bias, (nq, LANES)).astype(BF16)

    q_bias = [alibi_lanes(h) for h in range(n_heads)]

    lam_p = lam_ref[...]
    s1 = jnp.sum(lam_p[0:1] * lam_p[1:2], axis=-1, keepdims=True)
    s2 = jnp.sum(lam_p[2:3] * lam_p[3:4], axis=-1, keepdims=True)
    lam = jnp.exp(s1) - jnp.exp(s2) + LAMBDA_INIT

    def all_sublanes(z, op):
        shift = SUBLANES // 2
        while shift:
            z = op(z, pltpu.roll(z, shift, 0))
            shift //= 2
        return z

    for i in range(n_tiles):
        rows = slice(i * tq, (i + 1) * tq)
        klen = (i + 1) * tq
        kmain = klen - half
        for h, hl in enumerate(head_lanes):
            q = q_ref[0, rows, hl]
            zero = jnp.zeros_like(q)
            q0 = jnp.where(lane < HEAD, q, zero)
            q1 = jnp.where(lane >= HEAD, q, zero)
            qcat = jnp.concatenate([q0[:half], q1[:half], q0[half:], q1[half:]], axis=0)
            qcat = jnp.concatenate([qcat, q_bias[h]], axis=1)
            keys = jnp.concatenate([k_ref[0, :klen, hl], kpos_ref[:klen, :]], axis=1)
            st = lax.dot_general(keys[:kmain], qcat, (((1,), (1,)), ((), ())),
                                 preferred_element_type=F32)
            sx = lax.dot_general(keys[kmain:], qcat[tq:], (((1,), (1,)), ((), ())),
                                 preferred_element_type=F32)
            last = st[i * tq:]
            last = jnp.concatenate([jnp.where(tri, last[:, :tq], NEG_INF), last[:, tq:]], axis=1)
            st = last if i == 0 else jnp.concatenate([st[:i * tq], last], axis=0)
            sx = jnp.where(tri, sx, NEG_INF)
            st = st.reshape(kmain // SUBLANES, SUBLANES, nq)
            sx = sx.reshape(half // SUBLANES, SUBLANES, tq)
            mx = all_sublanes(jnp.max(st, axis=0), jnp.maximum)
            mx_late = jnp.maximum(mx[:, tq:], all_sublanes(jnp.max(sx, axis=0), jnp.maximum))
            mx = jnp.concatenate([mx[:, :tq], mx_late], axis=1)
            p = jnp.exp2(st - mx)
            px = jnp.exp2(sx - mx_late)
            ls = all_sublanes(jnp.sum(p, axis=0), jnp.add)
            ls = jnp.concatenate(
                [ls[:, :tq], ls[:, tq:] + all_sublanes(jnp.sum(px, axis=0), jnp.add)], axis=1)
            acc = jnp.dot(vt_ref[h, :, :kmain], p.reshape(kmain, nq).astype(BF16),
                          preferred_element_type=F32)
            acc_late = acc[:, tq:] + jnp.dot(vt_ref[h, :, kmain:klen],
                                             px.reshape(half, tq).astype(BF16),
                                             preferred_element_type=F32)
            acc = jnp.concatenate([acc[:, :tq], acc_late], axis=1)
            att = acc.reshape(LANES // SUBLANES, SUBLANES, nq) / ls
            ot = jnp.concatenate(
                [att[:, :, :half] - lam * att[:, :, half:tq],
                 att[:, :, tq:tq + half] - lam * att[:, :, tq + half:]], axis=2)
            ms = all_sublanes(jnp.sum(ot * ot, axis=0), jnp.add) * (1.0 / LANES)
            ot = (ot * lax.rsqrt(ms + SUBLN_EPS)).reshape(LANES, tq)
            o = ot.T * (subg_ref[...] * (1.0 - LAMBDA_INIT))
            o_ref[0, rows, hl] = (o * gb_ref[0, rows, hl].astype(F32)).astype(BF16)


def _diff_attn(dq, dk, dv, gb, subln_g, lam_params, *, tq, heads_per_step):
    B, T, W = dq.shape
    n_heads = W // LANES
    slopes = jnp.asarray([2.0 ** (-8.0 * (h + 1) / n_heads) for h in range(n_heads)], F32)
    slopes = jnp.broadcast_to(slopes[:, None, None], (n_heads, 1, LANES))
    pos = lax.broadcasted_iota(jnp.int32, (T, LANES), 0)
    col = lax.broadcasted_iota(jnp.int32, (T, LANES), 1)
    pos_lo = pos % KEY_POS_SPLIT
    kpos = jnp.where(col < 4, jnp.where(col % 2 == 0, pos - pos_lo, pos_lo), 0).astype(BF16)
    hw = heads_per_step * LANES
    blk = pl.BlockSpec((1, T, hw), lambda b, h: (b, 0, h))
    subg = subln_g.reshape(1, LANES)
    return pl.pallas_call(
        functools.partial(_diff_attn_kernel, tq=tq),
        grid=(B, n_heads // heads_per_step),
        in_specs=[blk, blk, blk, blk,
                  pl.BlockSpec((T, LANES), lambda b, h: (0, 0)),
                  pl.BlockSpec((heads_per_step, 1, LANES), lambda b, h: (h, 0, 0)),
                  pl.BlockSpec((1, LANES), lambda b, h: (0, 0)),
                  pl.BlockSpec(lam_params.shape, lambda b, h: (0, 0))],
        out_specs=blk,
        out_shape=jax.ShapeDtypeStruct((B, T, W), BF16),
        scratch_shapes=[pltpu.VMEM((heads_per_step, LANES, T), BF16)],
        compiler_params=pltpu.CompilerParams(
            dimension_semantics=("arbitrary", "arbitrary"), vmem_limit_bytes=VMEM_LIMIT),
        name="diffattn",
    )(dq, dk, dv, gb, kpos, slopes, subg, lam_params)


def _out_proj_kernel(ya_ref, yb_ref, x_ref, w_ref, o_ref):
    wa = ya_ref.shape[2]
    acc = jnp.dot(ya_ref[0], w_ref[:wa, :].astype(BF16), preferred_element_type=F32)
    acc = acc + jnp.dot(yb_ref[0], w_ref[wa:, :].astype(BF16), preferred_element_type=F32)
    o_ref[0] = x_ref[0] + acc


def _out_proj(ya, yb, x, w_out, *, tm):
    B, T, D = x.shape
    row = lambda w: pl.BlockSpec((1, tm, w), lambda b, t: (b, t, 0))
    return pl.pallas_call(
        _out_proj_kernel,
        grid=(B, T // tm),
        in_specs=[row(ya.shape[2]), row(yb.shape[2]), row(D),
                  pl.BlockSpec(w_out.shape, lambda b, t: (0, 0))],
        out_specs=row(D),
        out_shape=jax.ShapeDtypeStruct((B, T, D), F32),
        compiler_params=pltpu.CompilerParams(
            dimension_semantics=("arbitrary", "arbitrary"), vmem_limit_bytes=VMEM_LIMIT),
        name="out_proj",
    )(ya, yb, x, w_out)


def kernel(x, norm_g, w_in, shift_mu, w0, w_decay_up, a0, w_iclr_up, k_k, k_a, r_k, ln_x_w, ln_x_b, q_norm_g, k_norm_g, lambda_q1, lambda_k1, lambda_q2, lambda_k2, subln_g, w_out):
    depth = norm_g.shape[0]
    assert depth == 1, "lambda_init is specialised to a single layer"
    h = x
    for l in range(depth):
        r, k, v, zwa, ga, dq, dk, dv, gb = _in_proj(
            h, norm_g[l], w_in[l], shift_mu[l], q_norm_g[l], k_norm_g[l], tm=1024)
        ya = _rwkv(r, k, v, zwa, ga, w0[l], w_decay_up[l], a0[l], w_iclr_up[l], k_k[l], k_a[l],
                   r_k[l], ln_x_w[l], ln_x_b[l], bb=8, tb=128)
        lam_params = jnp.stack([lambda_q1[l], lambda_k1[l], lambda_q2[l], lambda_k2[l]])
        yb = _diff_attn(dq, dk, dv, gb, subln_g[l], lam_params, tq=512, heads_per_step=4)
        h = _out_proj(ya, yb, h, w_out[l], tm=1024)
    return h
```

```python
import functools
import math

import jax
import jax.numpy as jnp
import numpy as np
from jax import lax
from jax.experimental import pallas as pl
from jax.experimental.pallas import tpu as pltpu

F32 = jnp.float32
BF16 = jnp.bfloat16

LANES = 128
SUBLANES = 8
MXU_WIDTH = 256
HEAD = 64
CHUNK = 64
RWKV_W = 512
DIFF_W = 512
LORA = 64
SHIFT_COLS = 3 * RWKV_W + 2 * LORA
NORM_EPS = 1e-6
QK_NORM_EPS = 1e-6
SUBLN_EPS = 1e-5
RWKV_GN_EPS = 64e-5
LAMBDA_INIT = 0.8 - 0.6 * math.exp(-0.3 * 0)
NEG_INF = -1e30
LOG2E = math.log2(math.e)
KEY_POS_SPLIT = 16
VMEM_LIMIT = 56 * 1024 * 1024


def _dot(a, b):
    return jnp.dot(a.astype(BF16), b.astype(BF16), preferred_element_type=F32)


def _dot_nt(a, b):
    return lax.dot_general(a.astype(BF16), b.astype(BF16), (((1,), (1,)), ((), ())),
                           preferred_element_type=F32)


def _sigmoid(x):
    return 0.5 * jnp.tanh(0.5 * x) + 0.5


def _in_proj_kernel(x_ref, g_ref, w_ref, mu_ref, qg_ref, kg_ref, ones_ref,
                    r_ref, k_ref, v_ref, zwa_ref, ga_ref, dq_ref, dk_ref, dv_ref, gb_ref,
                    hn_ref, carry_ref):
    t = pl.program_id(1)
    x = x_ref[0]
    tm = x.shape[0]
    ms = jnp.mean(x * x, axis=-1, keepdims=True)
    hn_ref[...] = (x * lax.rsqrt(ms + NORM_EPS) * g_ref[...]).astype(BF16)

    @pl.when(t == 0)
    def _():
        carry_ref[...] = jnp.zeros_like(carry_ref)

    row0 = lax.broadcasted_iota(jnp.int32, (tm, 1), 0) == 0

    def proj(c0, c1):
        return jnp.dot(hn_ref[...], w_ref[:, c0:c1].astype(BF16), preferred_element_type=F32)

    def shifted(c0, c1):
        p = proj(c0, c1)
        prev = jnp.where(row0, carry_ref[:, c0:c1], pltpu.roll(p, 1, 0))
        carry_ref[:, c0:c1] = p[tm - 1:tm, :]
        return p + (prev - p) * mu_ref[:, c0:c1]

    r_ref[0] = shifted(0, RWKV_W)
    k_ref[0] = shifted(RWKV_W, 2 * RWKV_W)
    v_ref[0] = shifted(2 * RWKV_W, 3 * RWKV_W)
    zwa_ref[0] = shifted(3 * RWKV_W, SHIFT_COLS)

    c = SHIFT_COLS
    g = proj(c, c + RWKV_W)
    ga_ref[0] = (g * _sigmoid(g)).astype(BF16)
    c += RWKV_W

    def qk_norm(p, gain):
        sq = (p * p).astype(BF16)
        w = ones_ref.shape[0]
        ms = jnp.concatenate(
            [jnp.dot(sq[:, c0:c0 + w], ones_ref[...], preferred_element_type=F32)
             for c0 in range(0, DIFF_W, w)], axis=1) * (1.0 / HEAD)
        return (p * lax.rsqrt(ms + QK_NORM_EPS) * gain).astype(BF16)

    dq_ref[0] = qk_norm(proj(c, c + DIFF_W), qg_ref[...] * (HEAD ** -0.5 * LOG2E))
    c += DIFF_W
    dk_ref[0] = qk_norm(proj(c, c + DIFF_W), kg_ref[...])
    c += DIFF_W
    dv_ref[0] = proj(c, c + DIFF_W).astype(BF16)
    c += DIFF_W
    g = proj(c, c + DIFF_W)
    gb_ref[0] = (g * _sigmoid(g)).astype(BF16)


def _in_proj(x, norm_g, w_in, shift_mu, q_norm_g, k_norm_g, *, tm):
    B, T, D = x.shape
    n_rep = DIFF_W // HEAD
    qg = jnp.tile(q_norm_g.reshape(1, HEAD), (1, n_rep))
    kg = jnp.tile(k_norm_g.reshape(1, HEAD), (1, n_rep))
    row = lambda w: pl.BlockSpec((1, tm, w), lambda b, t: (b, t, 0))
    full = lambda a: pl.BlockSpec(a.shape, lambda b, t: (0,) * a.ndim,
                                  pipeline_mode=pl.Buffered(1))
    f32o = lambda w: jax.ShapeDtypeStruct((B, T, w), F32)
    bf16o = lambda w: jax.ShapeDtypeStruct((B, T, w), BF16)
    g2 = norm_g.reshape(1, D)
    mu2 = shift_mu.reshape(1, SHIFT_COLS)
    group = np.arange(MXU_WIDTH) // HEAD
    ones_bd = jnp.asarray(group[:, None] == group[None, :], BF16)
    return pl.pallas_call(
        _in_proj_kernel,
        grid=(B, T // tm),
        in_specs=[row(D), full(g2), full(w_in), full(mu2), full(qg), full(kg), full(ones_bd)],
        out_specs=[row(RWKV_W), row(RWKV_W), row(RWKV_W), row(2 * LORA), row(RWKV_W),
                   row(DIFF_W), row(DIFF_W), row(DIFF_W), row(DIFF_W)],
        out_shape=[f32o(RWKV_W), f32o(RWKV_W), f32o(RWKV_W), f32o(2 * LORA), bf16o(RWKV_W),
                   bf16o(DIFF_W), bf16o(DIFF_W), bf16o(DIFF_W), bf16o(DIFF_W)],
        scratch_shapes=[pltpu.VMEM((tm, D), BF16), pltpu.VMEM((1, SHIFT_COLS), F32)],
        compiler_params=pltpu.CompilerParams(
            dimension_semantics=("arbitrary", "arbitrary"), vmem_limit_bytes=VMEM_LIMIT),
        name="in_proj",
    )(x, g2, w_in, mu2, qg, kg, ones_bd)


def _rwkv_kernel(r_ref, k_ref, v_ref, zwa_ref, ga_ref, wd_ref, wa_ref, vec_ref,
                 o_ref, s_ref, y_ref):
    bb, tb, width = r_ref.shape
    rows_total = bb * tb
    chunks_per_seq = tb // CHUNK
    n_chunks = bb * chunks_per_seq
    n_pairs = width // LANES
    P2 = 2 * CHUNK

    @pl.when(pl.program_id(1) == 0)
    def _():
        s_ref[...] = jnp.zeros_like(s_ref)

    w0 = vec_ref[0:1, :]
    a0 = vec_ref[1:2, :]
    k_k = vec_ref[2:3, :]
    k_a = vec_ref[3:4, :]
    r_k = vec_ref[4:5, :]
    ln_w = vec_ref[5:6, :]
    ln_b = vec_ref[6:7, :]

    ri = lax.broadcasted_iota(jnp.int32, (P2, LANES), 0)
    ci = lax.broadcasted_iota(jnp.int32, (P2, LANES), 1)
    same_head = (ri // HEAD) == (ci // HEAD)
    ones_bd = jnp.where(same_head, 1.0, 0.0).astype(BF16)
    ti = lax.broadcasted_iota(jnp.int32, (CHUNK, LANES), 0)
    si = lax.broadcasted_iota(jnp.int32, (CHUNK, LANES), 1) % HEAD
    strict = si < ti
    incl = si <= ti
    eye = si == ti
    lane_blocks = [slice(p * LANES, (p + 1) * LANES) for p in range(n_pairs)]

    def group_sum(z):
        zb = z.astype(BF16)
        return jnp.concatenate(
            [jnp.dot(zb[:, lb], ones_bd, preferred_element_type=F32) for lb in lane_blocks],
            axis=1)

    def bd(z):
        return jnp.where(same_head, jnp.concatenate([z, z], axis=0), 0.0)

    def head_transposed(z):
        zt = bd(z).T
        return zt[:CHUNK] + zt[CHUNK:]

    flat = lambda ref: ref[...].reshape(rows_total, ref.shape[2])
    r = flat(r_ref)
    k = flat(k_ref)
    v = flat(v_ref)
    zwa = flat(zwa_ref)
    u = w0 + _dot(jnp.tanh(zwa), wd_ref[...])
    lw = -math.exp(-0.5) * _sigmoid(u)
    a = _sigmoid(a0 + _dot(zwa, wa_ref[...]))
    kkr = k * k_k
    kk = kkr * lax.rsqrt(jnp.maximum(group_sum(kkr * kkr), 1e-24))
    kmod = k * (1.0 + (a - 1.0) * k_a)

    tri_r = lax.broadcasted_iota(jnp.int32, (CHUNK, CHUNK), 0)
    tri_c = lax.broadcasted_iota(jnp.int32, (CHUNK, CHUNK), 1)
    lower = jnp.where(tri_c <= tri_r, 1.0, 0.0).astype(BF16)
    lw_hi = lw.astype(BF16)
    lw_lo = (lw - lw_hi.astype(F32)).astype(BF16)
    cs = jnp.concatenate(
        [jnp.dot(lower, lw_hi[c * CHUNK:(c + 1) * CHUNK], preferred_element_type=F32)
         + jnp.dot(lower, lw_lo[c * CHUNK:(c + 1) * CHUNK], preferred_element_type=F32)
         for c in range(n_chunks)], axis=0)
    e_pos = jnp.exp(cs)
    e_neg = jnp.exp(-cs)
    al_full = -kk * jnp.exp(cs - lw)
    rt_full = r * e_pos
    bt_full = kk * a * e_neg
    kt_full = kmod * e_neg

    items = [(c, p) for c in range(n_chunks) for p in range(n_pairs)]
    blk = lambda z, c, p: z[c * CHUNK:(c + 1) * CHUNK, lane_blocks[p]]
    e_tot = {it: blk(e_pos, *it)[CHUNK - 1:CHUNK, :] for it in items}
    al = {it: blk(al_full, *it) for it in items}
    rt = {it: blk(rt_full, *it) for it in items}
    bt = {it: blk(bt_full, *it) for it in items}
    kt = {it: blk(kt_full, *it) for it in items}
    v_bd = {it: bd(blk(v, *it)) for it in items}

    gram = {it: _dot_nt(jnp.concatenate([al[it], rt[it]], axis=0),
                        jnp.concatenate([bd(bt[it]), bd(kt[it])], axis=0))
            for it in items}
    a_ab = {it: jnp.where(strict, gram[it][:CHUNK, :LANES], 0.0) for it in items}
    a_kr = {it: jnp.concatenate([jnp.where(strict, gram[it][:CHUNK, LANES:], 0.0),
                                 jnp.where(incl, gram[it][CHUNK:, LANES:], 0.0)], axis=0)
            for it in items}
    a_rb = {it: jnp.where(incl, gram[it][CHUNK:, :LANES], 0.0) for it in items}

    n_levels = int(math.log2(CHUNK))
    tm = {it: jnp.where(eye, 1.0, a_ab[it]) for it in items}
    ap = {it: _dot(a_ab[it], bd(a_ab[it])) for it in items}
    for _ in range(1, n_levels - 1):
        pp = {it: _dot(jnp.concatenate([ap[it], tm[it]], axis=0), bd(ap[it]))
              for it in items}
        ap = {it: pp[it][:CHUNK] for it in items}
        tm = {it: tm[it] + pp[it][CHUNK:] for it in items}
    tm = {it: tm[it] + _dot(tm[it], bd(ap[it])) for it in items}

    av = {it: _dot(a_kr[it], v_bd[it]) for it in items}
    wu = {it: _dot(tm[it], jnp.concatenate([bd(al[it]), bd(av[it][:CHUNK])], axis=1))
          for it in items}
    wu_bd = {it: jnp.concatenate([bd(wu[it][:, :LANES]), bd(wu[it][:, LANES:])], axis=1)
             for it in items}
    ab = {it: _dot(a_rb[it], wu_bd[it]) for it in items}
    bk_t = {it: jnp.concatenate([head_transposed(bt[it] * e_tot[it]),
                                 head_transposed(kt[it] * e_tot[it])], axis=1)
            for it in items}
    gh = {it: _dot(bk_t[it], jnp.concatenate(
        [wu_bd[it], jnp.concatenate([jnp.zeros_like(v_bd[it]), v_bd[it]], axis=1)], axis=0))
          for it in items}
    gq = {it: jnp.concatenate([jnp.where(eye, e_tot[it], 0.0) + gh[it][:, :LANES],
                               rt[it] + ab[it][:, :LANES]], axis=0) for it in items}
    hm = {it: gh[it][:, LANES:] for it in items}
    y1 = {it: ab[it][:, LANES:] + av[it][CHUNK:] for it in items}

    for b in range(bb):
        for p in range(n_pairs):
            s = s_ref[b, p]
            for c in range(b * chunks_per_seq, (b + 1) * chunks_per_seq):
                it = (c, p)
                ys = _dot(gq[it], s)
                y_ref[c * CHUNK:(c + 1) * CHUNK, lane_blocks[p]] = ys[CHUNK:] + y1[it]
                s = bd(ys[:CHUNK] + hm[it])
            s_ref[b, p] = s

    y = y_ref[...]
    mu = group_sum(y) * (1.0 / HEAD)
    d = y - mu
    var = group_sum(d * d) * (1.0 / HEAD)
    yn = d * lax.rsqrt(var + RWKV_GN_EPS) * ln_w + ln_b
    bonus = group_sum(r * kmod * r_k) * v
    out = (yn + bonus) * flat(ga_ref).astype(F32)
    o_ref[...] = out.reshape(bb, tb, width).astype(BF16)


def _rwkv(r, k, v, zwa, ga, w0, w_decay_up, a0, w_iclr_up, k_k, k_a, r_k, ln_w, ln_b, *, bb, tb):
    B, T, W = r.shape
    n_pairs = W // LANES
    zeros = jnp.zeros((LORA, W), F32)
    wd = jnp.concatenate([w_decay_up, zeros], axis=0)
    wa = jnp.concatenate([zeros, w_iclr_up], axis=0)
    vec = jnp.stack([w0, a0, k_k, k_a, r_k.reshape(W), ln_w, ln_b, jnp.zeros((W,), F32)])
    blk = pl.BlockSpec((bb, tb, W), lambda b, t: (b, t, 0))
    full = lambda a: pl.BlockSpec(a.shape, lambda b, t: (0,) * a.ndim)
    return pl.pallas_call(
        _rwkv_kernel,
        grid=(B // bb, T // tb),
        in_specs=[blk, blk, blk,
                  pl.BlockSpec((bb, tb, 2 * LORA), lambda b, t: (b, t, 0)),
                  blk, full(wd), full(wa), full(vec)],
        out_specs=blk,
        out_shape=jax.ShapeDtypeStruct((B, T, W), BF16),
        scratch_shapes=[pltpu.VMEM((bb, n_pairs, LANES, LANES), F32),
                        pltpu.VMEM((bb * tb, W), F32)],
        compiler_params=pltpu.CompilerParams(
            dimension_semantics=("arbitrary", "arbitrary"), vmem_limit_bytes=VMEM_LIMIT),
        name="rwkv7",
    )(r, k, v, zwa, ga, wd, wa, vec)


def _diff_attn_kernel(q_ref, k_ref, v_ref, gb_ref, kpos_ref, slope_ref, subg_ref, lam_ref,
                      o_ref, vt_ref, *, tq):
    seq = q_ref.shape[1]
    n_heads = q_ref.shape[2] // LANES
    n_tiles = seq // tq
    nq = 2 * tq
    head_lanes = [slice(h * LANES, (h + 1) * LANES) for h in range(n_heads)]
    for h, hl in enumerate(head_lanes):
        for r0 in range(0, seq, tq):
            vt_ref[h, :, r0:r0 + tq] = v_ref[0, r0:r0 + tq, hl].astype(F32).T.astype(BF16)

    lane = lax.broadcasted_iota(jnp.int32, (tq, LANES), 1)
    lane1 = lax.broadcasted_iota(jnp.int32, (1, LANES), 1)
    half = tq // 2
    tri = (lax.broadcasted_iota(jnp.int32, (half, tq), 0)
           <= lax.broadcasted_iota(jnp.int32, (half, tq), 1) % half)

    def alibi_lanes(h):
        slope = slope_ref[h][:, 0:1] * LOG2E
        slope_hi = slope.astype(BF16).astype(F32)
        slope_lo = slope - slope_hi
        q_bias = jnp.where(lane1 < 2, slope_hi, jnp.where(lane1 < 4, slope_lo, 0.0))
        return jnp.broadcast_to(q_bias, (nq, LANES)).astype(BF16)

    q_bias = [alibi_lanes(h) for h in range(n_heads)]

    lam_p = lam_ref[...]
    s1 = jnp.sum(lam_p[0:1] * lam_p[1:2], axis=-1, keepdims=True)
    s2 = jnp.sum(lam_p[2:3] * lam_p[3:4], axis=-1, keepdims=True)
    lam = jnp.exp(s1) - jnp.exp(s2) + LAMBDA_INIT

    def all_sublanes(z, op):
        shift = SUBLANES // 2
        while shift:
            z = op(z, pltpu.roll(z, shift, 0))
            shift //= 2
        return z

    for i in range(n_tiles):
        rows = slice(i * tq, (i + 1) * tq)
        klen = (i + 1) * tq
        kmain = klen - half
        for h, hl in enumerate(head_lanes):
            q = q_ref[0, rows, hl]
            zero = jnp.zeros_like(q)
            q0 = jnp.where(lane < HEAD, q, zero)
            q1 = jnp.where(lane >= HEAD, q, zero)
            qcat = jnp.concatenate([q0[:half], q1[:half], q0[half:], q1[half:]], axis=0)
            qcat = jnp.concatenate([qcat, q_bias[h]], axis=1)
            keys = jnp.concatenate([k_ref[0, :klen, hl], kpos_ref[:klen, :]], axis=1)
            st = lax.dot_general(keys[:kmain], qcat, (((1,), (1,)), ((), ())),
                                 preferred_element_type=F32)
            sx = lax.dot_general(keys[kmain:], qcat[tq:], (((1,), (1,)), ((), ())),
                                 preferred_element_type=F32)
            last = st[i * tq:]
            last = jnp.concatenate([jnp.where(tri, last[:, :tq], NEG_INF), last[:, tq:]], axis=1)
            st = last if i == 0 else jnp.concatenate([st[:i * tq], last], axis=0)
            sx = jnp.where(tri, sx, NEG_INF)
            st = st.reshape(kmain // SUBLANES, SUBLANES, nq)
            sx = sx.reshape(half // SUBLANES, SUBLANES, tq)
            mx = all_sublanes(jnp.max(st, axis=0), jnp.maximum)
            mx_late = jnp.maximum(mx[:, tq:], all_sublanes(jnp.max(sx, axis=0), jnp.maximum))
            mx = jnp.concatenate([mx[:, :tq], mx_late], axis=1)
            p = jnp.exp2(st - mx)
            px = jnp.exp2(sx - mx_late)
            ls = all_sublanes(jnp.sum(p, axis=0), jnp.add)
            ls = jnp.concatenate(
                [ls[:, :tq], ls[:, tq:] + all_sublanes(jnp.sum(px, axis=0), jnp.add)], axis=1)
            acc = jnp.dot(vt_ref[h, :, :kmain], p.reshape(kmain, nq).astype(BF16),
                          preferred_element_type=F32)
            acc_late = acc[:, tq:] + jnp.dot(vt_ref[h, :, kmain:klen],
                                             px.reshape(half, tq).astype(BF16),
                                             preferred_element_type=F32)
            acc = jnp.concatenate([acc[:, :tq], acc_late], axis=1)
            att = acc.reshape(LANES // SUBLANES, SUBLANES, nq) / ls
            ot = jnp.concatenate(
                [att[:, :, :half] - lam * att[:, :, half:tq],
                 att[:, :, tq:tq + half] - lam * att[:, :, tq + half:]], axis=2)
            ms = all_sublanes(jnp.sum(ot * ot, axis=0), jnp.add) * (1.0 / LANES)
            ot = (ot * lax.rsqrt(ms + SUBLN_EPS)).reshape(LANES, tq)
            o = ot.T * (subg_ref[...] * (1.0 - LAMBDA_INIT))
            o_ref[0, rows, hl] = (o * gb_ref[0, rows, hl].astype(F32)).astype(BF16)


def _diff_attn(dq, dk, dv, gb, subln_g, lam_params, *, tq, heads_per_step):
    B, T, W = dq.shape
    n_heads = W // LANES
    slopes = jnp.asarray([2.0 ** (-8.0 * (h + 1) / n_heads) for h in range(n_heads)], F32)
    slopes = jnp.broadcast_to(slopes[:, None, None], (n_heads, 1, LANES))
    pos = jnp.arange(T, dtype=jnp.int32)
    pos_lo = pos % KEY_POS_SPLIT
    pos_hi = pos - pos_lo
    kpos = jnp.zeros((T, LANES), F32)
    kpos = kpos.at[:, 0].set(pos_hi).at[:, 1].set(pos_lo).at[:, 2].set(pos_hi).at[:, 3].set(pos_lo)
    kpos = kpos.astype(BF16)
    hw = heads_per_step * LANES
    blk = pl.BlockSpec((1, T, hw), lambda b, h: (b, 0, h))
    subg = subln_g.reshape(1, LANES)
    return pl.pallas_call(
        functools.partial(_diff_attn_kernel, tq=tq),
        grid=(B, n_heads // heads_per_step),
        in_specs=[blk, blk, blk, blk,
                  pl.BlockSpec((T, LANES), lambda b, h: (0, 0)),
                  pl.BlockSpec((heads_per_step, 1, LANES), lambda b, h: (h, 0, 0)),
                  pl.BlockSpec((1, LANES), lambda b, h: (0, 0)),
                  pl.BlockSpec(lam_params.shape, lambda b, h: (0, 0))],
        out_specs=blk,
        out_shape=jax.ShapeDtypeStruct((B, T, W), BF16),
        scratch_shapes=[pltpu.VMEM((heads_per_step, LANES, T), BF16)],
        compiler_params=pltpu.CompilerParams(
            dimension_semantics=("arbitrary", "arbitrary"), vmem_limit_bytes=VMEM_LIMIT),
        name="diffattn",
    )(dq, dk, dv, gb, kpos, slopes, subg, lam_params)


def _out_proj_kernel(ya_ref, yb_ref, x_ref, w_ref, o_ref):
    wa = ya_ref.shape[2]
    acc = jnp.dot(ya_ref[0], w_ref[:wa, :].astype(BF16), preferred_element_type=F32)
    acc = acc + jnp.dot(yb_ref[0], w_ref[wa:, :].astype(BF16), preferred_element_type=F32)
    o_ref[0] = x_ref[0] + acc


def _out_proj(ya, yb, x, w_out, *, tm):
    B, T, D = x.shape
    row = lambda w: pl.BlockSpec((1, tm, w), lambda b, t: (b, t, 0))
    return pl.pallas_call(
        _out_proj_kernel,
        grid=(B, T // tm),
        in_specs=[row(ya.shape[2]), row(yb.shape[2]), row(D),
                  pl.BlockSpec(w_out.shape, lambda b, t: (0, 0))],
        out_specs=row(D),
        out_shape=jax.ShapeDtypeStruct((B, T, D), F32),
        compiler_params=pltpu.CompilerParams(
            dimension_semantics=("arbitrary", "arbitrary"), vmem_limit_bytes=VMEM_LIMIT),
        name="out_proj",
    )(ya, yb, x, w_out)


def kernel(x, norm_g, w_in, shift_mu, w0, w_decay_up, a0, w_iclr_up, k_k, k_a, r_k, ln_x_w, ln_x_b, q_norm_g, k_norm_g, lambda_q1, lambda_k1, lambda_q2, lambda_k2, subln_g, w_out):
    depth = norm_g.shape[0]
    assert depth == 1, "lambda_init is specialised to a single layer"
    h = x
    for l in range(depth):
        r, k, v, zwa, ga, dq, dk, dv, gb = _in_proj(
            h, norm_g[l], w_in[l], shift_mu[l], q_norm_g[l], k_norm_g[l], tm=1024)
        ya = _rwkv(r, k, v, zwa, ga, w0[l], w_decay_up[l], a0[l], w_iclr_up[l], k_k[l], k_a[l],
                   r_k[l], ln_x_w[l], ln_x_b[l], bb=8, tb=128)
        lam_params = jnp.stack([lambda_q1[l], lambda_k1[l], lambda_q2[l], lambda_k2[l]])
        yb = _diff_attn(dq, dk, dv, gb, subln_g[l], lam_params, tq=512, heads_per_step=4)
        h = _out_proj(ya, yb, h, w_out[l], tm=1024)
    return h
```

```python
import functools
import math

import jax
import jax.numpy as jnp
import numpy as np
from jax import lax
from jax.experimental import pallas as pl
from jax.experimental.pallas import tpu as pltpu

F32 = jnp.float32
BF16 = jnp.bfloat16

LANES = 128
SUBLANES = 8
MXU_WIDTH = 256
HEAD = 64
CHUNK = 64
RWKV_W = 512
DIFF_W = 512
LORA = 64
SHIFT_COLS = 3 * RWKV_W + 2 * LORA
NORM_EPS = 1e-6
QK_NORM_EPS = 1e-6
SUBLN_EPS = 1e-5
RWKV_GN_EPS = 64e-5
LAMBDA_INIT = 0.8 - 0.6 * math.exp(-0.3 * 0)
NEG_INF = -1e30
LOG2E = math.log2(math.e)
KEY_POS_SPLIT = 16
VMEM_LIMIT = 56 * 1024 * 1024


def _dot(a, b):
    return jnp.dot(a.astype(BF16), b.astype(BF16), preferred_element_type=F32)


def _dot_nt(a, b):
    return lax.dot_general(a.astype(BF16), b.astype(BF16), (((1,), (1,)), ((), ())),
                           preferred_element_type=F32)


def _sigmoid(x):
    return 0.5 * jnp.tanh(0.5 * x) + 0.5


def _in_proj_kernel(x_ref, g_ref, w_ref, mu_ref, qg_ref, kg_ref, ones_ref,
                    r_ref, k_ref, v_ref, zwa_ref, ga_ref, dq_ref, dk_ref, dv_ref, gb_ref,
                    hn_ref, carry_ref):
    t = pl.program_id(1)
    x = x_ref[0]
    tm = x.shape[0]
    ms = jnp.mean(x * x, axis=-1, keepdims=True)
    hn_ref[...] = (x * lax.rsqrt(ms + NORM_EPS) * g_ref[...]).astype(BF16)

    @pl.when(t == 0)
    def _():
        carry_ref[...] = jnp.zeros_like(carry_ref)

    row0 = lax.broadcasted_iota(jnp.int32, (tm, 1), 0) == 0

    def proj(c0, c1):
        return jnp.dot(hn_ref[...], w_ref[:, c0:c1].astype(BF16), preferred_element_type=F32)

    def shifted(c0, c1):
        p = proj(c0, c1)
        prev = jnp.where(row0, carry_ref[:, c0:c1], pltpu.roll(p, 1, 0))
        carry_ref[:, c0:c1] = p[tm - 1:tm, :]
        return p + (prev - p) * mu_ref[:, c0:c1]

    r_ref[0] = shifted(0, RWKV_W)
    k_ref[0] = shifted(RWKV_W, 2 * RWKV_W)
    v_ref[0] = shifted(2 * RWKV_W, 3 * RWKV_W)
    zwa_ref[0] = shifted(3 * RWKV_W, SHIFT_COLS)

    c = SHIFT_COLS
    g = proj(c, c + RWKV_W)
    ga_ref[0] = (g * _sigmoid(g)).astype(BF16)
    c += RWKV_W

    def qk_norm(p, gain):
        sq = (p * p).astype(BF16)
        w = ones_ref.shape[0]
        ms = jnp.concatenate(
            [jnp.dot(sq[:, c0:c0 + w], ones_ref[...], preferred_element_type=F32)
             for c0 in range(0, DIFF_W, w)], axis=1) * (1.0 / HEAD)
        return (p * lax.rsqrt(ms + QK_NORM_EPS) * gain).astype(BF16)

    dq_ref[0] = qk_norm(proj(c, c + DIFF_W), qg_ref[...] * (HEAD ** -0.5 * LOG2E))
    c += DIFF_W
    dk_ref[0] = qk_norm(proj(c, c + DIFF_W), kg_ref[...])
    c += DIFF_W
    dv_ref[0] = proj(c, c + DIFF_W).astype(BF16)
    c += DIFF_W
    g = proj(c, c + DIFF_W)
    gb_ref[0] = (g * _sigmoid(g)).astype(BF16)


def _in_proj(x, norm_g, w_in, shift_mu, q_norm_g, k_norm_g, *, tm):
    B, T, D = x.shape
    n_rep = DIFF_W // HEAD
    qg = jnp.tile(q_norm_g.reshape(1, HEAD), (1, n_rep))
    kg = jnp.tile(k_norm_g.reshape(1, HEAD), (1, n_rep))
    row = lambda w: pl.BlockSpec((1, tm, w), lambda b, t: (b, t, 0))
    full = lambda a: pl.BlockSpec(a.shape, lambda b, t: (0,) * a.ndim,
                                  pipeline_mode=pl.Buffered(1))
    f32o = lambda w: jax.ShapeDtypeStruct((B, T, w), F32)
    bf16o = lambda w: jax.ShapeDtypeStruct((B, T, w), BF16)
    g2 = norm_g.reshape(1, D)
    mu2 = shift_mu.reshape(1, SHIFT_COLS)
    group = np.arange(MXU_WIDTH) // HEAD
    ones_bd = jnp.asarray(group[:, None] == group[None, :], BF16)
    return pl.pallas_call(
        _in_proj_kernel,
        grid=(B, T // tm),
        in_specs=[row(D), full(g2), full(w_in), full(mu2), full(qg), full(kg), full(ones_bd)],
        out_specs=[row(RWKV_W), row(RWKV_W), row(RWKV_W), row(2 * LORA), row(RWKV_W),
                   row(DIFF_W), row(DIFF_W), row(DIFF_W), row(DIFF_W)],
        out_shape=[f32o(RWKV_W), f32o(RWKV_W), f32o(RWKV_W), f32o(2 * LORA), bf16o(RWKV_W),
                   bf16o(DIFF_W), bf16o(DIFF_W), bf16o(DIFF_W), bf16o(DIFF_W)],
        scratch_shapes=[pltpu.VMEM((tm, D), BF16), pltpu.VMEM((1, SHIFT_COLS), F32)],
        compiler_params=pltpu.CompilerParams(
            dimension_semantics=("arbitrary", "arbitrary"), vmem_limit_bytes=VMEM_LIMIT),
        name="in_proj",
    )(x, g2, w_in, mu2, qg, kg, ones_bd)


def _rwkv_kernel(r_ref, k_ref, v_ref, zwa_ref, ga_ref, wd_ref, wa_ref, vec_ref,
                 o_ref, s_ref, y_ref):
    bb, tb, width = r_ref.shape
    rows_total = bb * tb
    chunks_per_seq = tb // CHUNK
    n_chunks = bb * chunks_per_seq
    n_pairs = width // LANES
    P2 = 2 * CHUNK

    @pl.when(pl.program_id(1) == 0)
    def _():
        s_ref[...] = jnp.zeros_like(s_ref)

    w0 = vec_ref[0:1, :]
    a0 = vec_ref[1:2, :]
    k_k = vec_ref[2:3, :]
    k_a = vec_ref[3:4, :]
    r_k = vec_ref[4:5, :]
    ln_w = vec_ref[5:6, :]
    ln_b = vec_ref[6:7, :]

    ri = lax.broadcasted_iota(jnp.int32, (P2, LANES), 0)
    ci = lax.broadcasted_iota(jnp.int32, (P2, LANES), 1)
    same_head = (ri // HEAD) == (ci // HEAD)
    ones_bd = jnp.where(same_head, 1.0, 0.0).astype(BF16)
    ti = lax.broadcasted_iota(jnp.int32, (CHUNK, LANES), 0)
    si = lax.broadcasted_iota(jnp.int32, (CHUNK, LANES), 1) % HEAD
    strict = si < ti
    incl = si <= ti
    eye = si == ti
    lane_blocks = [slice(p * LANES, (p + 1) * LANES) for p in range(n_pairs)]

    def group_sum(z):
        zb = z.astype(BF16)
        return jnp.concatenate(
            [jnp.dot(zb[:, lb], ones_bd, preferred_element_type=F32) for lb in lane_blocks],
            axis=1)

    def bd(z):
        return jnp.where(same_head, jnp.concatenate([z, z], axis=0), 0.0)

    def head_transposed(z):
        zt = bd(z).T
        return zt[:CHUNK] + zt[CHUNK:]

    flat = lambda ref: ref[...].reshape(rows_total, ref.shape[2])
    r = flat(r_ref)
    k = flat(k_ref)
    v = flat(v_ref)
    zwa = flat(zwa_ref)
    u = w0 + _dot(jnp.tanh(zwa), wd_ref[...])
    lw = -math.exp(-0.5) * _sigmoid(u)
    a = _sigmoid(a0 + _dot(zwa, wa_ref[...]))
    kkr = k * k_k
    kk = kkr * lax.rsqrt(jnp.maximum(group_sum(kkr * kkr), 1e-24))
    kmod = k * (1.0 + (a - 1.0) * k_a)

    tri_r = lax.broadcasted_iota(jnp.int32, (CHUNK, CHUNK), 0)
    tri_c = lax.broadcasted_iota(jnp.int32, (CHUNK, CHUNK), 1)
    lower = jnp.where(tri_c <= tri_r, 1.0, 0.0).astype(BF16)
    lw_hi = lw.astype(BF16)
    lw_lo = (lw - lw_hi.astype(F32)).astype(BF16)
    cs = jnp.concatenate(
        [jnp.dot(lower, lw_hi[c * CHUNK:(c + 1) * CHUNK], preferred_element_type=F32)
         + jnp.dot(lower, lw_lo[c * CHUNK:(c + 1) * CHUNK], preferred_element_type=F32)
         for c in range(n_chunks)], axis=0)
    e_pos = jnp.exp(cs)
    e_neg = jnp.exp(-cs)
    al_full = -kk * jnp.exp(cs - lw)
    rt_full = r * e_pos
    bt_full = kk * a * e_neg
    kt_full = kmod * e_neg

    items = [(c, p) for c in range(n_chunks) for p in range(n_pairs)]
    blk = lambda z, c, p: z[c * CHUNK:(c + 1) * CHUNK, lane_blocks[p]]
    e_tot = {it: blk(e_pos, *it)[CHUNK - 1:CHUNK, :] for it in items}
    al = {it: blk(al_full, *it) for it in items}
    rt = {it: blk(rt_full, *it) for it in items}
    bt = {it: blk(bt_full, *it) for it in items}
    kt = {it: blk(kt_full, *it) for it in items}
    v_bd = {it: bd(blk(v, *it)) for it in items}

    gram = {it: _dot_nt(jnp.concatenate([al[it], rt[it]], axis=0),
                        jnp.concatenate([bd(bt[it]), bd(kt[it])], axis=0))
            for it in items}
    a_ab = {it: jnp.where(strict, gram[it][:CHUNK, :LANES], 0.0) for it in items}
    a_kr = {it: jnp.concatenate([jnp.where(strict, gram[it][:CHUNK, LANES:], 0.0),
                                 jnp.where(incl, gram[it][CHUNK:, LANES:], 0.0)], axis=0)
            for it in items}
    a_rb = {it: jnp.where(incl, gram[it][CHUNK:, :LANES], 0.0) for it in items}

    n_levels = int(math.log2(CHUNK))
    tm = {it: jnp.where(eye, 1.0, a_ab[it]) for it in items}
    ap = {it: _dot(a_ab[it], bd(a_ab[it])) for it in items}
    for _ in range(1, n_levels - 1):
        pp = {it: _dot(jnp.concatenate([ap[it], tm[it]], axis=0), bd(ap[it]))
              for it in items}
        ap = {it: pp[it][:CHUNK] for it in items}
        tm = {it: tm[it] + pp[it][CHUNK:] for it in items}
    tm = {it: tm[it] + _dot(tm[it], bd(ap[it])) for it in items}

    av = {it: _dot(a_kr[it], v_bd[it]) for it in items}
    wu = {it: _dot(tm[it], jnp.concatenate([bd(al[it]), bd(av[it][:CHUNK])], axis=1))
          for it in items}
    wu_bd = {it: jnp.concatenate([bd(wu[it][:, :LANES]), bd(wu[it][:, LANES:])], axis=1)
             for it in items}
    ab = {it: _dot(a_rb[it], wu_bd[it]) for it in items}
    bk_t = {it: jnp.concatenate([head_transposed(bt[it] * e_tot[it]),
                                 head_transposed(kt[it] * e_tot[it])], axis=1)
            for it in items}
    gh = {it: _dot(bk_t[it], jnp.concatenate(
        [wu_bd[it], jnp.concatenate([jnp.zeros_like(v_bd[it]), v_bd[it]], axis=1)], axis=0))
          for it in items}
    gq = {it: jnp.concatenate([jnp.where(eye, e_tot[it], 0.0) + gh[it][:, :LANES],
                               rt[it] + ab[it][:, :LANES]], axis=0) for it in items}
    hm = {it: gh[it][:, LANES:] for it in items}
    y1 = {it: ab[it][:, LANES:] + av[it][CHUNK:] for it in items}

    for b in range(bb):
        for p in range(n_pairs):
            s = s_ref[b, p]
            for c in range(b * chunks_per_seq, (b + 1) * chunks_per_seq):
                it = (c, p)
                ys = _dot(gq[it], s)
                y_ref[c * CHUNK:(c + 1) * CHUNK, lane_blocks[p]] = ys[CHUNK:] + y1[it]
                s = bd(ys[:CHUNK] + hm[it])
            s_ref[b, p] = s

    y = y_ref[...]
    mu = group_sum(y) * (1.0 / HEAD)
    d = y - mu
    var = group_sum(d * d) * (1.0 / HEAD)
    yn = d * lax.rsqrt(var + RWKV_GN_EPS) * ln_w + ln_b
    bonus = group_sum(r * kmod * r_k) * v
    out = (yn + bonus) * flat(ga_ref).astype(F32)
    o_ref[...] = out.reshape(bb, tb, width).astype(BF16)


def _rwkv(r, k, v, zwa, ga, w0, w_decay_up, a0, w_iclr_up, k_k, k_a, r_k, ln_w, ln_b, *, bb, tb):
    B, T, W = r.shape
    n_pairs = W // LANES
    zeros = jnp.zeros((LORA, W), F32)
    wd = jnp.concatenate([w_decay_up, zeros], axis=0)
    wa = jnp.concatenate([zeros, w_iclr_up], axis=0)
    vec = jnp.stack([w0, a0, k_k, k_a, r_k.reshape(W), ln_w, ln_b, jnp.zeros((W,), F32)])
    blk = pl.BlockSpec((bb, tb, W), lambda b, t: (b, t, 0))
    full = lambda a: pl.BlockSpec(a.shape, lambda b, t: (0,) * a.ndim)
    return pl.pallas_call(
        _rwkv_kernel,
        grid=(B // bb, T // tb),
        in_specs=[blk, blk, blk,
                  pl.BlockSpec((bb, tb, 2 * LORA), lambda b, t: (b, t, 0)),
                  blk, full(wd), full(wa), full(vec)],
        out_specs=blk,
        out_shape=jax.ShapeDtypeStruct((B, T, W), BF16),
        scratch_shapes=[pltpu.VMEM((bb, n_pairs, LANES, LANES), F32),
                        pltpu.VMEM((bb * tb, W), F32)],
        compiler_params=pltpu.CompilerParams(
            dimension_semantics=("arbitrary", "arbitrary"), vmem_limit_bytes=VMEM_LIMIT),
        name="rwkv7",
    )(r, k, v, zwa, ga, wd, wa, vec)


def _diff_attn_kernel(q_ref, k_ref, v_ref, gb_ref, kpos_ref, slope_ref, subg_ref, lam_ref,
                      o_ref, vt_ref, keys_ref, *, tq):
    seq = q_ref.shape[1]
    n_heads = q_ref.shape[2] // LANES
    n_tiles = seq // tq
    nq = 2 * tq
    head_lanes = [slice(h * LANES, (h + 1) * LANES) for h in range(n_heads)]
    for h, hl in enumerate(head_lanes):
        for r0 in range(0, seq, tq):
            vt_ref[h, :, r0:r0 + tq] = v_ref[0, r0:r0 + tq, hl].astype(F32).T.astype(BF16)
        keys_ref[h, :, :LANES] = k_ref[0, :, hl]
        keys_ref[h, :, LANES:] = kpos_ref[...]

    lane = lax.broadcasted_iota(jnp.int32, (tq, LANES), 1)
    lane1 = lax.broadcasted_iota(jnp.int32, (1, LANES), 1)
    half = tq // 2
    tri = (lax.broadcasted_iota(jnp.int32, (half, tq), 0)
           <= lax.broadcasted_iota(jnp.int32, (half, tq), 1) % half)

    def alibi_lanes(h):
        slope = slope_ref[h][:, 0:1] * LOG2E
        slope_hi = slope.astype(BF16).astype(F32)
        slope_lo = slope - slope_hi
        q_bias = jnp.where(lane1 < 2, slope_hi, jnp.where(lane1 < 4, slope_lo, 0.0))
        return jnp.broadcast_to(q_bias, (nq, LANES)).astype(BF16)

    q_bias = [alibi_lanes(h) for h in range(n_heads)]

    lam_p = lam_ref[...]
    s1 = jnp.sum(lam_p[0:1] * lam_p[1:2], axis=-1, keepdims=True)
    s2 = jnp.sum(lam_p[2:3] * lam_p[3:4], axis=-1, keepdims=True)
    lam = jnp.exp(s1) - jnp.exp(s2) + LAMBDA_INIT

    def all_sublanes(z, op):
        shift = SUBLANES // 2
        while shift:
            z = op(z, pltpu.roll(z, shift, 0))
            shift //= 2
        return z

    for i in range(n_tiles):
        rows = slice(i * tq, (i + 1) * tq)
        klen = (i + 1) * tq
        kmain = klen - half
        for h, hl in enumerate(head_lanes):
            q = q_ref[0, rows, hl]
            zero = jnp.zeros_like(q)
            q0 = jnp.where(lane < HEAD, q, zero)
            q1 = jnp.where(lane >= HEAD, q, zero)
            qcat = jnp.concatenate([q0[:half], q1[:half], q0[half:], q1[half:]], axis=0)
            qcat = jnp.concatenate([qcat, q_bias[h]], axis=1)
            keys = keys_ref[h, :klen, :]
            st = lax.dot_general(keys[:kmain], qcat, (((1,), (1,)), ((), ())),
                                 preferred_element_type=F32)
            sx = lax.dot_general(keys[kmain:], qcat[tq:], (((1,), (1,)), ((), ())),
                                 preferred_element_type=F32)
            last = st[i * tq:]
            last = jnp.concatenate([jnp.where(tri, last[:, :tq], NEG_INF), last[:, tq:]], axis=1)
            st = last if i == 0 else jnp.concatenate([st[:i * tq], last], axis=0)
            sx = jnp.where(tri, sx, NEG_INF)
            st = st.reshape(kmain // SUBLANES, SUBLANES, nq)
            sx = sx.reshape(half // SUBLANES, SUBLANES, tq)
            mx = all_sublanes(jnp.max(st, axis=0), jnp.maximum)
            mx_late = jnp.maximum(mx[:, tq:], all_sublanes(jnp.max(sx, axis=0), jnp.maximum))
            mx = jnp.concatenate([mx[:, :tq], mx_late], axis=1)
            p = jnp.exp2(st - mx)
            px = jnp.exp2(sx - mx_late)
            ls = all_sublanes(jnp.sum(p, axis=0), jnp.add)
            ls = jnp.concatenate(
                [ls[:, :tq], ls[:, tq:] + all_sublanes(jnp.sum(px, axis=0), jnp.add)], axis=1)
            acc = jnp.dot(vt_ref[h, :, :kmain], p.reshape(kmain, nq).astype(BF16),
                          preferred_element_type=F32)
            acc_late = acc[:, tq:] + jnp.dot(vt_ref[h, :, kmain:klen],
                                             px.reshape(half, tq).astype(BF16),
                                             preferred_element_type=F32)
            acc = jnp.concatenate([acc[:, :tq], acc_late], axis=1)
            att = acc.reshape(LANES // SUBLANES, SUBLANES, nq) / ls
            ot = jnp.concatenate(
                [att[:, :, :half] - lam * att[:, :, half:tq],
                 att[:, :, tq:tq + half] - lam * att[:, :, tq + half:]], axis=2)
            ms = all_sublanes(jnp.sum(ot * ot, axis=0), jnp.add) * (1.0 / LANES)
            ot = (ot * lax.rsqrt(ms + SUBLN_EPS)).reshape(LANES, tq)
            o = ot.T * (subg_ref[...] * (1.0 - LAMBDA_INIT))
            o_ref[0, rows, hl] = (o * gb_ref[0, rows, hl].astype(F32)).astype(BF16)


def _diff_attn(dq, dk, dv, gb, subln_g, lam_params, *, tq, heads_per_step):
    B, T, W = dq.shape
    n_heads = W // LANES
    slopes = jnp.asarray([2.0 ** (-8.0 * (h + 1) / n_heads) for h in range(n_heads)], F32)
    slopes = jnp.broadcast_to(slopes[:, None, None], (n_heads, 1, LANES))
    pos = np.arange(T)
    pos_lo = pos % KEY_POS_SPLIT
    pos_hi = pos - pos_lo
    kpos = np.zeros((T, LANES), np.float32)
    kpos[:, 0], kpos[:, 1], kpos[:, 2], kpos[:, 3] = pos_hi, pos_lo, pos_hi, pos_lo
    kpos = jnp.asarray(kpos, BF16)
    hw = heads_per_step * LANES
    blk = pl.BlockSpec((1, T, hw), lambda b, h: (b, 0, h))
    subg = subln_g.reshape(1, LANES)
    return pl.pallas_call(
        functools.partial(_diff_attn_kernel, tq=tq),
        grid=(B, n_heads // heads_per_step),
        in_specs=[blk, blk, blk, blk,
                  pl.BlockSpec((T, LANES), lambda b, h: (0, 0)),
                  pl.BlockSpec((heads_per_step, 1, LANES), lambda b, h: (h, 0, 0)),
                  pl.BlockSpec((1, LANES), lambda b, h: (0, 0)),
                  pl.BlockSpec(lam_params.shape, lambda b, h: (0, 0))],
        out_specs=blk,
        out_shape=jax.ShapeDtypeStruct((B, T, W), BF16),
        scratch_shapes=[pltpu.VMEM((heads_per_step, LANES, T), BF16),
                        pltpu.VMEM((heads_per_step, T, 2 * LANES), BF16)],
        compiler_params=pltpu.CompilerParams(
            dimension_semantics=("arbitrary", "arbitrary"), vmem_limit_bytes=VMEM_LIMIT),
        name="diffattn",
    )(dq, dk, dv, gb, kpos, slopes, subg, lam_params)


def _out_proj_kernel(ya_ref, yb_ref, x_ref, w_ref, o_ref):
    wa = ya_ref.shape[2]
    acc = jnp.dot(ya_ref[0], w_ref[:wa, :].astype(BF16), preferred_element_type=F32)
    acc = acc + jnp.dot(yb_ref[0], w_ref[wa:, :].astype(BF16), preferred_element_type=F32)
    o_ref[0] = x_ref[0] + acc


def _out_proj(ya, yb, x, w_out, *, tm):
    B, T, D = x.shape
    row = lambda w: pl.BlockSpec((1, tm, w), lambda b, t: (b, t, 0))
    return pl.pallas_call(
        _out_proj_kernel,
        grid=(B, T // tm),
        in_specs=[row(ya.shape[2]), row(yb.shape[2]), row(D),
                  pl.BlockSpec(w_out.shape, lambda b, t: (0, 0))],
        out_specs=row(D),
        out_shape=jax.ShapeDtypeStruct((B, T, D), F32),
        compiler_params=pltpu.CompilerParams(
            dimension_semantics=("arbitrary", "arbitrary"), vmem_limit_bytes=VMEM_LIMIT),
        name="out_proj",
    )(ya, yb, x, w_out)


def kernel(x, norm_g, w_in, shift_mu, w0, w_decay_up, a0, w_iclr_up, k_k, k_a, r_k, ln_x_w, ln_x_b, q_norm_g, k_norm_g, lambda_q1, lambda_k1, lambda_q2, lambda_k2, subln_g, w_out):
    depth = norm_g.shape[0]
    assert depth == 1, "lambda_init is specialised to a single layer"
    h = x
    for l in range(depth):
        r, k, v, zwa, ga, dq, dk, dv, gb = _in_proj(
            h, norm_g[l], w_in[l], shift_mu[l], q_norm_g[l], k_norm_g[l], tm=1024)
        ya = _rwkv(r, k, v, zwa, ga, w0[l], w_decay_up[l], a0[l], w_iclr_up[l], k_k[l], k_a[l],
                   r_k[l], ln_x_w[l], ln_x_b[l], bb=8, tb=128)
        lam_params = jnp.stack([lambda_q1[l], lambda_k1[l], lambda_q2[l], lambda_k2[l]])
        yb = _diff_attn(dq, dk, dv, gb, subln_g[l], lam_params, tq=512, heads_per_step=4)
        h = _out_proj(ya, yb, h, w_out[l], tm=1024)
    return h
```

```python
import functools
import math

import jax
import jax.numpy as jnp
import numpy as np
from jax import lax
from jax.experimental import pallas as pl
from jax.experimental.pallas import tpu as pltpu

F32 = jnp.float32
BF16 = jnp.bfloat16

LANES = 128
SUBLANES = 8
MXU_WIDTH = 256
HEAD = 64
CHUNK = 64
RWKV_W = 512
DIFF_W = 512
LORA = 64
SHIFT_COLS = 3 * RWKV_W + 2 * LORA
NORM_EPS = 1e-6
QK_NORM_EPS = 1e-6
SUBLN_EPS = 1e-5
RWKV_GN_EPS = 64e-5
LAMBDA_INIT = 0.8 - 0.6 * math.exp(-0.3 * 0)
NEG_INF = -1e30
LOG2E = math.log2(math.e)
KEY_POS_SPLIT = 16
VMEM_LIMIT = 56 * 1024 * 1024

PROJ_ROWS = 1024
RWKV_SEQS, RWKV_ROWS = 8, 128
ATTN_QUERIES = 512
ATTN_HEADS = 4


def _dot(a, b):
    return jnp.dot(a.astype(BF16), b.astype(BF16), preferred_element_type=F32)


def _dot_nt(a, b):
    return lax.dot_general(a.astype(BF16), b.astype(BF16), (((1,), (1,)), ((), ())),
                           preferred_element_type=F32)


def _sigmoid(x):
    return 0.5 * jnp.tanh(0.5 * x) + 0.5


def _in_proj_kernel(x_ref, g_ref, w_ref, mu_ref, qg_ref, kg_ref, ones_ref,
                    r_ref, k_ref, v_ref, zwa_ref, ga_ref, dq_ref, dk_ref, dv_ref, gb_ref,
                    hn_ref, carry_ref):
    t = pl.program_id(1)
    x = x_ref[0]
    tm = x.shape[0]
    ms = jnp.mean(x * x, axis=-1, keepdims=True)
    hn_ref[...] = (x * lax.rsqrt(ms + NORM_EPS) * g_ref[...]).astype(BF16)

    @pl.when(t == 0)
    def _():
        carry_ref[...] = jnp.zeros_like(carry_ref)

    row0 = lax.broadcasted_iota(jnp.int32, (tm, 1), 0) == 0

    def proj(c0, c1):
        return jnp.dot(hn_ref[...], w_ref[:, c0:c1].astype(BF16), preferred_element_type=F32)

    def shifted(c0, c1):
        p = proj(c0, c1)
        prev = jnp.where(row0, carry_ref[:, c0:c1], pltpu.roll(p, 1, 0))
        carry_ref[:, c0:c1] = p[tm - 1:tm, :]
        return p + (prev - p) * mu_ref[:, c0:c1]

    r_ref[0] = shifted(0, RWKV_W)
    k_ref[0] = shifted(RWKV_W, 2 * RWKV_W)
    v_ref[0] = shifted(2 * RWKV_W, 3 * RWKV_W)
    zwa_ref[0] = shifted(3 * RWKV_W, SHIFT_COLS)

    c = SHIFT_COLS
    g = proj(c, c + RWKV_W)
    ga_ref[0] = (g * _sigmoid(g)).astype(BF16)
    c += RWKV_W

    def qk_norm(p, gain):
        sq = (p * p).astype(BF16)
        w = ones_ref.shape[0]
        ms = jnp.concatenate(
            [jnp.dot(sq[:, c0:c0 + w], ones_ref[...], preferred_element_type=F32)
             for c0 in range(0, DIFF_W, w)], axis=1) * (1.0 / HEAD)
        return (p * lax.rsqrt(ms + QK_NORM_EPS) * gain).astype(BF16)

    dq_ref[0] = qk_norm(proj(c, c + DIFF_W), qg_ref[...] * (HEAD ** -0.5 * LOG2E))
    c += DIFF_W
    dk_ref[0] = qk_norm(proj(c, c + DIFF_W), kg_ref[...])
    c += DIFF_W
    dv_ref[0] = proj(c, c + DIFF_W).astype(BF16)
    c += DIFF_W
    g = proj(c, c + DIFF_W)
    gb_ref[0] = (g * _sigmoid(g)).astype(BF16)


def _in_proj(x, norm_g, w_in, shift_mu, q_norm_g, k_norm_g, *, tm):
    B, T, D = x.shape
    n_rep = DIFF_W // HEAD
    qg = jnp.tile(q_norm_g.reshape(1, HEAD), (1, n_rep))
    kg = jnp.tile(k_norm_g.reshape(1, HEAD), (1, n_rep))
    row = lambda w: pl.BlockSpec((1, tm, w), lambda b, t: (b, t, 0))
    full = lambda a: pl.BlockSpec(a.shape, lambda b, t: (0,) * a.ndim,
                                  pipeline_mode=pl.Buffered(1))
    f32o = lambda w: jax.ShapeDtypeStruct((B, T, w), F32)
    bf16o = lambda w: jax.ShapeDtypeStruct((B, T, w), BF16)
    g2 = norm_g.reshape(1, D)
    mu2 = shift_mu.reshape(1, SHIFT_COLS)
    group = np.arange(MXU_WIDTH) // HEAD
    ones_bd = jnp.asarray(group[:, None] == group[None, :], BF16)
    return pl.pallas_call(
        _in_proj_kernel,
        grid=(B, T // tm),
        in_specs=[row(D), full(g2), full(w_in), full(mu2), full(qg), full(kg), full(ones_bd)],
        out_specs=[row(RWKV_W), row(RWKV_W), row(RWKV_W), row(2 * LORA), row(RWKV_W),
                   row(DIFF_W), row(DIFF_W), row(DIFF_W), row(DIFF_W)],
        out_shape=[f32o(RWKV_W), f32o(RWKV_W), f32o(RWKV_W), f32o(2 * LORA), bf16o(RWKV_W),
                   bf16o(DIFF_W), bf16o(DIFF_W), bf16o(DIFF_W), bf16o(DIFF_W)],
        scratch_shapes=[pltpu.VMEM((tm, D), BF16), pltpu.VMEM((1, SHIFT_COLS), F32)],
        compiler_params=pltpu.CompilerParams(
            dimension_semantics=("arbitrary", "arbitrary"), vmem_limit_bytes=VMEM_LIMIT),
        name="in_proj",
    )(x, g2, w_in, mu2, qg, kg, ones_bd)


def _rwkv_kernel(r_ref, k_ref, v_ref, zwa_ref, ga_ref, wd_ref, wa_ref, vec_ref,
                 o_ref, s_ref, y_ref):
    bb, tb, width = r_ref.shape
    rows_total = bb * tb
    chunks_per_seq = tb // CHUNK
    n_chunks = bb * chunks_per_seq
    n_pairs = width // LANES
    P2 = 2 * CHUNK

    @pl.when(pl.program_id(1) == 0)
    def _():
        s_ref[...] = jnp.zeros_like(s_ref)

    w0 = vec_ref[0:1, :]
    a0 = vec_ref[1:2, :]
    k_k = vec_ref[2:3, :]
    k_a = vec_ref[3:4, :]
    r_k = vec_ref[4:5, :]
    ln_w = vec_ref[5:6, :]
    ln_b = vec_ref[6:7, :]

    ri = lax.broadcasted_iota(jnp.int32, (P2, LANES), 0)
    ci = lax.broadcasted_iota(jnp.int32, (P2, LANES), 1)
    same_head = (ri // HEAD) == (ci // HEAD)
    ones_bd = jnp.where(same_head, 1.0, 0.0).astype(BF16)
    ti = lax.broadcasted_iota(jnp.int32, (CHUNK, LANES), 0)
    si = lax.broadcasted_iota(jnp.int32, (CHUNK, LANES), 1) % HEAD
    strict = si < ti
    incl = si <= ti
    eye = si == ti
    lane_blocks = [slice(p * LANES, (p + 1) * LANES) for p in range(n_pairs)]

    def group_sum(z):
        zb = z.astype(BF16)
        return jnp.concatenate(
            [jnp.dot(zb[:, lb], ones_bd, preferred_element_type=F32) for lb in lane_blocks],
            axis=1)

    def bd(z):
        return jnp.where(same_head, jnp.concatenate([z, z], axis=0), 0.0)

    def head_transposed(z):
        zt = bd(z).T
        return zt[:CHUNK] + zt[CHUNK:]

    flat = lambda ref: ref[...].reshape(rows_total, ref.shape[2])
    r = flat(r_ref)
    k = flat(k_ref)
    v = flat(v_ref)
    zwa = flat(zwa_ref)
    u = w0 + _dot(jnp.tanh(zwa), wd_ref[...])
    lw = -math.exp(-0.5) * _sigmoid(u)
    a = _sigmoid(a0 + _dot(zwa, wa_ref[...]))
    kkr = k * k_k
    kk = kkr * lax.rsqrt(jnp.maximum(group_sum(kkr * kkr), 1e-24))
    kmod = k * (1.0 + (a - 1.0) * k_a)

    tri_r = lax.broadcasted_iota(jnp.int32, (CHUNK, CHUNK), 0)
    tri_c = lax.broadcasted_iota(jnp.int32, (CHUNK, CHUNK), 1)
    lower = jnp.where(tri_c <= tri_r, 1.0, 0.0).astype(BF16)
    lw_hi = lw.astype(BF16)
    lw_lo = (lw - lw_hi.astype(F32)).astype(BF16)
    cs = jnp.concatenate(
        [jnp.dot(lower, lw_hi[c * CHUNK:(c + 1) * CHUNK], preferred_element_type=F32)
         + jnp.dot(lower, lw_lo[c * CHUNK:(c + 1) * CHUNK], preferred_element_type=F32)
         for c in range(n_chunks)], axis=0)
    e_pos = jnp.exp(cs)
    e_neg = jnp.exp(-cs)
    al_full = -kk * jnp.exp(cs - lw)
    rt_full = r * e_pos
    bt_full = kk * a * e_neg
    kt_full = kmod * e_neg

    items = [(c, p) for c in range(n_chunks) for p in range(n_pairs)]
    blk = lambda z, c, p: z[c * CHUNK:(c + 1) * CHUNK, lane_blocks[p]]
    e_tot = {it: blk(e_pos, *it)[CHUNK - 1:CHUNK, :] for it in items}
    al = {it: blk(al_full, *it) for it in items}
    rt = {it: blk(rt_full, *it) for it in items}
    bt = {it: blk(bt_full, *it) for it in items}
    kt = {it: blk(kt_full, *it) for it in items}
    v_bd = {it: bd(blk(v, *it)) for it in items}

    gram = {it: _dot_nt(jnp.concatenate([al[it], rt[it]], axis=0),
                        jnp.concatenate([bd(bt[it]), bd(kt[it])], axis=0))
            for it in items}
    a_ab = {it: jnp.where(strict, gram[it][:CHUNK, :LANES], 0.0) for it in items}
    a_kr = {it: jnp.concatenate([jnp.where(strict, gram[it][:CHUNK, LANES:], 0.0),
                                 jnp.where(incl, gram[it][CHUNK:, LANES:], 0.0)], axis=0)
            for it in items}
    a_rb = {it: jnp.where(incl, gram[it][CHUNK:, :LANES], 0.0) for it in items}

    n_levels = int(math.log2(CHUNK))
    tm = {it: jnp.where(eye, 1.0, a_ab[it]) for it in items}
    ap = {it: _dot(a_ab[it], bd(a_ab[it])) for it in items}
    for _ in range(1, n_levels - 1):
        pp = {it: _dot(jnp.concatenate([ap[it], tm[it]], axis=0), bd(ap[it]))
              for it in items}
        ap = {it: pp[it][:CHUNK] for it in items}
        tm = {it: tm[it] + pp[it][CHUNK:] for it in items}
    tm = {it: tm[it] + _dot(tm[it], bd(ap[it])) for it in items}

    av = {it: _dot(a_kr[it], v_bd[it]) for it in items}
    wu = {it: _dot(tm[it], jnp.concatenate([bd(al[it]), bd(av[it][:CHUNK])], axis=1))
          for it in items}
    wu_bd = {it: jnp.concatenate([bd(wu[it][:, :LANES]), bd(wu[it][:, LANES:])], axis=1)
             for it in items}
    ab = {it: _dot(a_rb[it], wu_bd[it]) for it in items}
    bk_t = {it: jnp.concatenate([head_transposed(bt[it] * e_tot[it]),
                                 head_transposed(kt[it] * e_tot[it])], axis=1)
            for it in items}
    gh = {it: _dot(bk_t[it], jnp.concatenate(
        [wu_bd[it], jnp.concatenate([jnp.zeros_like(v_bd[it]), v_bd[it]], axis=1)], axis=0))
          for it in items}
    gq = {it: jnp.concatenate([jnp.where(eye, e_tot[it], 0.0) + gh[it][:, :LANES],
                               rt[it] + ab[it][:, :LANES]], axis=0) for it in items}
    hm = {it: gh[it][:, LANES:] for it in items}
    y1 = {it: ab[it][:, LANES:] + av[it][CHUNK:] for it in items}

    for b in range(bb):
        for p in range(n_pairs):
            s = s_ref[b, p]
            for c in range(b * chunks_per_seq, (b + 1) * chunks_per_seq):
                it = (c, p)
                ys = _dot(gq[it], s)
                y_ref[c * CHUNK:(c + 1) * CHUNK, lane_blocks[p]] = ys[CHUNK:] + y1[it]
                s = bd(ys[:CHUNK] + hm[it])
            s_ref[b, p] = s

    y = y_ref[...]
    mu = group_sum(y) * (1.0 / HEAD)
    d = y - mu
    var = group_sum(d * d) * (1.0 / HEAD)
    yn = d * lax.rsqrt(var + RWKV_GN_EPS) * ln_w + ln_b
    bonus = group_sum(r * kmod * r_k) * v
    out = (yn + bonus) * flat(ga_ref).astype(F32)
    o_ref[...] = out.reshape(bb, tb, width).astype(BF16)


def _rwkv(r, k, v, zwa, ga, w0, w_decay_up, a0, w_iclr_up, k_k, k_a, r_k, ln_w, ln_b, *, bb, tb):
    B, T, W = r.shape
    n_pairs = W // LANES
    zeros = jnp.zeros((LORA, W), F32)
    wd = jnp.concatenate([w_decay_up, zeros], axis=0)
    wa = jnp.concatenate([zeros, w_iclr_up], axis=0)
    vec = jnp.stack([w0, a0, k_k, k_a, r_k.reshape(W), ln_w, ln_b, jnp.zeros((W,), F32)])
    blk = pl.BlockSpec((bb, tb, W), lambda b, t: (b, t, 0))
    full = lambda a: pl.BlockSpec(a.shape, lambda b, t: (0,) * a.ndim)
    return pl.pallas_call(
        _rwkv_kernel,
        grid=(B // bb, T // tb),
        in_specs=[blk, blk, blk,
                  pl.BlockSpec((bb, tb, 2 * LORA), lambda b, t: (b, t, 0)),
                  blk, full(wd), full(wa), full(vec)],
        out_specs=blk,
        out_shape=jax.ShapeDtypeStruct((B, T, W), BF16),
        scratch_shapes=[pltpu.VMEM((bb, n_pairs, LANES, LANES), F32),
                        pltpu.VMEM((bb * tb, W), F32)],
        compiler_params=pltpu.CompilerParams(
            dimension_semantics=("arbitrary", "arbitrary"), vmem_limit_bytes=VMEM_LIMIT),
        name="rwkv7",
    )(r, k, v, zwa, ga, wd, wa, vec)


def _diff_attn_kernel(q_ref, k_ref, v_ref, gb_ref, kpos_ref, slope_ref, subg_ref, lam_ref,
                      o_ref, vt_ref, *, tq):
    seq = q_ref.shape[1]
    n_heads = q_ref.shape[2] // LANES
    n_tiles = seq // tq
    nq = 2 * tq
    head_lanes = [slice(h * LANES, (h + 1) * LANES) for h in range(n_heads)]
    for h, hl in enumerate(head_lanes):
        for r0 in range(0, seq, tq):
            vt_ref[h, :, r0:r0 + tq] = v_ref[0, r0:r0 + tq, hl].astype(F32).T.astype(BF16)

    lane = lax.broadcasted_iota(jnp.int32, (tq, LANES), 1)
    lane1 = lax.broadcasted_iota(jnp.int32, (1, LANES), 1)
    half = tq // 2
    tri = (lax.broadcasted_iota(jnp.int32, (half, tq), 0)
           <= lax.broadcasted_iota(jnp.int32, (half, tq), 1) % half)

    def alibi_lanes(h):
        slope = slope_ref[h][:, 0:1] * LOG2E
        slope_hi = slope.astype(BF16).astype(F32)
        slope_lo = slope - slope_hi
        q_bias = jnp.where(lane1 < 2, slope_hi, jnp.where(lane1 < 4, slope_lo, 0.0))
        return jnp.broadcast_to(q_bias, (nq, LANES)).astype(BF16)

    q_bias = [alibi_lanes(h) for h in range(n_heads)]

    lam_p = lam_ref[...]
    s1 = jnp.sum(lam_p[0:1] * lam_p[1:2], axis=-1, keepdims=True)
    s2 = jnp.sum(lam_p[2:3] * lam_p[3:4], axis=-1, keepdims=True)
    lam = jnp.exp(s1) - jnp.exp(s2) + LAMBDA_INIT

    def all_sublanes(z, op):
        shift = SUBLANES // 2
        while shift:
            z = op(z, pltpu.roll(z, shift, 0))
            shift //= 2
        return z

    for i in range(n_tiles):
        rows = slice(i * tq, (i + 1) * tq)
        klen = (i + 1) * tq
        kmain = klen - half
        for h, hl in enumerate(head_lanes):
            q = q_ref[0, rows, hl]
            zero = jnp.zeros_like(q)
            q0 = jnp.where(lane < HEAD, q, zero)
            q1 = jnp.where(lane >= HEAD, q, zero)
            qcat = jnp.concatenate([q0[:half], q1[:half], q0[half:], q1[half:]], axis=0)
            qcat = jnp.concatenate([qcat, q_bias[h]], axis=1)
            keys = jnp.concatenate([k_ref[0, :klen, hl], kpos_ref[:klen, :]], axis=1)
            st = lax.dot_general(keys[:kmain], qcat, (((1,), (1,)), ((), ())),
                                 preferred_element_type=F32)
            sx = lax.dot_general(keys[kmain:], qcat[tq:], (((1,), (1,)), ((), ())),
                                 preferred_element_type=F32)
            last = st[i * tq:]
            last = jnp.concatenate([jnp.where(tri, last[:, :tq], NEG_INF), last[:, tq:]], axis=1)
            st = last if i == 0 else jnp.concatenate([st[:i * tq], last], axis=0)
            sx = jnp.where(tri, sx, NEG_INF)
            st = st.reshape(kmain // SUBLANES, SUBLANES, nq)
            sx = sx.reshape(half // SUBLANES, SUBLANES, tq)
            mx = all_sublanes(jnp.max(st, axis=0), jnp.maximum)
            mx_late = jnp.maximum(mx[:, tq:], all_sublanes(jnp.max(sx, axis=0), jnp.maximum))
            mx = jnp.concatenate([mx[:, :tq], mx_late], axis=1)
            p = jnp.exp2(st - mx)
            px = jnp.exp2(sx - mx_late)
            ls = all_sublanes(jnp.sum(p, axis=0), jnp.add)
            ls = jnp.concatenate(
                [ls[:, :tq], ls[:, tq:] + all_sublanes(jnp.sum(px, axis=0), jnp.add)], axis=1)
            acc = jnp.dot(vt_ref[h, :, :kmain], p.reshape(kmain, nq).astype(BF16),
                          preferred_element_type=F32)
            acc_late = acc[:, tq:] + jnp.dot(vt_ref[h, :, kmain:klen],
                                             px.reshape(half, tq).astype(BF16),
                                             preferred_element_type=F32)
            acc = jnp.concatenate([acc[:, :tq], acc_late], axis=1)
            att = acc.reshape(LANES // SUBLANES, SUBLANES, nq) / ls
            ot = jnp.concatenate(
                [att[:, :, :half] - lam * att[:, :, half:tq],
                 att[:, :, tq:tq + half] - lam * att[:, :, tq + half:]], axis=2)
            ms = all_sublanes(jnp.sum(ot * ot, axis=0), jnp.add) * (1.0 / LANES)
            ot = (ot * lax.rsqrt(ms + SUBLN_EPS)).reshape(LANES, tq)
            o = ot.T * (subg_ref[...] * (1.0 - LAMBDA_INIT))
            o_ref[0, rows, hl] = (o * gb_ref[0, rows, hl].astype(F32)).astype(BF16)


def _diff_attn(dq, dk, dv, gb, subln_g, lam_params, *, tq, heads_per_step):
    B, T, W = dq.shape
    n_heads = W // LANES
    slopes = jnp.asarray([2.0 ** (-8.0 * (h + 1) / n_heads) for h in range(n_heads)], F32)
    slopes = jnp.broadcast_to(slopes[:, None, None], (n_heads, 1, LANES))
    pos = jnp.arange(T, dtype=jnp.int32)
    pos_lo = pos % KEY_POS_SPLIT
    pos_hi = pos - pos_lo
    kpos = jnp.zeros((T, LANES), F32)
    kpos = kpos.at[:, 0].set(pos_hi).at[:, 1].set(pos_lo).at[:, 2].set(pos_hi).at[:, 3].set(pos_lo)
    kpos = kpos.astype(BF16)
    hw = heads_per_step * LANES
    blk = pl.BlockSpec((1, T, hw), lambda b, h: (b, 0, h))
    subg = subln_g.reshape(1, LANES)
    return pl.pallas_call(
        functools.partial(_diff_attn_kernel, tq=tq),
        grid=(B, n_heads // heads_per_step),
        in_specs=[blk, blk, blk, blk,
                  pl.BlockSpec((T, LANES), lambda b, h: (0, 0)),
                  pl.BlockSpec((heads_per_step, 1, LANES), lambda b, h: (h, 0, 0)),
                  pl.BlockSpec((1, LANES), lambda b, h: (0, 0)),
                  pl.BlockSpec(lam_params.shape, lambda b, h: (0, 0))],
        out_specs=blk,
        out_shape=jax.ShapeDtypeStruct((B, T, W), BF16),
        scratch_shapes=[pltpu.VMEM((heads_per_step, LANES, T), BF16)],
        compiler_params=pltpu.CompilerParams(
            dimension_semantics=("arbitrary", "arbitrary"), vmem_limit_bytes=VMEM_LIMIT),
        name="diffattn",
    )(dq, dk, dv, gb, kpos, slopes, subg, lam_params)


def _out_proj_kernel(ya_ref, yb_ref, x_ref, w_ref, o_ref):
    wa = ya_ref.shape[2]
    acc = jnp.dot(ya_ref[0], w_ref[:wa, :].astype(BF16), preferred_element_type=F32)
    acc = acc + jnp.dot(yb_ref[0], w_ref[wa:, :].astype(BF16), preferred_element_type=F32)
    o_ref[0] = x_ref[0] + acc


def _out_proj(ya, yb, x, w_out, *, tm):
    B, T, D = x.shape
    row = lambda w: pl.BlockSpec((1, tm, w), lambda b, t: (b, t, 0))
    return pl.pallas_call(
        _out_proj_kernel,
        grid=(B, T // tm),
        in_specs=[row(ya.shape[2]), row(yb.shape[2]), row(D),
                  pl.BlockSpec(w_out.shape, lambda b, t: (0, 0))],
        out_specs=row(D),
        out_shape=jax.ShapeDtypeStruct((B, T, D), F32),
        compiler_params=pltpu.CompilerParams(
            dimension_semantics=("arbitrary", "arbitrary"), vmem_limit_bytes=VMEM_LIMIT),
        name="out_proj",
    )(ya, yb, x, w_out)


def kernel(x, norm_g, w_in, shift_mu, w0, w_decay_up, a0, w_iclr_up, k_k, k_a, r_k, ln_x_w, ln_x_b, q_norm_g, k_norm_g, lambda_q1, lambda_k1, lambda_q2, lambda_k2, subln_g, w_out):
    depth = norm_g.shape[0]
    assert depth == 1, "lambda_init is specialised to a single layer"
    h = x
    for l in range(depth):
        r, k, v, zwa, ga, dq, dk, dv, gb = _in_proj(
            h, norm_g[l], w_in[l], shift_mu[l], q_norm_g[l], k_norm_g[l], tm=PROJ_ROWS)
        ya = _rwkv(r, k, v, zwa, ga, w0[l], w_decay_up[l], a0[l], w_iclr_up[l], k_k[l], k_a[l],
                   r_k[l], ln_x_w[l], ln_x_b[l], bb=RWKV_SEQS, tb=RWKV_ROWS)
        lam_params = jnp.stack([lambda_q1[l], lambda_k1[l], lambda_q2[l], lambda_k2[l]])
        yb = _diff_attn(dq, dk, dv, gb, subln_g[l], lam_params,
                        tq=ATTN_QUERIES, heads_per_step=ATTN_HEADS)
        h = _out_proj(ya, yb, h, w_out[l], tm=PROJ_ROWS)
    return h
```

```python
import functools
import math

import jax
import jax.numpy as jnp
import numpy as np
from jax import lax
from jax.experimental import pallas as pl
from jax.experimental.pallas import tpu as pltpu

F32 = jnp.float32
BF16 = jnp.bfloat16

LANES = 128
SUBLANES = 8
MXU_WIDTH = 256
HEAD = 64
CHUNK = 64
RWKV_W = 512
DIFF_W = 512
LORA = 64
SHIFT_COLS = 3 * RWKV_W + 2 * LORA
NORM_EPS = 1e-6
QK_NORM_EPS = 1e-6
SUBLN_EPS = 1e-5
RWKV_GN_EPS = 64e-5
LAMBDA_INIT = 0.8 - 0.6 * math.exp(-0.3 * 0)
NEG_INF = -1e30
LOG2E = math.log2(math.e)
KEY_POS_SPLIT = 16
VMEM_LIMIT = 56 * 1024 * 1024

PROJ_ROWS = 1024
RWKV_SEQS, RWKV_ROWS = 8, 128
ATTN_QUERIES = 512
ATTN_HEADS = 4


def _dot(a, b):
    return jnp.dot(a.astype(BF16), b.astype(BF16), preferred_element_type=F32)


def _dot_nt(a, b):
    return lax.dot_general(a.astype(BF16), b.astype(BF16), (((1,), (1,)), ((), ())),
                           preferred_element_type=F32)


def _sigmoid(x):
    return 0.5 * jnp.tanh(0.5 * x) + 0.5


def _in_proj_kernel(x_ref, g_ref, w_ref, mu_ref, qg_ref, kg_ref, ones_ref,
                    r_ref, k_ref, v_ref, zwa_ref, ga_ref, dq_ref, dk_ref, dv_ref, gb_ref,
                    hn_ref, carry_ref):
    t = pl.program_id(1)
    x = x_ref[0]
    tm = x.shape[0]
    ms = jnp.mean(x * x, axis=-1, keepdims=True)
    hn_ref[...] = (x * lax.rsqrt(ms + NORM_EPS) * g_ref[...]).astype(BF16)

    @pl.when(t == 0)
    def _():
        carry_ref[...] = jnp.zeros_like(carry_ref)

    row0 = lax.broadcasted_iota(jnp.int32, (tm, 1), 0) == 0

    def proj(c0, c1):
        return jnp.dot(hn_ref[...], w_ref[:, c0:c1].astype(BF16), preferred_element_type=F32)

    def shifted(c0, c1):
        p = proj(c0, c1)
        prev = jnp.where(row0, carry_ref[:, c0:c1], pltpu.roll(p, 1, 0))
        carry_ref[:, c0:c1] = p[tm - 1:tm, :]
        return p + (prev - p) * mu_ref[:, c0:c1]

    r_ref[0] = shifted(0, RWKV_W)
    k_ref[0] = shifted(RWKV_W, 2 * RWKV_W)
    v_ref[0] = shifted(2 * RWKV_W, 3 * RWKV_W)
    zwa_ref[0] = shifted(3 * RWKV_W, SHIFT_COLS)

    c = SHIFT_COLS
    g = proj(c, c + RWKV_W)
    ga_ref[0] = (g * _sigmoid(g)).astype(BF16)
    c += RWKV_W

    def qk_norm(p, gain):
        sq = (p * p).astype(BF16)
        w = ones_ref.shape[0]
        ms = jnp.concatenate(
            [jnp.dot(sq[:, c0:c0 + w], ones_ref[...], preferred_element_type=F32)
             for c0 in range(0, DIFF_W, w)], axis=1) * (1.0 / HEAD)
        return (p * lax.rsqrt(ms + QK_NORM_EPS) * gain).astype(BF16)

    dq_ref[0] = qk_norm(proj(c, c + DIFF_W), qg_ref[...] * (HEAD ** -0.5 * LOG2E))
    c += DIFF_W
    dk_ref[0] = qk_norm(proj(c, c + DIFF_W), kg_ref[...])
    c += DIFF_W
    dv_ref[0] = proj(c, c + DIFF_W).astype(BF16)
    c += DIFF_W
    g = proj(c, c + DIFF_W)
    gb_ref[0] = (g * _sigmoid(g)).astype(BF16)


def _in_proj(x, norm_g, w_in, shift_mu, q_norm_g, k_norm_g, *, tm):
    B, T, D = x.shape
    assert T % tm == 0, (T, tm)
    n_rep = DIFF_W // HEAD
    qg = jnp.tile(q_norm_g.reshape(1, HEAD), (1, n_rep))
    kg = jnp.tile(k_norm_g.reshape(1, HEAD), (1, n_rep))
    row = lambda w: pl.BlockSpec((1, tm, w), lambda b, t: (b, t, 0))
    full = lambda a: pl.BlockSpec(a.shape, lambda b, t: (0,) * a.ndim,
                                  pipeline_mode=pl.Buffered(1))
    f32o = lambda w: jax.ShapeDtypeStruct((B, T, w), F32)
    bf16o = lambda w: jax.ShapeDtypeStruct((B, T, w), BF16)
    g2 = norm_g.reshape(1, D)
    mu2 = shift_mu.reshape(1, SHIFT_COLS)
    group = np.arange(MXU_WIDTH) // HEAD
    ones_bd = jnp.asarray(group[:, None] == group[None, :], BF16)
    return pl.pallas_call(
        _in_proj_kernel,
        grid=(B, T // tm),
        in_specs=[row(D), full(g2), full(w_in), full(mu2), full(qg), full(kg), full(ones_bd)],
        out_specs=[row(RWKV_W), row(RWKV_W), row(RWKV_W), row(2 * LORA), row(RWKV_W),
                   row(DIFF_W), row(DIFF_W), row(DIFF_W), row(DIFF_W)],
        out_shape=[f32o(RWKV_W), f32o(RWKV_W), f32o(RWKV_W), f32o(2 * LORA), bf16o(RWKV_W),
                   bf16o(DIFF_W), bf16o(DIFF_W), bf16o(DIFF_W), bf16o(DIFF_W)],
        scratch_shapes=[pltpu.VMEM((tm, D), BF16), pltpu.VMEM((1, SHIFT_COLS), F32)],
        compiler_params=pltpu.CompilerParams(
            dimension_semantics=("arbitrary", "arbitrary"), vmem_limit_bytes=VMEM_LIMIT),
        name="in_proj",
    )(x, g2, w_in, mu2, qg, kg, ones_bd)


def _rwkv_kernel(r_ref, k_ref, v_ref, zwa_ref, ga_ref, wd_ref, wa_ref, vec_ref,
                 o_ref, s_ref, y_ref):
    bb, tb, width = r_ref.shape
    rows_total = bb * tb
    chunks_per_seq = tb // CHUNK
    n_chunks = bb * chunks_per_seq
    n_pairs = width // LANES
    P2 = 2 * CHUNK

    @pl.when(pl.program_id(1) == 0)
    def _():
        s_ref[...] = jnp.zeros_like(s_ref)

    w0 = vec_ref[0:1, :]
    a0 = vec_ref[1:2, :]
    k_k = vec_ref[2:3, :]
    k_a = vec_ref[3:4, :]
    r_k = vec_ref[4:5, :]
    ln_w = vec_ref[5:6, :]
    ln_b = vec_ref[6:7, :]

    ri = lax.broadcasted_iota(jnp.int32, (P2, LANES), 0)
    ci = lax.broadcasted_iota(jnp.int32, (P2, LANES), 1)
    same_head = (ri // HEAD) == (ci // HEAD)
    ones_bd = jnp.where(same_head, 1.0, 0.0).astype(BF16)
    ti = lax.broadcasted_iota(jnp.int32, (CHUNK, LANES), 0)
    si = lax.broadcasted_iota(jnp.int32, (CHUNK, LANES), 1) % HEAD
    strict = si < ti
    incl = si <= ti
    eye = si == ti
    lane_blocks = [slice(p * LANES, (p + 1) * LANES) for p in range(n_pairs)]

    def group_sum(z):
        zb = z.astype(BF16)
        return jnp.concatenate(
            [jnp.dot(zb[:, lb], ones_bd, preferred_element_type=F32) for lb in lane_blocks],
            axis=1)

    def bd(z):
        return jnp.where(same_head, jnp.concatenate([z, z], axis=0), 0.0)

    def head_transposed(z):
        zt = bd(z).T
        return zt[:CHUNK] + zt[CHUNK:]

    flat = lambda ref: ref[...].reshape(rows_total, ref.shape[2])
    r = flat(r_ref)
    k = flat(k_ref)
    v = flat(v_ref)
    zwa = flat(zwa_ref)
    u = w0 + _dot(jnp.tanh(zwa), wd_ref[...])
    lw = -math.exp(-0.5) * _sigmoid(u)
    a = _sigmoid(a0 + _dot(zwa, wa_ref[...]))
    kkr = k * k_k
    kk = kkr * lax.rsqrt(jnp.maximum(group_sum(kkr * kkr), 1e-24))
    kmod = k * (1.0 + (a - 1.0) * k_a)

    tri_r = lax.broadcasted_iota(jnp.int32, (CHUNK, CHUNK), 0)
    tri_c = lax.broadcasted_iota(jnp.int32, (CHUNK, CHUNK), 1)
    lower = jnp.where(tri_c <= tri_r, 1.0, 0.0).astype(BF16)
    lw_hi = lw.astype(BF16)
    lw_lo = (lw - lw_hi.astype(F32)).astype(BF16)
    cs = jnp.concatenate(
        [jnp.dot(lower, lw_hi[c * CHUNK:(c + 1) * CHUNK], preferred_element_type=F32)
         + jnp.dot(lower, lw_lo[c * CHUNK:(c + 1) * CHUNK], preferred_element_type=F32)
         for c in range(n_chunks)], axis=0)
    e_pos = jnp.exp(cs)
    e_neg = jnp.exp(-cs)
    al_full = -kk * jnp.exp(cs - lw)
    rt_full = r * e_pos
    bt_full = kk * a * e_neg
    kt_full = kmod * e_neg

    items = [(c, p) for c in range(n_chunks) for p in range(n_pairs)]
    blk = lambda z, c, p: z[c * CHUNK:(c + 1) * CHUNK, lane_blocks[p]]
    e_tot = {it: blk(e_pos, *it)[CHUNK - 1:CHUNK, :] for it in items}
    al = {it: blk(al_full, *it) for it in items}
    rt = {it: blk(rt_full, *it) for it in items}
    bt = {it: blk(bt_full, *it) for it in items}
    kt = {it: blk(kt_full, *it) for it in items}
    v_bd = {it: bd(blk(v, *it)) for it in items}

    gram = {it: _dot_nt(jnp.concatenate([al[it], rt[it]], axis=0),
                        jnp.concatenate([bd(bt[it]), bd(kt[it])], axis=0))
            for it in items}
    a_ab = {it: jnp.where(strict, gram[it][:CHUNK, :LANES], 0.0) for it in items}
    a_kr = {it: jnp.concatenate([jnp.where(strict, gram[it][:CHUNK, LANES:], 0.0),
                                 jnp.where(incl, gram[it][CHUNK:, LANES:], 0.0)], axis=0)
            for it in items}
    a_rb = {it: jnp.where(incl, gram[it][CHUNK:, :LANES], 0.0) for it in items}

    n_levels = int(math.log2(CHUNK))
    tm = {it: jnp.where(eye, 1.0, a_ab[it]) for it in items}
    ap = {it: _dot(a_ab[it], bd(a_ab[it])) for it in items}
    for _ in range(1, n_levels - 1):
        pp = {it: _dot(jnp.concatenate([ap[it], tm[it]], axis=0), bd(ap[it]))
              for it in items}
        ap = {it: pp[it][:CHUNK] for it in items}
        tm = {it: tm[it] + pp[it][CHUNK:] for it in items}
    tm = {it: tm[it] + _dot(tm[it], bd(ap[it])) for it in items}

    av = {it: _dot(a_kr[it], v_bd[it]) for it in items}
    wu = {it: _dot(tm[it], jnp.concatenate([bd(al[it]), bd(av[it][:CHUNK])], axis=1))
          for it in items}
    wu_bd = {it: jnp.concatenate([bd(wu[it][:, :LANES]), bd(wu[it][:, LANES:])], axis=1)
             for it in items}
    ab = {it: _dot(a_rb[it], wu_bd[it]) for it in items}
    bk_t = {it: jnp.concatenate([head_transposed(bt[it] * e_tot[it]),
                                 head_transposed(kt[it] * e_tot[it])], axis=1)
            for it in items}
    gh = {it: _dot(bk_t[it], jnp.concatenate(
        [wu_bd[it], jnp.concatenate([jnp.zeros_like(v_bd[it]), v_bd[it]], axis=1)], axis=0))
          for it in items}
    gq = {it: jnp.concatenate([jnp.where(eye, e_tot[it], 0.0) + gh[it][:, :LANES],
                               rt[it] + ab[it][:, :LANES]], axis=0) for it in items}
    hm = {it: gh[it][:, LANES:] for it in items}
    y1 = {it: ab[it][:, LANES:] + av[it][CHUNK:] for it in items}

    for b in range(bb):
        for p in range(n_pairs):
            s = s_ref[b, p]
            for c in range(b * chunks_per_seq, (b + 1) * chunks_per_seq):
                it = (c, p)
                ys = _dot(gq[it], s)
                y_ref[c * CHUNK:(c + 1) * CHUNK, lane_blocks[p]] = ys[CHUNK:] + y1[it]
                s = bd(ys[:CHUNK] + hm[it])
            s_ref[b, p] = s

    y = y_ref[...]
    mu = group_sum(y) * (1.0 / HEAD)
    d = y - mu
    var = group_sum(d * d) * (1.0 / HEAD)
    yn = d * lax.rsqrt(var + RWKV_GN_EPS) * ln_w + ln_b
    bonus = group_sum(r * kmod * r_k) * v
    out = (yn + bonus) * flat(ga_ref).astype(F32)
    o_ref[...] = out.reshape(bb, tb, width).astype(BF16)


def _rwkv(r, k, v, zwa, ga, w0, w_decay_up, a0, w_iclr_up, k_k, k_a, r_k, ln_w, ln_b, *, bb, tb):
    B, T, W = r.shape
    assert B % bb == 0 and T % tb == 0 and tb % CHUNK == 0, (B, T, bb, tb)
    n_pairs = W // LANES
    zeros = jnp.zeros((LORA, W), F32)
    wd = jnp.concatenate([w_decay_up, zeros], axis=0)
    wa = jnp.concatenate([zeros, w_iclr_up], axis=0)
    vec = jnp.stack([w0, a0, k_k, k_a, r_k.reshape(W), ln_w, ln_b, jnp.zeros((W,), F32)])
    blk = pl.BlockSpec((bb, tb, W), lambda b, t: (b, t, 0))
    full = lambda a: pl.BlockSpec(a.shape, lambda b, t: (0,) * a.ndim)
    return pl.pallas_call(
        _rwkv_kernel,
        grid=(B // bb, T // tb),
        in_specs=[blk, blk, blk,
                  pl.BlockSpec((bb, tb, 2 * LORA), lambda b, t: (b, t, 0)),
                  blk, full(wd), full(wa), full(vec)],
        out_specs=blk,
        out_shape=jax.ShapeDtypeStruct((B, T, W), BF16),
        scratch_shapes=[pltpu.VMEM((bb, n_pairs, LANES, LANES), F32),
                        pltpu.VMEM((bb * tb, W), F32)],
        compiler_params=pltpu.CompilerParams(
            dimension_semantics=("arbitrary", "arbitrary"), vmem_limit_bytes=VMEM_LIMIT),
        name="rwkv7",
    )(r, k, v, zwa, ga, wd, wa, vec)


def _diff_attn_kernel(q_ref, k_ref, v_ref, gb_ref, kpos_ref, slope_ref, subg_ref, lam_ref,
                      o_ref, vt_ref, *, tq):
    seq = q_ref.shape[1]
    n_heads = q_ref.shape[2] // LANES
    n_tiles = seq // tq
    nq = 2 * tq
    head_lanes = [slice(h * LANES, (h + 1) * LANES) for h in range(n_heads)]
    for h, hl in enumerate(head_lanes):
        for r0 in range(0, seq, tq):
            vt_ref[h, :, r0:r0 + tq] = v_ref[0, r0:r0 + tq, hl].astype(F32).T.astype(BF16)

    lane = lax.broadcasted_iota(jnp.int32, (tq, LANES), 1)
    lane1 = lax.broadcasted_iota(jnp.int32, (1, LANES), 1)
    half = tq // 2
    tri = (lax.broadcasted_iota(jnp.int32, (half, tq), 0)
           <= lax.broadcasted_iota(jnp.int32, (half, tq), 1) % half)

    def alibi_lanes(h):
        slope = slope_ref[h][:, 0:1] * LOG2E
        slope_hi = slope.astype(BF16).astype(F32)
        slope_lo = slope - slope_hi
        q_bias = jnp.where(lane1 < 2, slope_hi, jnp.where(lane1 < 4, slope_lo, 0.0))
        return jnp.broadcast_to(q_bias, (nq, LANES)).astype(BF16)

    q_bias = [alibi_lanes(h) for h in range(n_heads)]

    lam_p = lam_ref[...]
    s1 = jnp.sum(lam_p[0:1] * lam_p[1:2], axis=-1, keepdims=True)
    s2 = jnp.sum(lam_p[2:3] * lam_p[3:4], axis=-1, keepdims=True)
    lam = jnp.exp(s1) - jnp.exp(s2) + LAMBDA_INIT

    def all_sublanes(z, op):
        shift = SUBLANES // 2
        while shift:
            z = op(z, pltpu.roll(z, shift, 0))
            shift //= 2
        return z

    for i in range(n_tiles):
        rows = slice(i * tq, (i + 1) * tq)
        klen = (i + 1) * tq
        kmain = klen - half
        for h, hl in enumerate(head_lanes):
            q = q_ref[0, rows, hl]
            zero = jnp.zeros_like(q)
            q0 = jnp.where(lane < HEAD, q, zero)
            q1 = jnp.where(lane >= HEAD, q, zero)
            qcat = jnp.concatenate([q0[:half], q1[:half], q0[half:], q1[half:]], axis=0)
            qcat = jnp.concatenate([qcat, q_bias[h]], axis=1)
            keys = jnp.concatenate([k_ref[0, :klen, hl], kpos_ref[:klen, :]], axis=1)
            st = lax.dot_general(keys[:kmain], qcat, (((1,), (1,)), ((), ())),
                                 preferred_element_type=F32)
            sx = lax.dot_general(keys[kmain:], qcat[tq:], (((1,), (1,)), ((), ())),
                                 preferred_element_type=F32)
            last = st[i * tq:]
            last = jnp.concatenate([jnp.where(tri, last[:, :tq], NEG_INF), last[:, tq:]], axis=1)
            st = last if i == 0 else jnp.concatenate([st[:i * tq], last], axis=0)
            sx = jnp.where(tri, sx, NEG_INF)
            st = st.reshape(kmain // SUBLANES, SUBLANES, nq)
            sx = sx.reshape(half // SUBLANES, SUBLANES, tq)
            mx = all_sublanes(jnp.max(st, axis=0), jnp.maximum)
            mx_late = jnp.maximum(mx[:, tq:], all_sublanes(jnp.max(sx, axis=0), jnp.maximum))
            mx = jnp.concatenate([mx[:, :tq], mx_late], axis=1)
            p = jnp.exp2(st - mx)
            px = jnp.exp2(sx - mx_late)
            ls = all_sublanes(jnp.sum(p, axis=0), jnp.add)
            ls = jnp.concatenate(
                [ls[:, :tq], ls[:, tq:] + all_sublanes(jnp.sum(px, axis=0), jnp.add)], axis=1)
            acc = jnp.dot(vt_ref[h, :, :kmain], p.reshape(kmain, nq).astype(BF16),
                          preferred_element_type=F32)
            acc_late = acc[:, tq:] + jnp.dot(vt_ref[h, :, kmain:klen],
                                             px.reshape(half, tq).astype(BF16),
                                             preferred_element_type=F32)
            acc = jnp.concatenate([acc[:, :tq], acc_late], axis=1)
            att = acc.reshape(LANES // SUBLANES, SUBLANES, nq) / ls
            ot = jnp.concatenate(
                [att[:, :, :half] - lam * att[:, :, half:tq],
                 att[:, :, tq:tq + half] - lam * att[:, :, tq + half:]], axis=2)
            ms = all_sublanes(jnp.sum(ot * ot, axis=0), jnp.add) * (1.0 / LANES)
            ot = (ot * lax.rsqrt(ms + SUBLN_EPS)).reshape(LANES, tq)
            o = ot.T * (subg_ref[...] * (1.0 - LAMBDA_INIT))
            o_ref[0, rows, hl] = (o * gb_ref[0, rows, hl].astype(F32)).astype(BF16)


def _diff_attn(dq, dk, dv, gb, subln_g, lam_params, *, tq, heads_per_step):
    B, T, W = dq.shape
    n_heads = W // LANES
    assert T % tq == 0 and tq % (2 * SUBLANES) == 0 and n_heads % heads_per_step == 0
    slopes = jnp.asarray([2.0 ** (-8.0 * (h + 1) / n_heads) for h in range(n_heads)], F32)
    slopes = jnp.broadcast_to(slopes[:, None, None], (n_heads, 1, LANES))
    pos = jnp.arange(T, dtype=jnp.int32)
    pos_lo = pos % KEY_POS_SPLIT
    pos_hi = pos - pos_lo
    kpos = jnp.zeros((T, LANES), F32)
    kpos = kpos.at[:, 0].set(pos_hi).at[:, 1].set(pos_lo).at[:, 2].set(pos_hi).at[:, 3].set(pos_lo)
    kpos = kpos.astype(BF16)
    hw = heads_per_step * LANES
    blk = pl.BlockSpec((1, T, hw), lambda b, h: (b, 0, h))
    subg = subln_g.reshape(1, LANES)
    return pl.pallas_call(
        functools.partial(_diff_attn_kernel, tq=tq),
        grid=(B, n_heads // heads_per_step),
        in_specs=[blk, blk, blk, blk,
                  pl.BlockSpec((T, LANES), lambda b, h: (0, 0)),
                  pl.BlockSpec((heads_per_step, 1, LANES), lambda b, h: (h, 0, 0)),
                  pl.BlockSpec((1, LANES), lambda b, h: (0, 0)),
                  pl.BlockSpec(lam_params.shape, lambda b, h: (0, 0))],
        out_specs=blk,
        out_shape=jax.ShapeDtypeStruct((B, T, W), BF16),
        scratch_shapes=[pltpu.VMEM((heads_per_step, LANES, T), BF16)],
        compiler_params=pltpu.CompilerParams(
            dimension_semantics=("arbitrary", "arbitrary"), vmem_limit_bytes=VMEM_LIMIT),
        name="diffattn",
    )(dq, dk, dv, gb, kpos, slopes, subg, lam_params)


def _out_proj_kernel(ya_ref, yb_ref, x_ref, w_ref, o_ref):
    wa = ya_ref.shape[2]
    acc = jnp.dot(ya_ref[0], w_ref[:wa, :].astype(BF16), preferred_element_type=F32)
    acc = acc + jnp.dot(yb_ref[0], w_ref[wa:, :].astype(BF16), preferred_element_type=F32)
    o_ref[0] = x_ref[0] + acc


def _out_proj(ya, yb, x, w_out, *, tm):
    B, T, D = x.shape
    assert T % tm == 0, (T, tm)
    row = lambda w: pl.BlockSpec((1, tm, w), lambda b, t: (b, t, 0))
    return pl.pallas_call(
        _out_proj_kernel,
        grid=(B, T // tm),
        in_specs=[row(ya.shape[2]), row(yb.shape[2]), row(D),
                  pl.BlockSpec(w_out.shape, lambda b, t: (0, 0))],
        out_specs=row(D),
        out_shape=jax.ShapeDtypeStruct((B, T, D), F32),
        compiler_params=pltpu.CompilerParams(
            dimension_semantics=("arbitrary", "arbitrary"), vmem_limit_bytes=VMEM_LIMIT),
        name="out_proj",
    )(ya, yb, x, w_out)


def kernel(x, norm_g, w_in, shift_mu, w0, w_decay_up, a0, w_iclr_up, k_k, k_a, r_k, ln_x_w, ln_x_b, q_norm_g, k_norm_g, lambda_q1, lambda_k1, lambda_q2, lambda_k2, subln_g, w_out):
    depth = norm_g.shape[0]
    assert depth == 1, "lambda_init is specialised to a single layer"
    h = x
    for l in range(depth):
        r, k, v, zwa, ga, dq, dk, dv, gb = _in_proj(
            h, norm_g[l], w_in[l], shift_mu[l], q_norm_g[l], k_norm_g[l], tm=PROJ_ROWS)
        ya = _rwkv(r, k, v, zwa, ga, w0[l], w_decay_up[l], a0[l], w_iclr_up[l], k_k[l], k_a[l],
                   r_k[l], ln_x_w[l], ln_x_b[l], bb=RWKV_SEQS, tb=RWKV_ROWS)
        lam_params = jnp.stack([lambda_q1[l], lambda_k1[l], lambda_q2[l], lambda_k2[l]])
        yb = _diff_attn(dq, dk, dv, gb, subln_g[l], lam_params,
                        tq=ATTN_QUERIES, heads_per_step=ATTN_HEADS)
        h = _out_proj(ya, yb, h, w_out[l], tm=PROJ_ROWS)
    return h
```

```python
import functools
import math

import jax
import jax.numpy as jnp
import numpy as np
from jax import lax
from jax.experimental import pallas as pl
from jax.experimental.pallas import tpu as pltpu

F32 = jnp.float32
BF16 = jnp.bfloat16

LANES = 128
SUBLANES = 8
MXU_WIDTH = 256
HEAD = 64
CHUNK = 64
RWKV_W = 512
DIFF_W = 512
LORA = 64
SHIFT_COLS = 3 * RWKV_W + 2 * LORA
NORM_EPS = 1e-6
QK_NORM_EPS = 1e-6
SUBLN_EPS = 1e-5
RWKV_GN_EPS = 64e-5
LAMBDA_INIT = 0.8 - 0.6 * math.exp(-0.3 * 0)
NEG_INF = -1e30
LOG2E = math.log2(math.e)
KEY_POS_SPLIT = 16
VMEM_LIMIT = 56 * 1024 * 1024

PROJ_ROWS = 1024
RWKV_SEQS, RWKV_ROWS = 8, 128
ATTN_QUERIES = 512
ATTN_HEADS = 4


def _dot(a, b):
    return jnp.dot(a.astype(BF16), b.astype(BF16), preferred_element_type=F32)


def _dot_nt(a, b):
    return lax.dot_general(a.astype(BF16), b.astype(BF16), (((1,), (1,)), ((), ())),
                           preferred_element_type=F32)


def _sigmoid(x):
    return 0.5 * jnp.tanh(0.5 * x) + 0.5


def _in_proj_kernel(x_ref, g_ref, w_ref, mu_ref, qg_ref, kg_ref, ones_ref,
                    r_ref, k_ref, v_ref, zwa_ref, ga_ref, dq_ref, dk_ref, dv_ref, gb_ref,
                    hn_ref, carry_ref):
    t = pl.program_id(1)
    x = x_ref[0]
    tm = x.shape[0]
    ms = jnp.mean(x * x, axis=-1, keepdims=True)
    hn_ref[...] = (x * lax.rsqrt(ms + NORM_EPS) * g_ref[...]).astype(BF16)

    @pl.when(t == 0)
    def _():
        carry_ref[...] = jnp.zeros_like(carry_ref)

    row0 = lax.broadcasted_iota(jnp.int32, (tm, 1), 0) == 0

    def proj(c0, c1):
        return jnp.dot(hn_ref[...], w_ref[:, c0:c1].astype(BF16), preferred_element_type=F32)

    def shifted(c0, c1):
        p = proj(c0, c1)
        prev = jnp.where(row0, carry_ref[:, c0:c1], pltpu.roll(p, 1, 0))
        carry_ref[:, c0:c1] = p[tm - 1:tm, :]
        return p + (prev - p) * mu_ref[:, c0:c1]

    r_ref[0] = shifted(0, RWKV_W)
    k_ref[0] = shifted(RWKV_W, 2 * RWKV_W)
    v_ref[0] = shifted(2 * RWKV_W, 3 * RWKV_W)
    zwa_ref[0] = shifted(3 * RWKV_W, SHIFT_COLS)

    c = SHIFT_COLS
    g = proj(c, c + RWKV_W)
    ga_ref[0] = (g * _sigmoid(g)).astype(BF16)
    c += RWKV_W

    def qk_norm(p, gain):
        sq = (p * p).astype(BF16)
        w = ones_ref.shape[0]
        ms = jnp.concatenate(
            [jnp.dot(sq[:, c0:c0 + w], ones_ref[...], preferred_element_type=F32)
             for c0 in range(0, DIFF_W, w)], axis=1) * (1.0 / HEAD)
        return (p * lax.rsqrt(ms + QK_NORM_EPS) * gain).astype(BF16)

    dq_ref[0] = qk_norm(proj(c, c + DIFF_W), qg_ref[...] * (HEAD ** -0.5 * LOG2E))
    c += DIFF_W
    dk_ref[0] = qk_norm(proj(c, c + DIFF_W), kg_ref[...])
    c += DIFF_W
    dv_ref[0] = proj(c, c + DIFF_W).astype(BF16)
    c += DIFF_W
    g = proj(c, c + DIFF_W)
    gb_ref[0] = (g * _sigmoid(g)).astype(BF16)


def _in_proj(x, norm_g, w_in, shift_mu, q_norm_g, k_norm_g, *, tm):
    B, T, D = x.shape
    assert T % tm == 0, (T, tm)
    n_rep = DIFF_W // HEAD
    qg = jnp.tile(q_norm_g.reshape(1, HEAD), (1, n_rep))
    kg = jnp.tile(k_norm_g.reshape(1, HEAD), (1, n_rep))
    row = lambda w: pl.BlockSpec((1, tm, w), lambda b, t: (b, t, 0))
    full = lambda a: pl.BlockSpec(a.shape, lambda b, t: (0,) * a.ndim,
                                  pipeline_mode=pl.Buffered(1))
    f32o = lambda w: jax.ShapeDtypeStruct((B, T, w), F32)
    bf16o = lambda w: jax.ShapeDtypeStruct((B, T, w), BF16)
    g2 = norm_g.reshape(1, D)
    mu2 = shift_mu.reshape(1, SHIFT_COLS)
    group = np.arange(MXU_WIDTH) // HEAD
    ones_bd = jnp.asarray(group[:, None] == group[None, :], BF16)
    return pl.pallas_call(
        _in_proj_kernel,
        grid=(B, T // tm),
        in_specs=[row(D), full(g2), full(w_in), full(mu2), full(qg), full(kg), full(ones_bd)],
        out_specs=[row(RWKV_W), row(RWKV_W), row(RWKV_W), row(2 * LORA), row(RWKV_W),
                   row(DIFF_W), row(DIFF_W), row(DIFF_W), row(DIFF_W)],
        out_shape=[f32o(RWKV_W), f32o(RWKV_W), f32o(RWKV_W), f32o(2 * LORA), bf16o(RWKV_W),
                   bf16o(DIFF_W), bf16o(DIFF_W), bf16o(DIFF_W), bf16o(DIFF_W)],
        scratch_shapes=[pltpu.VMEM((tm, D), BF16), pltpu.VMEM((1, SHIFT_COLS), F32)],
        compiler_params=pltpu.CompilerParams(
            dimension_semantics=("arbitrary", "arbitrary"), vmem_limit_bytes=VMEM_LIMIT),
        name="in_proj",
    )(x, g2, w_in, mu2, qg, kg, ones_bd)


def _rwkv_kernel(r_ref, k_ref, v_ref, zwa_ref, ga_ref, wd_ref, wa_ref, vec_ref,
                 o_ref, s_ref, y_ref):
    bb, tb, width = r_ref.shape
    rows_total = bb * tb
    chunks_per_seq = tb // CHUNK
    n_chunks = bb * chunks_per_seq
    n_pairs = width // LANES
    P2 = 2 * CHUNK

    @pl.when(pl.program_id(1) == 0)
    def _():
        s_ref[...] = jnp.zeros_like(s_ref)

    w0 = vec_ref[0:1, :]
    a0 = vec_ref[1:2, :]
    k_k = vec_ref[2:3, :]
    k_a = vec_ref[3:4, :]
    r_k = vec_ref[4:5, :]
    ln_w = vec_ref[5:6, :]
    ln_b = vec_ref[6:7, :]

    ri = lax.broadcasted_iota(jnp.int32, (P2, LANES), 0)
    ci = lax.broadcasted_iota(jnp.int32, (P2, LANES), 1)
    same_head = (ri // HEAD) == (ci // HEAD)
    ones_bd = jnp.where(same_head, 1.0, 0.0).astype(BF16)
    ti = lax.broadcasted_iota(jnp.int32, (CHUNK, LANES), 0)
    si = lax.broadcasted_iota(jnp.int32, (CHUNK, LANES), 1) % HEAD
    strict = si < ti
    incl = si <= ti
    eye = si == ti
    lane_blocks = [slice(p * LANES, (p + 1) * LANES) for p in range(n_pairs)]

    def group_sum(z):
        zb = z.astype(BF16)
        return jnp.concatenate(
            [jnp.dot(zb[:, lb], ones_bd, preferred_element_type=F32) for lb in lane_blocks],
            axis=1)

    def bd(z):
        return jnp.where(same_head, jnp.concatenate([z, z], axis=0), 0.0)

    def head_transposed(z):
        zt = bd(z).T
        return zt[:CHUNK] + zt[CHUNK:]

    flat = lambda ref: ref[...].reshape(rows_total, ref.shape[2])
    r = flat(r_ref)
    k = flat(k_ref)
    v = flat(v_ref)
    zwa = flat(zwa_ref)
    half_decay = -0.5 * math.exp(-0.5) * LOG2E
    th_u = jnp.tanh(0.5 * w0 + _dot(jnp.tanh(zwa), 0.5 * wd_ref[...]))
    lw = half_decay * th_u + half_decay
    th_a = jnp.tanh(0.5 * a0 + _dot(zwa, 0.5 * wa_ref[...]))
    a = 0.5 * th_a + 0.5
    kkr = k * k_k
    kk = kkr * lax.rsqrt(jnp.maximum(group_sum(kkr * kkr), 1e-24))
    half_k_a = 0.5 * k_a
    kmod = k * ((1.0 - half_k_a) + th_a * half_k_a)

    tri_r = lax.broadcasted_iota(jnp.int32, (CHUNK, CHUNK), 0)
    tri_c = lax.broadcasted_iota(jnp.int32, (CHUNK, CHUNK), 1)
    lower = jnp.where(tri_c <= tri_r, 1.0, 0.0).astype(BF16)
    lw_hi = lw.astype(BF16)
    lw_lo = (lw - lw_hi.astype(F32)).astype(BF16)
    cs = jnp.concatenate(
        [jnp.dot(lower, lw_hi[c * CHUNK:(c + 1) * CHUNK], preferred_element_type=F32)
         + jnp.dot(lower, lw_lo[c * CHUNK:(c + 1) * CHUNK], preferred_element_type=F32)
         for c in range(n_chunks)], axis=0)
    e_pos = jnp.exp2(cs)
    e_neg = jnp.exp2(-cs)
    al_full = -kk * jnp.exp2(cs - lw)
    rt_full = r * e_pos
    bt_full = kk * a * e_neg
    kt_full = kmod * e_neg

    items = [(c, p) for c in range(n_chunks) for p in range(n_pairs)]
    blk = lambda z, c, p: z[c * CHUNK:(c + 1) * CHUNK, lane_blocks[p]]
    e_tot = {it: blk(e_pos, *it)[CHUNK - 1:CHUNK, :] for it in items}
    al = {it: blk(al_full, *it) for it in items}
    rt = {it: blk(rt_full, *it) for it in items}
    bt = {it: blk(bt_full, *it) for it in items}
    kt = {it: blk(kt_full, *it) for it in items}
    v_bd = {it: bd(blk(v, *it)) for it in items}

    gram = {it: _dot_nt(jnp.concatenate([al[it], rt[it]], axis=0),
                        jnp.concatenate([bd(bt[it]), bd(kt[it])], axis=0))
            for it in items}
    a_ab = {it: jnp.where(strict, gram[it][:CHUNK, :LANES], 0.0) for it in items}
    a_kr = {it: jnp.concatenate([jnp.where(strict, gram[it][:CHUNK, LANES:], 0.0),
                                 jnp.where(incl, gram[it][CHUNK:, LANES:], 0.0)], axis=0)
            for it in items}
    a_rb = {it: jnp.where(incl, gram[it][CHUNK:, :LANES], 0.0) for it in items}

    n_levels = int(math.log2(CHUNK))
    tm = {it: jnp.where(eye, 1.0, a_ab[it]) for it in items}
    ap = {it: _dot(a_ab[it], bd(a_ab[it])) for it in items}
    for _ in range(1, n_levels - 1):
        pp = {it: _dot(jnp.concatenate([ap[it], tm[it]], axis=0), bd(ap[it]))
              for it in items}
        ap = {it: pp[it][:CHUNK] for it in items}
        tm = {it: tm[it] + pp[it][CHUNK:] for it in items}
    tm = {it: tm[it] + _dot(tm[it], bd(ap[it])) for it in items}

    av = {it: _dot(a_kr[it], v_bd[it]) for it in items}
    wu = {it: _dot(tm[it], jnp.concatenate([bd(al[it]), bd(av[it][:CHUNK])], axis=1))
          for it in items}
    wu_bd = {it: jnp.concatenate([bd(wu[it][:, :LANES]), bd(wu[it][:, LANES:])], axis=1)
             for it in items}
    ab = {it: _dot(a_rb[it], wu_bd[it]) for it in items}
    bk_t = {it: jnp.concatenate([head_transposed(bt[it] * e_tot[it]),
                                 head_transposed(kt[it] * e_tot[it])], axis=1)
            for it in items}
    gh = {it: _dot(bk_t[it], jnp.concatenate(
        [wu_bd[it], jnp.concatenate([jnp.zeros_like(v_bd[it]), v_bd[it]], axis=1)], axis=0))
          for it in items}
    gq = {it: jnp.concatenate([jnp.where(eye, e_tot[it], 0.0) + gh[it][:, :LANES],
                               rt[it] + ab[it][:, :LANES]], axis=0) for it in items}
    hm = {it: gh[it][:, LANES:] for it in items}
    y1 = {it: ab[it][:, LANES:] + av[it][CHUNK:] for it in items}

    for b in range(bb):
        for p in range(n_pairs):
            s = s_ref[b, p]
            for c in range(b * chunks_per_seq, (b + 1) * chunks_per_seq):
                it = (c, p)
                ys = _dot(gq[it], s)
                y_ref[c * CHUNK:(c + 1) * CHUNK, lane_blocks[p]] = ys[CHUNK:] + y1[it]
                s = bd(ys[:CHUNK] + hm[it])
            s_ref[b, p] = s

    y = y_ref[...]
    mu = group_sum(y) * (1.0 / HEAD)
    d = y - mu
    var = group_sum(d * d) * (1.0 / HEAD)
    yn = d * lax.rsqrt(var + RWKV_GN_EPS) * ln_w + ln_b
    bonus = group_sum(r * kmod * r_k) * v
    out = (yn + bonus) * flat(ga_ref).astype(F32)
    o_ref[...] = out.reshape(bb, tb, width).astype(BF16)


def _rwkv(r, k, v, zwa, ga, w0, w_decay_up, a0, w_iclr_up, k_k, k_a, r_k, ln_w, ln_b, *, bb, tb):
    B, T, W = r.shape
    assert B % bb == 0 and T % tb == 0 and tb % CHUNK == 0, (B, T, bb, tb)
    n_pairs = W // LANES
    zeros = jnp.zeros((LORA, W), F32)
    wd = jnp.concatenate([w_decay_up, zeros], axis=0)
    wa = jnp.concatenate([zeros, w_iclr_up], axis=0)
    vec = jnp.stack([w0, a0, k_k, k_a, r_k.reshape(W), ln_w, ln_b, jnp.zeros((W,), F32)])
    blk = pl.BlockSpec((bb, tb, W), lambda b, t: (b, t, 0))
    full = lambda a: pl.BlockSpec(a.shape, lambda b, t: (0,) * a.ndim)
    return pl.pallas_call(
        _rwkv_kernel,
        grid=(B // bb, T // tb),
        in_specs=[blk, blk, blk,
                  pl.BlockSpec((bb, tb, 2 * LORA), lambda b, t: (b, t, 0)),
                  blk, full(wd), full(wa), full(vec)],
        out_specs=blk,
        out_shape=jax.ShapeDtypeStruct((B, T, W), BF16),
        scratch_shapes=[pltpu.VMEM((bb, n_pairs, LANES, LANES), F32),
                        pltpu.VMEM((bb * tb, W), F32)],
        compiler_params=pltpu.CompilerParams(
            dimension_semantics=("arbitrary", "arbitrary"), vmem_limit_bytes=VMEM_LIMIT),
        name="rwkv7",
    )(r, k, v, zwa, ga, wd, wa, vec)


def _diff_attn_kernel(q_ref, k_ref, v_ref, gb_ref, kpos_ref, slope_ref, subg_ref, lam_ref,
                      o_ref, vt_ref, *, tq):
    seq = q_ref.shape[1]
    n_heads = q_ref.shape[2] // LANES
    n_tiles = seq // tq
    nq = 2 * tq
    head_lanes = [slice(h * LANES, (h + 1) * LANES) for h in range(n_heads)]
    for h, hl in enumerate(head_lanes):
        for r0 in range(0, seq, tq):
            vt_ref[h, :, r0:r0 + tq] = v_ref[0, r0:r0 + tq, hl].astype(F32).T.astype(BF16)

    lane = lax.broadcasted_iota(jnp.int32, (tq, LANES), 1)
    lane1 = lax.broadcasted_iota(jnp.int32, (1, LANES), 1)
    half = tq // 2
    tri = (lax.broadcasted_iota(jnp.int32, (half, tq), 0)
           <= lax.broadcasted_iota(jnp.int32, (half, tq), 1) % half)

    def alibi_lanes(h):
        slope = slope_ref[h][:, 0:1] * LOG2E
        slope_hi = slope.astype(BF16).astype(F32)
        slope_lo = slope - slope_hi
        q_bias = jnp.where(lane1 < 2, slope_hi, jnp.where(lane1 < 4, slope_lo, 0.0))
        return jnp.broadcast_to(q_bias, (nq, LANES)).astype(BF16)

    q_bias = [alibi_lanes(h) for h in range(n_heads)]

    lam_p = lam_ref[...]
    s1 = jnp.sum(lam_p[0:1] * lam_p[1:2], axis=-1, keepdims=True)
    s2 = jnp.sum(lam_p[2:3] * lam_p[3:4], axis=-1, keepdims=True)
    lam = jnp.exp(s1) - jnp.exp(s2) + LAMBDA_INIT

    def all_sublanes(z, op):
        shift = SUBLANES // 2
        while shift:
            z = op(z, pltpu.roll(z, shift, 0))
            shift //= 2
        return z

    for i in range(n_tiles):
        rows = slice(i * tq, (i + 1) * tq)
        klen = (i + 1) * tq
        kmain = klen - half
        for h, hl in enumerate(head_lanes):
            q = q_ref[0, rows, hl]
            zero = jnp.zeros_like(q)
            q0 = jnp.where(lane < HEAD, q, zero)
            q1 = jnp.where(lane >= HEAD, q, zero)
            qcat = jnp.concatenate([q0[:half], q1[:half], q0[half:], q1[half:]], axis=0)
            qcat = jnp.concatenate([qcat, q_bias[h]], axis=1)
            keys = jnp.concatenate([k_ref[0, :klen, hl], kpos_ref[:klen, :]], axis=1)
            st = lax.dot_general(keys[:kmain], qcat, (((1,), (1,)), ((), ())),
                                 preferred_element_type=F32)
            sx = lax.dot_general(keys[kmain:], qcat[tq:], (((1,), (1,)), ((), ())),
                                 preferred_element_type=F32)
            last = st[i * tq:]
            last = jnp.concatenate([jnp.where(tri, last[:, :tq], NEG_INF), last[:, tq:]], axis=1)
            st = last if i == 0 else jnp.concatenate([st[:i * tq], last], axis=0)
            sx = jnp.where(tri, sx, NEG_INF)
            st = st.reshape(kmain // SUBLANES, SUBLANES, nq)
            sx = sx.reshape(half // SUBLANES, SUBLANES, tq)
            mx = all_sublanes(jnp.max(st, axis=0), jnp.maximum)
            mx_late = jnp.maximum(mx[:, tq:], all_sublanes(jnp.max(sx, axis=0), jnp.maximum))
            mx = jnp.concatenate([mx[:, :tq], mx_late], axis=1)
            p = jnp.exp2(st - mx)
            px = jnp.exp2(sx - mx_late)
            ls = all_sublanes(jnp.sum(p, axis=0), jnp.add)
            ls = jnp.concatenate(
                [ls[:, :tq], ls[:, tq:] + all_sublanes(jnp.sum(px, axis=0), jnp.add)], axis=1)
            acc = jnp.dot(vt_ref[h, :, :kmain], p.reshape(kmain, nq).astype(BF16),
                          preferred_element_type=F32)
            acc_late = acc[:, tq:] + jnp.dot(vt_ref[h, :, kmain:klen],
                                             px.reshape(half, tq).astype(BF16),
                                             preferred_element_type=F32)
            acc = jnp.concatenate([acc[:, :tq], acc_late], axis=1)
            att = acc.reshape(LANES // SUBLANES, SUBLANES, nq) / ls
            ot = jnp.concatenate(
                [att[:, :, :half] - lam * att[:, :, half:tq],
                 att[:, :, tq:tq + half] - lam * att[:, :, tq + half:]], axis=2)
            ms = all_sublanes(jnp.sum(ot * ot, axis=0), jnp.add) * (1.0 / LANES)
            ot = (ot * lax.rsqrt(ms + SUBLN_EPS)).reshape(LANES, tq)
            o = ot.T * (subg_ref[...] * (1.0 - LAMBDA_INIT))
            o_ref[0, rows, hl] = (o * gb_ref[0, rows, hl].astype(F32)).astype(BF16)


def _diff_attn(dq, dk, dv, gb, subln_g, lam_params, *, tq, heads_per_step):
    B, T, W = dq.shape
    n_heads = W // LANES
    assert T % tq == 0 and tq % (2 * SUBLANES) == 0 and n_heads % heads_per_step == 0
    slopes = jnp.asarray([2.0 ** (-8.0 * (h + 1) / n_heads) for h in range(n_heads)], F32)
    slopes = jnp.broadcast_to(slopes[:, None, None], (n_heads, 1, LANES))
    pos = jnp.arange(T, dtype=jnp.int32)
    pos_lo = pos % KEY_POS_SPLIT
    pos_hi = pos - pos_lo
    kpos = jnp.zeros((T, LANES), F32)
    kpos = kpos.at[:, 0].set(pos_hi).at[:, 1].set(pos_lo).at[:, 2].set(pos_hi).at[:, 3].set(pos_lo)
    kpos = kpos.astype(BF16)
    hw = heads_per_step * LANES
    blk = pl.BlockSpec((1, T, hw), lambda b, h: (b, 0, h))
    subg = subln_g.reshape(1, LANES)
    return pl.pallas_call(
        functools.partial(_diff_attn_kernel, tq=tq),
        grid=(B, n_heads // heads_per_step),
        in_specs=[blk, blk, blk, blk,
                  pl.BlockSpec((T, LANES), lambda b, h: (0, 0)),
                  pl.BlockSpec((heads_per_step, 1, LANES), lambda b, h: (h, 0, 0)),
                  pl.BlockSpec((1, LANES), lambda b, h: (0, 0)),
                  pl.BlockSpec(lam_params.shape, lambda b, h: (0, 0))],
        out_specs=blk,
        out_shape=jax.ShapeDtypeStruct((B, T, W), BF16),
        scratch_shapes=[pltpu.VMEM((heads_per_step, LANES, T), BF16)],
        compiler_params=pltpu.CompilerParams(
            dimension_semantics=("arbitrary", "arbitrary"), vmem_limit_bytes=VMEM_LIMIT),
        name="diffattn",
    )(dq, dk, dv, gb, kpos, slopes, subg, lam_params)


def _out_proj_kernel(ya_ref, yb_ref, x_ref, w_ref, o_ref):
    wa = ya_ref.shape[2]
    acc = jnp.dot(ya_ref[0], w_ref[:wa, :].astype(BF16), preferred_element_type=F32)
    acc = acc + jnp.dot(yb_ref[0], w_ref[wa:, :].astype(BF16), preferred_element_type=F32)
    o_ref[0] = x_ref[0] + acc


def _out_proj(ya, yb, x, w_out, *, tm):
    B, T, D = x.shape
    assert T % tm == 0, (T, tm)
    row = lambda w: pl.BlockSpec((1, tm, w), lambda b, t: (b, t, 0))
    return pl.pallas_call(
        _out_proj_kernel,
        grid=(B, T // tm),
        in_specs=[row(ya.shape[2]), row(yb.shape[2]), row(D),
                  pl.BlockSpec(w_out.shape, lambda b, t: (0, 0))],
        out_specs=row(D),
        out_shape=jax.ShapeDtypeStruct((B, T, D), F32),
        compiler_params=pltpu.CompilerParams(
            dimension_semantics=("arbitrary", "arbitrary"), vmem_limit_bytes=VMEM_LIMIT),
        name="out_proj",
    )(ya, yb, x, w_out)


def kernel(x, norm_g, w_in, shift_mu, w0, w_decay_up, a0, w_iclr_up, k_k, k_a, r_k, ln_x_w, ln_x_b, q_norm_g, k_norm_g, lambda_q1, lambda_k1, lambda_q2, lambda_k2, subln_g, w_out):
    depth = norm_g.shape[0]
    assert depth == 1, "lambda_init is specialised to a single layer"
    h = x
    for l in range(depth):
        r, k, v, zwa, ga, dq, dk, dv, gb = _in_proj(
            h, norm_g[l], w_in[l], shift_mu[l], q_norm_g[l], k_norm_g[l], tm=PROJ_ROWS)
        ya = _rwkv(r, k, v, zwa, ga, w0[l], w_decay_up[l], a0[l], w_iclr_up[l], k_k[l], k_a[l],
                   r_k[l], ln_x_w[l], ln_x_b[l], bb=RWKV_SEQS, tb=RWKV_ROWS)
        lam_params = jnp.stack([lambda_q1[l], lambda_k1[l], lambda_q2[l], lambda_k2[l]])
        yb = _diff_attn(dq, dk, dv, gb, subln_g[l], lam_params,
                        tq=ATTN_QUERIES, heads_per_step=ATTN_HEADS)
        h = _out_proj(ya, yb, h, w_out[l], tm=PROJ_ROWS)
    return h
```

```python
import functools
import math

import jax
import jax.numpy as jnp
import numpy as np
from jax import lax
from jax.experimental import pallas as pl
from jax.experimental.pallas import tpu as pltpu

F32 = jnp.float32
BF16 = jnp.bfloat16

LANES = 128
SUBLANES = 8
MXU_WIDTH = 256
HEAD = 64
CHUNK = 64
RWKV_W = 512
DIFF_W = 512
LORA = 64
SHIFT_COLS = 3 * RWKV_W + 2 * LORA
NORM_EPS = 1e-6
QK_NORM_EPS = 1e-6
SUBLN_EPS = 1e-5
RWKV_GN_EPS = 64e-5
LAMBDA_INIT = 0.8 - 0.6 * math.exp(-0.3 * 0)
NEG_INF = -1e30
LOG2E = math.log2(math.e)
KEY_POS_SPLIT = 16
VMEM_LIMIT = 56 * 1024 * 1024

PROJ_ROWS = 1024
RWKV_SEQS, RWKV_ROWS = 8, 128
ATTN_QUERIES = 512
ATTN_HEADS = 4


def _dot(a, b):
    return jnp.dot(a.astype(BF16), b.astype(BF16), preferred_element_type=F32)


def _dot_nt(a, b):
    return lax.dot_general(a.astype(BF16), b.astype(BF16), (((1,), (1,)), ((), ())),
                           preferred_element_type=F32)


def _sigmoid(x):
    return 0.5 * jnp.tanh(0.5 * x) + 0.5


def _in_proj_kernel(x_ref, g_ref, w_ref, mu_ref, qg_ref, kg_ref, ones_ref,
                    r_ref, k_ref, v_ref, zwa_ref, ga_ref, dq_ref, dk_ref, dv_ref, gb_ref,
                    hn_ref, carry_ref):
    t = pl.program_id(1)
    x = x_ref[0]
    tm = x.shape[0]
    ms = jnp.mean(x * x, axis=-1, keepdims=True)
    hn_ref[...] = (x * lax.rsqrt(ms + NORM_EPS) * g_ref[...]).astype(BF16)

    @pl.when(t == 0)
    def _():
        carry_ref[...] = jnp.zeros_like(carry_ref)

    row0 = lax.broadcasted_iota(jnp.int32, (tm, 1), 0) == 0

    def proj(c0, c1):
        return jnp.dot(hn_ref[...], w_ref[:, c0:c1].astype(BF16), preferred_element_type=F32)

    def shifted(c0, c1):
        p = proj(c0, c1)
        prev = jnp.where(row0, carry_ref[:, c0:c1], pltpu.roll(p, 1, 0))
        carry_ref[:, c0:c1] = p[tm - 1:tm, :]
        return p + (prev - p) * mu_ref[:, c0:c1]

    r_ref[0] = shifted(0, RWKV_W)
    k_ref[0] = shifted(RWKV_W, 2 * RWKV_W)
    v_ref[0] = shifted(2 * RWKV_W, 3 * RWKV_W)
    zwa_ref[0] = shifted(3 * RWKV_W, SHIFT_COLS)

    c = SHIFT_COLS
    g = proj(c, c + RWKV_W)
    ga_ref[0] = (g * _sigmoid(g)).astype(BF16)
    c += RWKV_W

    def qk_norm(p, gain):
        sq = (p * p).astype(BF16)
        w = ones_ref.shape[0]
        ms = jnp.concatenate(
            [jnp.dot(sq[:, c0:c0 + w], ones_ref[...], preferred_element_type=F32)
             for c0 in range(0, DIFF_W, w)], axis=1) * (1.0 / HEAD)
        return (p * lax.rsqrt(ms + QK_NORM_EPS) * gain).astype(BF16)

    dq_ref[0] = qk_norm(proj(c, c + DIFF_W), qg_ref[...] * (HEAD ** -0.5 * LOG2E))
    c += DIFF_W
    dk_ref[0] = qk_norm(proj(c, c + DIFF_W), kg_ref[...])
    c += DIFF_W
    dv_ref[0] = proj(c, c + DIFF_W).astype(BF16)
    c += DIFF_W
    g = proj(c, c + DIFF_W)
    gb_ref[0] = (g * _sigmoid(g)).astype(BF16)


def _in_proj(x, norm_g, w_in, shift_mu, q_norm_g, k_norm_g, *, tm):
    B, T, D = x.shape
    assert T % tm == 0, (T, tm)
    n_rep = DIFF_W // HEAD
    qg = jnp.tile(q_norm_g.reshape(1, HEAD), (1, n_rep))
    kg = jnp.tile(k_norm_g.reshape(1, HEAD), (1, n_rep))
    row = lambda w: pl.BlockSpec((1, tm, w), lambda b, t: (b, t, 0))
    full = lambda a: pl.BlockSpec(a.shape, lambda b, t: (0,) * a.ndim,
                                  pipeline_mode=pl.Buffered(1))
    f32o = lambda w: jax.ShapeDtypeStruct((B, T, w), F32)
    bf16o = lambda w: jax.ShapeDtypeStruct((B, T, w), BF16)
    g2 = norm_g.reshape(1, D)
    mu2 = shift_mu.reshape(1, SHIFT_COLS)
    group = np.arange(MXU_WIDTH) // HEAD
    ones_bd = jnp.asarray(group[:, None] == group[None, :], BF16)
    return pl.pallas_call(
        _in_proj_kernel,
        grid=(B, T // tm),
        in_specs=[row(D), full(g2), full(w_in), full(mu2), full(qg), full(kg), full(ones_bd)],
        out_specs=[row(RWKV_W), row(RWKV_W), row(RWKV_W), row(2 * LORA), row(RWKV_W),
                   row(DIFF_W), row(DIFF_W), row(DIFF_W), row(DIFF_W)],
        out_shape=[f32o(RWKV_W), f32o(RWKV_W), f32o(RWKV_W), f32o(2 * LORA), bf16o(RWKV_W),
                   bf16o(DIFF_W), bf16o(DIFF_W), bf16o(DIFF_W), bf16o(DIFF_W)],
        scratch_shapes=[pltpu.VMEM((tm, D), BF16), pltpu.VMEM((1, SHIFT_COLS), F32)],
        compiler_params=pltpu.CompilerParams(
            dimension_semantics=("arbitrary", "arbitrary"), vmem_limit_bytes=VMEM_LIMIT),
        name="in_proj",
    )(x, g2, w_in, mu2, qg, kg, ones_bd)


def _rwkv_kernel(r_ref, k_ref, v_ref, zwa_ref, ga_ref, wd_ref, wa_ref, vec_ref,
                 o_ref, s_ref, y_ref):
    bb, tb, width = r_ref.shape
    rows_total = bb * tb
    chunks_per_seq = tb // CHUNK
    n_chunks = bb * chunks_per_seq
    n_pairs = width // LANES
    P2 = 2 * CHUNK

    @pl.when(pl.program_id(1) == 0)
    def _():
        s_ref[...] = jnp.zeros_like(s_ref)

    w0 = vec_ref[0:1, :]
    a0 = vec_ref[1:2, :]
    k_k = vec_ref[2:3, :]
    k_a = vec_ref[3:4, :]
    r_k = vec_ref[4:5, :]
    ln_w = vec_ref[5:6, :]
    ln_b = vec_ref[6:7, :]

    ri = lax.broadcasted_iota(jnp.int32, (P2, LANES), 0)
    ci = lax.broadcasted_iota(jnp.int32, (P2, LANES), 1)
    same_head = (ri // HEAD) == (ci // HEAD)
    ones_bd = jnp.where(same_head, 1.0, 0.0).astype(BF16)
    ti = lax.broadcasted_iota(jnp.int32, (CHUNK, LANES), 0)
    si = lax.broadcasted_iota(jnp.int32, (CHUNK, LANES), 1) % HEAD
    strict = si < ti
    incl = si <= ti
    eye = si == ti
    lane_blocks = [slice(p * LANES, (p + 1) * LANES) for p in range(n_pairs)]

    def group_sum(z):
        zb = z.astype(BF16)
        return jnp.concatenate(
            [jnp.dot(zb[:, lb], ones_bd, preferred_element_type=F32) for lb in lane_blocks],
            axis=1)

    def bd(z):
        return jnp.where(same_head, jnp.concatenate([z, z], axis=0), 0.0)

    def head_transposed(z):
        zt = bd(z).T
        return zt[:CHUNK] + zt[CHUNK:]

    flat = lambda ref: ref[...].reshape(rows_total, ref.shape[2])
    r = flat(r_ref)
    k = flat(k_ref)
    v = flat(v_ref)
    zwa = flat(zwa_ref)
    half_decay = -0.5 * math.exp(-0.5) * LOG2E
    th_u = jnp.tanh(0.5 * w0 + _dot(jnp.tanh(zwa), 0.5 * wd_ref[...]))
    lw = half_decay * th_u + half_decay
    th_a = jnp.tanh(0.5 * a0 + _dot(zwa, 0.5 * wa_ref[...]))
    a = 0.5 * th_a + 0.5
    kkr = k * k_k
    kk = kkr * lax.rsqrt(jnp.maximum(group_sum(kkr * kkr), 1e-24))
    half_k_a = 0.5 * k_a
    kmod = k * ((1.0 - half_k_a) + th_a * half_k_a)

    tri_r = lax.broadcasted_iota(jnp.int32, (CHUNK, CHUNK), 0)
    tri_c = lax.broadcasted_iota(jnp.int32, (CHUNK, CHUNK), 1)
    lower = jnp.where(tri_c <= tri_r, 1.0, 0.0).astype(BF16)
    lw_hi = lw.astype(BF16)
    lw_lo = (lw - lw_hi.astype(F32)).astype(BF16)
    cs = jnp.concatenate(
        [jnp.dot(lower, lw_hi[c * CHUNK:(c + 1) * CHUNK], preferred_element_type=F32)
         + jnp.dot(lower, lw_lo[c * CHUNK:(c + 1) * CHUNK], preferred_element_type=F32)
         for c in range(n_chunks)], axis=0)
    e_pos = jnp.exp2(cs)
    e_neg = jnp.exp2(-cs)
    al_full = -kk * jnp.exp2(cs - lw)
    rt_full = r * e_pos
    bt_full = kk * a * e_neg
    kt_full = kmod * e_neg

    items = [(c, p) for c in range(n_chunks) for p in range(n_pairs)]
    blk = lambda z, c, p: z[c * CHUNK:(c + 1) * CHUNK, lane_blocks[p]]
    e_tot = {it: blk(e_pos, *it)[CHUNK - 1:CHUNK, :] for it in items}
    al = {it: blk(al_full, *it) for it in items}
    rt = {it: blk(rt_full, *it) for it in items}
    bt = {it: blk(bt_full, *it) for it in items}
    kt = {it: blk(kt_full, *it) for it in items}
    v_bd = {it: bd(blk(v, *it)) for it in items}

    gram = {it: _dot_nt(jnp.concatenate([al[it], rt[it]], axis=0),
                        jnp.concatenate([bd(bt[it]), bd(kt[it])], axis=0))
            for it in items}
    a_ab = {it: jnp.where(strict, gram[it][:CHUNK, :LANES], 0.0) for it in items}
    a_kr = {it: jnp.concatenate([jnp.where(strict, gram[it][:CHUNK, LANES:], 0.0),
                                 jnp.where(incl, gram[it][CHUNK:, LANES:], 0.0)], axis=0)
            for it in items}
    a_rb = {it: jnp.where(incl, gram[it][CHUNK:, :LANES], 0.0) for it in items}

    n_levels = int(math.log2(CHUNK))
    tm = {it: jnp.where(eye, 1.0, a_ab[it]) for it in items}
    ap = {it: _dot(a_ab[it], bd(a_ab[it])) for it in items}
    for _ in range(1, n_levels - 1):
        pp = {it: _dot(jnp.concatenate([ap[it], tm[it]], axis=0), bd(ap[it]))
              for it in items}
        ap = {it: pp[it][:CHUNK] for it in items}
        tm = {it: tm[it] + pp[it][CHUNK:] for it in items}
    tm = {it: tm[it] + _dot(tm[it], bd(ap[it])) for it in items}

    av = {it: _dot(a_kr[it], v_bd[it]) for it in items}
    wu = {it: _dot(tm[it], jnp.concatenate([bd(al[it]), bd(av[it][:CHUNK])], axis=1))
          for it in items}
    wu_bd = {it: jnp.concatenate([bd(wu[it][:, :LANES]), bd(wu[it][:, LANES:])], axis=1)
             for it in items}
    ab = {it: _dot(a_rb[it], wu_bd[it]) for it in items}
    bk_t = {it: jnp.concatenate([head_transposed(bt[it] * e_tot[it]),
                                 head_transposed(kt[it] * e_tot[it])], axis=1)
            for it in items}
    gh = {it: _dot(bk_t[it], jnp.concatenate(
        [wu_bd[it], jnp.concatenate([jnp.zeros_like(v_bd[it]), v_bd[it]], axis=1)], axis=0))
          for it in items}
    gq = {it: jnp.concatenate([jnp.where(eye, e_tot[it], 0.0) + gh[it][:, :LANES],
                               rt[it] + ab[it][:, :LANES]], axis=0) for it in items}
    hm = {it: gh[it][:, LANES:] for it in items}
    y1 = {it: ab[it][:, LANES:] + av[it][CHUNK:] for it in items}

    for b in range(bb):
        for p in range(n_pairs):
            s = s_ref[b, p]
            for c in range(b * chunks_per_seq, (b + 1) * chunks_per_seq):
                it = (c, p)
                ys = _dot(gq[it], s)
                y_ref[c * CHUNK:(c + 1) * CHUNK, lane_blocks[p]] = ys[CHUNK:] + y1[it]
                s = bd(ys[:CHUNK] + hm[it])
            s_ref[b, p] = s

    y = y_ref[...]
    mu = group_sum(y) * (1.0 / HEAD)
    d = y - mu
    var = group_sum(d * d) * (1.0 / HEAD)
    yn = d * lax.rsqrt(var + RWKV_GN_EPS) * ln_w + ln_b
    bonus = group_sum(r * kmod * r_k) * v
    out = (yn + bonus) * flat(ga_ref).astype(F32)
    o_ref[...] = out.reshape(bb, tb, width).astype(BF16)


def _rwkv(r, k, v, zwa, ga, w0, w_decay_up, a0, w_iclr_up, k_k, k_a, r_k, ln_w, ln_b, *, bb, tb):
    B, T, W = r.shape
    assert B % bb == 0 and T % tb == 0 and tb % CHUNK == 0, (B, T, bb, tb)
    n_pairs = W // LANES
    zeros = jnp.zeros((LORA, W), F32)
    wd = jnp.concatenate([w_decay_up, zeros], axis=0)
    wa = jnp.concatenate([zeros, w_iclr_up], axis=0)
    vec = jnp.stack([w0, a0, k_k, k_a, r_k.reshape(W), ln_w, ln_b, jnp.zeros((W,), F32)])
    blk = pl.BlockSpec((bb, tb, W), lambda b, t: (b, t, 0))
    full = lambda a: pl.BlockSpec(a.shape, lambda b, t: (0,) * a.ndim)
    return pl.pallas_call(
        _rwkv_kernel,
        grid=(B // bb, T // tb),
        in_specs=[blk, blk, blk,
                  pl.BlockSpec((bb, tb, 2 * LORA), lambda b, t: (b, t, 0)),
                  blk, full(wd), full(wa), full(vec)],
        out_specs=blk,
        out_shape=jax.ShapeDtypeStruct((B, T, W), BF16),
        scratch_shapes=[pltpu.VMEM((bb, n_pairs, LANES, LANES), F32),
                        pltpu.VMEM((bb * tb, W), F32)],
        compiler_params=pltpu.CompilerParams(
            dimension_semantics=("arbitrary", "arbitrary"), vmem_limit_bytes=VMEM_LIMIT),
        name="rwkv7",
    )(r, k, v, zwa, ga, wd, wa, vec)


def _diff_attn_kernel(q_ref, k_ref, v_ref, gb_ref, kpos_ref, slope_ref, subg_ref, lam_ref,
                      o_ref, vt_ref, *, tq):
    seq = q_ref.shape[1]
    n_heads = q_ref.shape[2] // LANES
    n_tiles = seq // tq
    nq = 2 * tq
    head_lanes = [slice(h * LANES, (h + 1) * LANES) for h in range(n_heads)]
    for h, hl in enumerate(head_lanes):
        for r0 in range(0, seq, tq):
            vt_ref[h, :, r0:r0 + tq] = v_ref[0, r0:r0 + tq, hl].astype(F32).T.astype(BF16)

    lane = lax.broadcasted_iota(jnp.int32, (tq, LANES), 1)
    lane1 = lax.broadcasted_iota(jnp.int32, (1, LANES), 1)
    half = tq // 2
    tri = (lax.broadcasted_iota(jnp.int32, (half, tq), 0)
           <= lax.broadcasted_iota(jnp.int32, (half, tq), 1) % half)

    def alibi_lanes(h):
        slope = slope_ref[h][:, 0:1] * LOG2E
        slope_hi = slope.astype(BF16).astype(F32)
        slope_lo = slope - slope_hi
        q_bias = jnp.where(lane1 < 2, slope_hi, jnp.where(lane1 < 4, slope_lo, 0.0))
        return jnp.broadcast_to(q_bias, (nq, LANES)).astype(BF16)

    q_bias = [alibi_lanes(h) for h in range(n_heads)]

    lam_p = lam_ref[...]
    s1 = jnp.sum(lam_p[0:1] * lam_p[1:2], axis=-1, keepdims=True)
    s2 = jnp.sum(lam_p[2:3] * lam_p[3:4], axis=-1, keepdims=True)
    lam = jnp.exp(s1) - jnp.exp(s2) + LAMBDA_INIT

    def all_sublanes(z, op):
        shift = SUBLANES // 2
        while shift:
            z = op(z, pltpu.roll(z, shift, 0))
            shift //= 2
        return z

    for i in range(n_tiles):
        rows = slice(i * tq, (i + 1) * tq)
        klen = (i + 1) * tq
        kmain = klen - half
        for h, hl in enumerate(head_lanes):
            q = q_ref[0, rows, hl]
            zero = jnp.zeros_like(q)
            q0 = jnp.where(lane < HEAD, q, zero)
            q1 = jnp.where(lane >= HEAD, q, zero)
            qcat = jnp.concatenate([q0[:half], q1[:half], q0[half:], q1[half:]], axis=0)
            qcat = jnp.concatenate([qcat, q_bias[h]], axis=1)
            keys = jnp.concatenate([k_ref[0, :klen, hl], kpos_ref[:klen, :]], axis=1)
            st = lax.dot_general(keys[:kmain], qcat, (((1,), (1,)), ((), ())),
                                 preferred_element_type=F32)
            sx = lax.dot_general(keys[kmain:], qcat[tq:], (((1,), (1,)), ((), ())),
                                 preferred_element_type=F32)
            last = st[i * tq:]
            last = jnp.concatenate([jnp.where(tri, last[:, :tq], NEG_INF), last[:, tq:]], axis=1)
            st = last if i == 0 else jnp.concatenate([st[:i * tq], last], axis=0)
            sx = jnp.where(tri, sx, NEG_INF)
            st = st.reshape(kmain // SUBLANES, SUBLANES, nq)
            sx = sx.reshape(half // SUBLANES, SUBLANES, tq)
            mx = all_sublanes(jnp.max(st, axis=0), jnp.maximum)
            mx_late = jnp.maximum(mx[:, tq:], all_sublanes(jnp.max(sx, axis=0), jnp.maximum))
            mx = jnp.concatenate([mx[:, :tq], mx_late], axis=1)
            p = jnp.exp2(st - mx)
            px = jnp.exp2(sx - mx_late)
            ls = all_sublanes(jnp.sum(p, axis=0), jnp.add)
            ls = jnp.concatenate(
                [ls[:, :tq], ls[:, tq:] + all_sublanes(jnp.sum(px, axis=0), jnp.add)], axis=1)
            acc = jnp.dot(vt_ref[h, :, :kmain], p.reshape(kmain, nq).astype(BF16),
                          preferred_element_type=F32)
            acc_late = acc[:, tq:] + jnp.dot(vt_ref[h, :, kmain:klen],
                                             px.reshape(half, tq).astype(BF16),
                                             preferred_element_type=F32)
            acc = jnp.concatenate([acc[:, :tq], acc_late], axis=1)
            att = acc.reshape(LANES // SUBLANES, SUBLANES, nq) / ls
            ot = jnp.concatenate(
                [att[:, :, :half] - lam * att[:, :, half:tq],
                 att[:, :, tq:tq + half] - lam * att[:, :, tq + half:]], axis=2)
            ms = all_sublanes(jnp.sum(ot * ot, axis=0), jnp.add) * (1.0 / LANES)
            ot = (ot * lax.rsqrt(ms + SUBLN_EPS)).reshape(LANES, tq)
            o = ot.T * (subg_ref[...] * (1.0 - LAMBDA_INIT))
            o_ref[0, rows, hl] = (o * gb_ref[0, rows, hl].astype(F32)).astype(BF16)


def _diff_attn(dq, dk, dv, gb, subln_g, lam_params, *, tq, heads_per_step):
    B, T, W = dq.shape
    n_heads = W // LANES
    assert T % tq == 0 and tq % (2 * SUBLANES) == 0 and n_heads % heads_per_step == 0
    slopes = jnp.asarray([2.0 ** (-8.0 * (h + 1) / n_heads) for h in range(n_heads)], F32)
    slopes = jnp.broadcast_to(slopes[:, None, None], (n_heads, 1, LANES))
    pos = jnp.arange(T, dtype=jnp.int32)
    pos_lo = pos % KEY_POS_SPLIT
    pos_hi = pos - pos_lo
    kpos = jnp.zeros((T, LANES), F32)
    kpos = kpos.at[:, 0].set(pos_hi).at[:, 1].set(pos_lo).at[:, 2].set(pos_hi).at[:, 3].set(pos_lo)
    kpos = kpos.astype(BF16)
    hw = heads_per_step * LANES
    blk = pl.BlockSpec((1, T, hw), lambda b, h: (b, 0, h))
    subg = subln_g.reshape(1, LANES)
    return pl.pallas_call(
        functools.partial(_diff_attn_kernel, tq=tq),
        grid=(B, n_heads // heads_per_step),
        in_specs=[blk, blk, blk, blk,
                  pl.BlockSpec((T, LANES), lambda b, h: (0, 0)),
                  pl.BlockSpec((heads_per_step, 1, LANES), lambda b, h: (h, 0, 0)),
                  pl.BlockSpec((1, LANES), lambda b, h: (0, 0)),
                  pl.BlockSpec(lam_params.shape, lambda b, h: (0, 0))],
        out_specs=blk,
        out_shape=jax.ShapeDtypeStruct((B, T, W), BF16),
        scratch_shapes=[pltpu.VMEM((heads_per_step, LANES, T), BF16)],
        compiler_params=pltpu.CompilerParams(
            dimension_semantics=("arbitrary", "arbitrary"), vmem_limit_bytes=VMEM_LIMIT),
        name="diffattn",
    )(dq, dk, dv, gb, kpos, slopes, subg, lam_params)


def _out_proj_kernel(ya_ref, yb_ref, x_ref, w_ref, o_ref):
    mixed = jnp.concatenate([ya_ref[0], yb_ref[0]], axis=1)
    o_ref[0] = x_ref[0] + jnp.dot(mixed, w_ref[...].astype(BF16), preferred_element_type=F32)


def _out_proj(ya, yb, x, w_out, *, tm):
    B, T, D = x.shape
    assert T % tm == 0, (T, tm)
    row = lambda w: pl.BlockSpec((1, tm, w), lambda b, t: (b, t, 0))
    return pl.pallas_call(
        _out_proj_kernel,
        grid=(B, T // tm),
        in_specs=[row(ya.shape[2]), row(yb.shape[2]), row(D),
                  pl.BlockSpec(w_out.shape, lambda b, t: (0, 0))],
        out_specs=row(D),
        out_shape=jax.ShapeDtypeStruct((B, T, D), F32),
        compiler_params=pltpu.CompilerParams(
            dimension_semantics=("arbitrary", "arbitrary"), vmem_limit_bytes=VMEM_LIMIT),
        name="out_proj",
    )(ya, yb, x, w_out)


def kernel(x, norm_g, w_in, shift_mu, w0, w_decay_up, a0, w_iclr_up, k_k, k_a, r_k, ln_x_w, ln_x_b, q_norm_g, k_norm_g, lambda_q1, lambda_k1, lambda_q2, lambda_k2, subln_g, w_out):
    depth = norm_g.shape[0]
    assert depth == 1, "lambda_init is specialised to a single layer"
    h = x
    for l in range(depth):
        r, k, v, zwa, ga, dq, dk, dv, gb = _in_proj(
            h, norm_g[l], w_in[l], shift_mu[l], q_norm_g[l], k_norm_g[l], tm=PROJ_ROWS)
        ya = _rwkv(r, k, v, zwa, ga, w0[l], w_decay_up[l], a0[l], w_iclr_up[l], k_k[l], k_a[l],
                   r_k[l], ln_x_w[l], ln_x_b[l], bb=RWKV_SEQS, tb=RWKV_ROWS)
        lam_params = jnp.stack([lambda_q1[l], lambda_k1[l], lambda_q2[l], lambda_k2[l]])
        yb = _diff_attn(dq, dk, dv, gb, subln_g[l], lam_params,
                        tq=ATTN_QUERIES, heads_per_step=ATTN_HEADS)
        h = _out_proj(ya, yb, h, w_out[l], tm=PROJ_ROWS)
    return h
```
